```python
import jax, jax.numpy as jnp
from jax import lax
import numpy as np

D_MODEL = 1024
BATCH = 4
SEQ = 4096
DEPTH = 2

CHUNK = 64
EPS = 1e-6

HEAD_DIM = 64
ATTN_WIDTH = D_MODEL // 2
ATTN_HEADS = ATTN_WIDTH // HEAD_DIM
ROPE_DIM = HEAD_DIM // 4
ROPE_THETA = 500000.0
IDX_HEADS = 4
IDX_DIM = 64
IDX_SCALE = (IDX_DIM ** -0.5) * (IDX_HEADS ** -0.5)
TOPK_MAX = 256
QBLOCK = 64

CONV_WIDTH = D_MODEL // 2
CONV_GROUPS = 8
CONV_K = 3

N_BRANCH = 2
MLP_HIDDEN = 4 * D_MODEL

PROJ_SIZES = (ATTN_WIDTH, ATTN_WIDTH, ATTN_WIDTH, IDX_HEADS * IDX_DIM, IDX_DIM, IDX_HEADS,
              CONV_WIDTH, CONV_WIDTH, CONV_WIDTH, N_BRANCH * D_MODEL)
IN_COLS = sum(PROJ_SIZES)

kernel_name = "hybrid_dsa_shortconv_gated_trunk"


def rmsnorm(x, g):
    xf = x.astype(jnp.float32)
    y = xf * lax.rsqrt(jnp.mean(xf * xf, axis=-1, keepdims=True) + EPS)
    return (y * g.astype(jnp.float32)).astype(x.dtype)


def rope_tables(T):
    inv = 1.0 / (ROPE_THETA ** (jnp.arange(0, ROPE_DIM, 2, dtype=jnp.float32) / ROPE_DIM))
    ang = jnp.arange(T, dtype=jnp.float32)[:, None] * inv[None, :]
    return jnp.cos(ang), jnp.sin(ang)


def apply_partial_rope(x, cos, sin):
    half = ROPE_DIM // 2
    xr = x[..., :ROPE_DIM].astype(jnp.float32)
    x1, x2 = xr[..., :half], xr[..., half:]
    c = cos[None, :, None, :]
    s = sin[None, :, None, :]
    rot = jnp.concatenate([x1 * c - x2 * s, x2 * c + x1 * s], axis=-1)
    return jnp.concatenate([rot.astype(x.dtype), x[..., ROPE_DIM:]], axis=-1)


def dsa_attention(q, k, v, iq, ik, iw):
    B, T, H, Dh = q.shape
    topk = min(TOPK_MAX, T // 4)
    nblk = T // QBLOCK
    key_pos = jnp.arange(T)
    scale = Dh ** -0.5
    ikf = ik.astype(jnp.float32)

    def block(i):
        t0 = i * QBLOCK
        qb = lax.dynamic_slice_in_dim(q, t0, QBLOCK, axis=1)
        iqb = lax.dynamic_slice_in_dim(iq, t0, QBLOCK, axis=1).astype(jnp.float32)
        iwb = lax.dynamic_slice_in_dim(iw, t0, QBLOCK, axis=1).astype(jnp.float32)
        tpos = t0 + jnp.arange(QBLOCK)
        limit = (tpos // CHUNK + 1) * CHUNK
        admissible = key_pos[None, :] < limit[:, None]
        logits = jnp.einsum('bthd,bsd->bths', iqb, ikf)
        score = jnp.einsum('bths,bth->bts', jax.nn.relu(logits), iwb) * IDX_SCALE
        score = jnp.where(admissible[None], score, -jnp.inf)
        _, idx = lax.top_k(score, topk)
        valid = idx < limit[None, :, None]
        kg = jax.vmap(lambda kk, ii: kk[ii])(k, idx)
        vg = jax.vmap(lambda vv, ii: vv[ii])(v, idx)
        s = jnp.einsum('bthd,btkhd->bthk', qb, kg).astype(jnp.float32) * scale
        s = jnp.where(valid[:, :, None, :], s, -jnp.inf)
        p = jax.nn.softmax(s, axis=-1).astype(v.dtype)
        return jnp.einsum('bthk,btkhd->bthd', p, vg)

    out = lax.map(block, jnp.arange(nblk))
    return out.transpose(1, 0, 2, 3, 4).reshape(B, T, H * Dh)


def short_conv(xc, conv_w):
    T = xc.shape[1]
    xp = jnp.pad(xc, ((0, 0), (CONV_K - 1, 0), (0, 0)))
    out = xp[:, 0:T] * conv_w[0]
    for j in range(1, CONV_K):
        out = out + xp[:, j:j + T] * conv_w[j]
    return out


def setup_inputs(seed: int = 0) -> dict:
    key = jax.random.key(seed)
    ks = jax.random.split(key, 12)
    f32 = jnp.float32
    x = jax.random.normal(ks[0], (BATCH, SEQ, D_MODEL), f32)
    norm_mix = 1.0 + 0.02 * jax.random.normal(ks[1], (DEPTH, D_MODEL), f32)
    w_in = jax.random.normal(ks[2], (DEPTH, D_MODEL, IN_COLS), f32) * D_MODEL ** -0.5
    conv_w = jax.random.normal(ks[3], (DEPTH, CONV_K, CONV_WIDTH), f32) * CONV_K ** -0.5
    w_attn_out = jax.random.normal(ks[4], (DEPTH, ATTN_WIDTH, D_MODEL), f32) * ATTN_WIDTH ** -0.5
    w_conv_out = jax.random.normal(ks[5], (DEPTH, CONV_WIDTH, D_MODEL), f32) * CONV_WIDTH ** -0.5
    w_mix_out = jax.random.normal(ks[6], (DEPTH, D_MODEL, D_MODEL), f32) * D_MODEL ** -0.5
    norm_mlp = 1.0 + 0.02 * jax.random.normal(ks[7], (DEPTH, D_MODEL), f32)
    w_mlp_up = jax.random.normal(ks[8], (DEPTH, D_MODEL, MLP_HIDDEN), f32) * D_MODEL ** -0.5
    w_mlp_down = jax.random.normal(ks[9], (DEPTH, MLP_HIDDEN, D_MODEL), f32) * MLP_HIDDEN ** -0.5
    norm_final = 1.0 + 0.02 * jax.random.normal(ks[10], (D_MODEL,), f32)
    return {"x": x, "norm_mix": norm_mix, "w_in": w_in, "conv_w": conv_w,
            "w_attn_out": w_attn_out, "w_conv_out": w_conv_out, "w_mix_out": w_mix_out,
            "norm_mlp": norm_mlp, "w_mlp_up": w_mlp_up, "w_mlp_down": w_mlp_down,
            "norm_final": norm_final}


def reference(x, norm_mix, w_in, conv_w, w_attn_out, w_conv_out, w_mix_out,
              norm_mlp, w_mlp_up, w_mlp_down, norm_final):
    B, T, D = x.shape
    cos, sin = rope_tables(T)
    offsets = np.cumsum(PROJ_SIZES)[:-1].tolist()
    for l in range(DEPTH):
        u = rmsnorm(x, norm_mix[l])
        proj = u @ w_in[l]
        q, k, v, iq, ik, iw, cB, cC, ch, g = jnp.split(proj, offsets, axis=-1)
        q = apply_partial_rope(q.reshape(B, T, ATTN_HEADS, HEAD_DIM), cos, sin)
        k = apply_partial_rope(k.reshape(B, T, ATTN_HEADS, HEAD_DIM), cos, sin)
        v = v.reshape(B, T, ATTN_HEADS, HEAD_DIM)
        iq = apply_partial_rope(iq.reshape(B, T, IDX_HEADS, IDX_DIM), cos, sin)
        ik = apply_partial_rope(ik[:, :, None, :], cos, sin)[:, :, 0, :]
        y_attn = dsa_attention(q, k, v, iq, ik, iw) @ w_attn_out[l]
        y_conv = (cB * short_conv(cC * ch, conv_w[l])) @ w_conv_out[l]
        gates = jax.nn.sigmoid(g.reshape(B, T, N_BRANCH, D))
        merged = gates[:, :, 0] * y_attn + gates[:, :, 1] * y_conv
        x = x + merged @ w_mix_out[l]
        u = rmsnorm(x, norm_mlp[l])
        x = x + jnp.square(jax.nn.relu(u @ w_mlp_up[l])) @ w_mlp_down[l]
    return rmsnorm(x, norm_final)
```

```python
import functools

import jax
import jax.numpy as jnp
import numpy as np
from jax import lax
from jax.experimental import pallas as pl
from jax.experimental.pallas import tpu as pltpu

F32 = jnp.float32
BF16 = jnp.bfloat16

D_MODEL = 1024
CHUNK = 64
EPS = 1e-6
HEAD_DIM = 64
ATTN_WIDTH = 512
ATTN_HEADS = 8
ROPE_DIM = 16
ROPE_THETA = 500000.0
IDX_HEADS = 4
IDX_DIM = 64
IDX_SCALE = (IDX_DIM ** -0.5) * (IDX_HEADS ** -0.5)
TOPK_MAX = 256
CONV_WIDTH = 512
CONV_K = 3
MLP_HIDDEN = 4 * D_MODEL

LANES = 128
SUBLANES = 8

OFF_Q = 0
OFF_K = OFF_Q + ATTN_WIDTH
OFF_V = OFF_K + ATTN_WIDTH
OFF_IQ = OFF_V + ATTN_WIDTH
OFF_IKW = OFF_IQ + IDX_HEADS * LANES
IW_LANE = 96
OFF_CB = OFF_IKW + LANES
OFF_CC = OFF_CB + CONV_WIDTH
OFF_CH = OFF_CC + CONV_WIDTH
OFF_G = OFF_CH + CONV_WIDTH
PROJ_COLS = OFF_G + 2 * D_MODEL

TM_PROJ = 256
TQ = 256
TK = 256
TM_POST = 256
MLP_CHUNK = 1024
SEARCH_PASSES = 4
MAX_ROUNDS = 96
NEG_BIAS = -2e30
M_INIT = -1e30
VMEM_LIMIT = 48 * 1024 * 1024


def _rmsnorm(x, g):
    ms = jnp.mean(x * x, axis=-1, keepdims=True)
    return x * lax.rsqrt(ms + EPS) * g


def _proj_kernel(x_ref, g_ref, w_ref, cos_ref, sa_ref, sb_ref, cw_ref,
                 qT_ref, k_ref, vT_ref, iqT_ref, ikw_ref, ikwT_ref, uc_ref, sg_ref,
                 zbuf, *, tiles_per_seq):
    i = pl.program_id(0)
    tm = x_ref.shape[0]
    u = _rmsnorm(x_ref[...], g_ref[...]).astype(BF16)
    cosv = cos_ref[...]
    sa = sa_ref[...]
    sb = sb_ref[...]

    def proj(c0, n):
        return jnp.dot(u, w_ref[:, c0:c0 + n], preferred_element_type=F32)

    def rope(a):
        outs = []
        for gidx in range(a.shape[1] // LANES):
            ag = a[:, gidx * LANES:(gidx + 1) * LANES]
            outs.append(ag * cosv
                        + pltpu.roll(ag, LANES - ROPE_DIM // 2, 1) * sa
                        + pltpu.roll(ag, ROPE_DIM // 2, 1) * sb)
        return outs[0] if len(outs) == 1 else jnp.concatenate(outs, axis=1)

    q = rope(proj(OFF_Q, ATTN_WIDTH)) * (HEAD_DIM ** -0.5)
    qT_ref[0] = q.T.astype(BF16)
    k_ref[0] = rope(proj(OFF_K, ATTN_WIDTH)).astype(BF16)
    vT_ref[0, 0] = proj(OFF_V, ATTN_WIDTH).T.astype(BF16)
    iqT_ref[0] = rope(proj(OFF_IQ, IDX_HEADS * LANES)).T.astype(BF16)
    ikw = rope(proj(OFF_IKW, LANES))
    ikw_ref[0] = ikw.astype(BF16)
    ikwT_ref[0] = ikw.T

    z = proj(OFF_CC, CONV_WIDTH) * proj(OFF_CH, CONV_WIDTH)

    @pl.when(i % tiles_per_seq == 0)
    def _():
        zbuf[0:SUBLANES, :] = jnp.zeros((SUBLANES, CONV_WIDTH), F32)

    @pl.when(i % tiles_per_seq != 0)
    def _():
        zbuf[0:SUBLANES, :] = zbuf[tm:tm + SUBLANES, :]

    zbuf[SUBLANES:SUBLANES + tm, :] = z
    z1 = zbuf[SUBLANES - 1:SUBLANES - 1 + tm, :]
    z2 = zbuf[SUBLANES - 2:SUBLANES - 2 + tm, :]
    conv = z2 * cw_ref[0:1, :] + z1 * cw_ref[1:2, :] + z * cw_ref[2:3, :]
    uc_ref[...] = (proj(OFF_CB, CONV_WIDTH) * conv).astype(BF16)

    for c in range(4):
        gc = proj(OFF_G + c * 512, 512)
        sg_ref[:, c * 512:(c + 1) * 512] = jax.nn.sigmoid(gc).astype(BF16)


def _proj_call(x2d, gain, w, cosv, sa, sb, cw, *, batch, seq):
    m = x2d.shape[0]
    tm = TM_PROJ
    nt = seq // tm
    const = lambda i: (0, 0)
    out_shape = (
        jax.ShapeDtypeStruct((batch, ATTN_WIDTH, seq), BF16),
        jax.ShapeDtypeStruct((batch, seq, ATTN_WIDTH), BF16),
        jax.ShapeDtypeStruct((batch, nt, ATTN_WIDTH, tm), BF16),
        jax.ShapeDtypeStruct((batch, IDX_HEADS * LANES, seq), BF16),
        jax.ShapeDtypeStruct((batch, seq, LANES), BF16),
        jax.ShapeDtypeStruct((batch, LANES, seq), F32),
        jax.ShapeDtypeStruct((m, CONV_WIDTH), BF16),
        jax.ShapeDtypeStruct((m, 2 * D_MODEL), BF16),
    )
    in_specs = [
        pl.BlockSpec((tm, D_MODEL), lambda i: (i, 0)),
        pl.BlockSpec((1, D_MODEL), const),
        pl.BlockSpec((D_MODEL, PROJ_COLS), const, pipeline_mode=pl.Buffered(1)),
        pl.BlockSpec((tm, LANES), lambda i: (i % nt, 0)),
        pl.BlockSpec((tm, LANES), lambda i: (i % nt, 0)),
        pl.BlockSpec((tm, LANES), lambda i: (i % nt, 0)),
        pl.BlockSpec((CONV_K, CONV_WIDTH), const),
    ]
    out_specs = (
        pl.BlockSpec((1, ATTN_WIDTH, tm), lambda i: (i // nt, 0, i % nt)),
        pl.BlockSpec((1, tm, ATTN_WIDTH), lambda i: (i // nt, i % nt, 0)),
        pl.BlockSpec((1, 1, ATTN_WIDTH, tm), lambda i: (i // nt, i % nt, 0, 0)),
        pl.BlockSpec((1, IDX_HEADS * LANES, tm), lambda i: (i // nt, 0, i % nt)),
        pl.BlockSpec((1, tm, LANES), lambda i: (i // nt, i % nt, 0)),
        pl.BlockSpec((1, LANES, tm), lambda i: (i // nt, 0, i % nt)),
        pl.BlockSpec((tm, CONV_WIDTH), lambda i: (i, 0)),
        pl.BlockSpec((tm, 2 * D_MODEL), lambda i: (i, 0)),
    )
    return pl.pallas_call(
        functools.partial(_proj_kernel, tiles_per_seq=nt),
        grid=(m // tm,),
        in_specs=in_specs,
        out_specs=out_specs,
        out_shape=out_shape,
        scratch_shapes=[pltpu.VMEM((tm + SUBLANES, CONV_WIDTH), F32)],
        compiler_params=pltpu.CompilerParams(
            dimension_semantics=("arbitrary",), vmem_limit_bytes=VMEM_LIMIT),
        name="proj",
    )(x2d, gain, w, cosv, sa, sb, cw)


def _attn_kernel(qT_ref, k_ref, vT_ref, iqT_ref, ikw_ref, ikwT_ref, o_ref,
                 s_ref, qz_ref, acc_ref, m_ref, l_ref):
    j = pl.program_id(1)
    nkt = j + 1
    grp = TK // SUBLANES

    w_t = ikwT_ref[0, IW_LANE:IW_LANE + SUBLANES, :]
    tpos = j * TQ + lax.broadcasted_iota(jnp.int32, (1, TQ), 1)
    limit = (tpos // CHUNK + 1) * CHUNK
    kvec = jnp.minimum(limit, TOPK_MAX).astype(F32)

    def p1(kt, carry):
        mx, mn = carry
        k0 = pl.multiple_of(kt * TK, TK)
        ikb = ikw_ref[0, pl.ds(k0, TK), :]
        sc = None
        for h in range(IDX_HEADS):
            lg = jnp.dot(ikb, iqT_ref[0, h * LANES:(h + 1) * LANES, :], preferred_element_type=F32)
            term = jnp.maximum(lg, 0.0) * w_t[h:h + 1, :]
            sc = term if sc is None else sc + term
        sc = sc * IDX_SCALE
        kpos = k0 + lax.broadcasted_iota(jnp.int32, (TK, 1), 0)
        adm = kpos < limit
        s_ref[pl.ds(k0, TK), :] = jnp.where(adm, sc, -jnp.inf)
        mx = jnp.maximum(mx, jnp.where(adm, sc, -jnp.inf).reshape(grp, SUBLANES, TQ).max(axis=0))
        mn = jnp.minimum(mn, jnp.where(adm, sc, jnp.inf).reshape(grp, SUBLANES, TQ).min(axis=0))
        return mx, mn

    mx8, mn8 = lax.fori_loop(
        0, nkt, p1,
        (jnp.full((SUBLANES, TQ), -jnp.inf, F32), jnp.full((SUBLANES, TQ), jnp.inf, F32)))
    rowmax = mx8.max(axis=0, keepdims=True)
    rowmin = mn8.min(axis=0, keepdims=True)

    def count(bound, strict):
        b8 = jnp.broadcast_to(bound, (SUBLANES, TQ))

        def body(kt, a):
            k0 = pl.multiple_of(kt * TK, TK)
            blk = s_ref[pl.ds(k0, TK), :].reshape(grp, SUBLANES, TQ)
            hit = (blk > b8[None]) if strict else (blk >= b8[None])
            return a + jnp.where(hit, 1.0, 0.0).sum(axis=0)

        a = lax.fori_loop(0, nkt, body, jnp.zeros((SUBLANES, TQ), F32))
        return a.sum(axis=0, keepdims=True)

    def min_at_least(bound):
        b8 = jnp.broadcast_to(bound, (SUBLANES, TQ))

        def body(kt, a):
            k0 = pl.multiple_of(kt * TK, TK)
            blk = s_ref[pl.ds(k0, TK), :].reshape(grp, SUBLANES, TQ)
            return jnp.minimum(a, jnp.where(blk >= b8[None], blk, jnp.inf).min(axis=0))

        a = lax.fori_loop(0, nkt, body, jnp.full((SUBLANES, TQ), jnp.inf, F32))
        return a.min(axis=0, keepdims=True)

    def search_pass(p, st):
        lo, hi, clo, chi = st
        frac = jnp.where(p % 2 == 0, (clo - kvec + 0.5) / (clo - chi), 0.5)
        mid = jnp.minimum(jnp.maximum(lo + (hi - lo) * frac, lo), hi)
        c = count(mid, strict=False)
        ge = c >= kvec
        return (jnp.where(ge, mid, lo), jnp.where(ge, hi, mid),
                jnp.where(ge, c, clo), jnp.where(ge, chi, c))

    def round_body(carry):
        rnd, _, lo, hi, clo, chi, _, _ = carry
        lo, hi, clo, chi = lax.fori_loop(0, SEARCH_PASSES, search_pass, (lo, hi, clo, chi))
        cand = min_at_least(lo)
        cgt = count(cand, strict=True)
        pending = jnp.max(jnp.where(cgt < kvec, 0, 1).astype(jnp.int32))
        return rnd + 1, pending, lo, hi, clo, chi, cand, cgt

    def round_cond(carry):
        return jnp.logical_and(carry[1] > 0, carry[0] < MAX_ROUNDS)

    init = (jnp.int32(0), jnp.int32(1), rowmin, rowmax + (rowmax - rowmin),
            limit.astype(F32), jnp.zeros((1, TQ), F32), rowmin, jnp.zeros((1, TQ), F32))
    _, _, _, _, _, _, thr, cgt = lax.while_loop(round_cond, round_body, init)
    tie_budget = kvec - cgt

    rows = lax.broadcasted_iota(jnp.int32, (LANES, TQ), 0)
    for h in range(ATTN_HEADS):
        pair = qT_ref[0, (h // 2) * LANES:(h // 2 + 1) * LANES, :]
        qz_ref[h] = jnp.where((rows // HEAD_DIM) == (h % 2), pair, jnp.zeros_like(pair))
    acc_ref[...] = jnp.zeros_like(acc_ref)
    m_ref[...] = jnp.full_like(m_ref, M_INIT)
    l_ref[...] = jnp.zeros_like(l_ref)
    ri = lax.broadcasted_iota(jnp.int32, (TK, TK), 0)
    ci = lax.broadcasted_iota(jnp.int32, (TK, TK), 1)
    lower = jnp.where(ci < ri, 1.0, 0.0).astype(BF16)

    def p3(kt, ties_before):
        k0 = pl.multiple_of(kt * TK, TK)
        blk = s_ref[pl.ds(k0, TK), :]
        eq = jnp.where(blk == thr, 1.0, 0.0)
        rank = jnp.dot(lower, eq.astype(BF16), preferred_element_type=F32) + ties_before
        keep_tie = jnp.where(rank < tie_budget, eq, 0.0)
        sel = jnp.where(blk > thr, 1.0, keep_tie)
        bias = jnp.where(sel > 0.5, 0.0, NEG_BIAS)
        kb = k_ref[0, pl.ds(k0, TK), :]
        for h in range(ATTN_HEADS):
            kp = kb[:, (h // 2) * LANES:(h // 2 + 1) * LANES]
            s = jnp.dot(kp, qz_ref[h], preferred_element_type=F32) + bias
            m_old = m_ref[h:h + 1, :]
            m_new = jnp.maximum(m_old, s.max(axis=0, keepdims=True))
            alpha = jnp.exp(m_old - m_new)
            p = jnp.exp(s - m_new)
            l_ref[h:h + 1, :] = alpha * l_ref[h:h + 1, :] + p.sum(axis=0, keepdims=True)
            m_ref[h:h + 1, :] = m_new
            v_h = vT_ref[0, kt, h * HEAD_DIM:(h + 1) * HEAD_DIM, :]
            pv = jnp.dot(v_h, p.astype(BF16), preferred_element_type=F32)
            acc_ref[h * HEAD_DIM:(h + 1) * HEAD_DIM, :] = (
                alpha * acc_ref[h * HEAD_DIM:(h + 1) * HEAD_DIM, :] + pv)
        return ties_before + eq.sum(axis=0, keepdims=True)

    lax.fori_loop(0, nkt, p3, jnp.zeros((1, TQ), F32))

    outs = []
    for h in range(ATTN_HEADS):
        outs.append(acc_ref[h * HEAD_DIM:(h + 1) * HEAD_DIM, :] / l_ref[h:h + 1, :])
    o_ref[0] = jnp.concatenate(outs, axis=0).T.astype(BF16)


def _attn_call(qT, k, vT, iqT, ikw, ikwT, *, batch, seq):
    nq = seq // TQ
    nkt = seq // TK
    return pl.pallas_call(
        _attn_kernel,
        grid=(batch, nq),
        in_specs=[
            pl.BlockSpec((1, ATTN_WIDTH, TQ), lambda b, j: (b, 0, j)),
            pl.BlockSpec((1, seq, ATTN_WIDTH), lambda b, j: (b, 0, 0)),
            pl.BlockSpec((1, nkt, ATTN_WIDTH, TK), lambda b, j: (b, 0, 0, 0)),
            pl.BlockSpec((1, IDX_HEADS * LANES, TQ), lambda b, j: (b, 0, j)),
            pl.BlockSpec((1, seq, LANES), lambda b, j: (b, 0, 0)),
            pl.BlockSpec((1, LANES, TQ), lambda b, j: (b, 0, j)),
        ],
        out_specs=pl.BlockSpec((1, TQ, ATTN_WIDTH), lambda b, j: (b, j, 0)),
        out_shape=jax.ShapeDtypeStruct((batch, seq, ATTN_WIDTH), BF16),
        scratch_shapes=[
            pltpu.VMEM((seq, TQ), F32),
            pltpu.VMEM((ATTN_HEADS, LANES, TQ), BF16),
            pltpu.VMEM((ATTN_WIDTH, TQ), F32),
            pltpu.VMEM((ATTN_HEADS, TQ), F32),
            pltpu.VMEM((ATTN_HEADS, TQ), F32),
        ],
        compiler_params=pltpu.CompilerParams(
            dimension_semantics=("arbitrary", "arbitrary"), vmem_limit_bytes=VMEM_LIMIT),
        name="dsa_attn",
    )(qT, k, vT, iqT, ikw, ikwT)


def _post_kernel(x_ref, at_ref, uc_ref, sg_ref, wao_ref, wco_ref, wmix_ref, gm_ref, wup_ref, wdn_ref,
                 gf_ref, o_ref, *, final):
    ya = jnp.dot(at_ref[...], wao_ref[...], preferred_element_type=F32)
    yc = jnp.dot(uc_ref[...], wco_ref[...], preferred_element_type=F32)
    merged = (sg_ref[:, 0:D_MODEL].astype(F32) * ya
              + sg_ref[:, D_MODEL:2 * D_MODEL].astype(F32) * yc)
    x1 = x_ref[...] + jnp.dot(merged.astype(BF16), wmix_ref[...], preferred_element_type=F32)
    u = _rmsnorm(x1, gm_ref[...]).astype(BF16)
    x2 = x1
    for c in range(MLP_HIDDEN // MLP_CHUNK):
        hid = jnp.dot(u, wup_ref[:, c * MLP_CHUNK:(c + 1) * MLP_CHUNK], preferred_element_type=F32)
        hid = jnp.square(jnp.maximum(hid, 0.0)).astype(BF16)
        x2 = x2 + jnp.dot(hid, wdn_ref[c * MLP_CHUNK:(c + 1) * MLP_CHUNK, :], preferred_element_type=F32)
    if final:
        x2 = _rmsnorm(x2, gf_ref[...])
    o_ref[...] = x2


def _post_call(x2d, attn, uc, sg, wao, wco, wmix, gm, wup, wdn, gf, *, final):
    m = x2d.shape[0]
    tm = TM_POST
    const = lambda i: (0, 0)
    row = lambda i: (i, 0)
    resident = lambda shape: pl.BlockSpec(shape, const, pipeline_mode=pl.Buffered(1))
    return pl.pallas_call(
        functools.partial(_post_kernel, final=final),
        grid=(m // tm,),
        in_specs=[
            pl.BlockSpec((tm, D_MODEL), row),
            pl.BlockSpec((tm, ATTN_WIDTH), row),
            pl.BlockSpec((tm, CONV_WIDTH), row),
            pl.BlockSpec((tm, 2 * D_MODEL), row),
            resident((ATTN_WIDTH, D_MODEL)),
            resident((CONV_WIDTH, D_MODEL)),
            resident((D_MODEL, D_MODEL)),
            pl.BlockSpec((1, D_MODEL), const),
            resident((D_MODEL, MLP_HIDDEN)),
            resident((MLP_HIDDEN, D_MODEL)),
            pl.BlockSpec((1, D_MODEL), const),
        ],
        out_specs=pl.BlockSpec((tm, D_MODEL), row),
        out_shape=jax.ShapeDtypeStruct((m, D_MODEL), F32),
        compiler_params=pltpu.CompilerParams(
            dimension_semantics=("arbitrary",), vmem_limit_bytes=VMEM_LIMIT),
        name="mixer_tail",
    )(x2d, attn, uc, sg, wao, wco, wmix, gm, wup, wdn, gf)


def _rope_tables(seq):
    half = ROPE_DIM // 2
    inv = 1.0 / (ROPE_THETA ** (jnp.arange(0, ROPE_DIM, 2, dtype=F32) / ROPE_DIM))
    ang = jnp.arange(seq, dtype=F32)[:, None] * inv[None, :]
    cos, sin = jnp.cos(ang), jnp.sin(ang)
    ones = jnp.ones((seq, HEAD_DIM - ROPE_DIM), F32)
    zeros = jnp.zeros((seq, HEAD_DIM - ROPE_DIM), F32)
    zh = jnp.zeros((seq, half), F32)
    cos_h = jnp.concatenate([cos, cos, ones], axis=1)
    sa_h = jnp.concatenate([-sin, zh, zeros], axis=1)
    sb_h = jnp.concatenate([zh, sin, zeros], axis=1)
    two = lambda t: jnp.concatenate([t, t], axis=1)
    return two(cos_h), two(sa_h), two(sb_h)


def _arrange_w_in(w):
    sizes = (ATTN_WIDTH, ATTN_WIDTH, ATTN_WIDTH, IDX_HEADS * IDX_DIM, IDX_DIM, IDX_HEADS,
             CONV_WIDTH, CONV_WIDTH, CONV_WIDTH, 2 * D_MODEL)
    offs = np.cumsum((0,) + sizes)
    q, k, v, iq, ik, iw, cb, cc, ch, g = [w[:, offs[n]:offs[n + 1]] for n in range(len(sizes))]
    z = lambda n: jnp.zeros((w.shape[0], n), w.dtype)
    iq_cols = []
    for h in range(IDX_HEADS):
        iq_cols += [iq[:, h * IDX_DIM:(h + 1) * IDX_DIM], z(LANES - IDX_DIM)]
    ikw = [ik, z(IW_LANE - IDX_DIM), iw, z(LANES - IW_LANE - IDX_HEADS)]
    return jnp.concatenate([q, k, v] + iq_cols + ikw + [cb, cc, ch, g], axis=1).astype(BF16)


def kernel(x, norm_mix, w_in, conv_w, w_attn_out, w_conv_out, w_mix_out, norm_mlp, w_mlp_up, w_mlp_down,
           norm_final):
    batch, seq, d = x.shape
    depth = w_in.shape[0]
    assert d == D_MODEL and seq % TQ == 0 and TQ == TK == TM_PROJ and seq // 4 >= TOPK_MAX
    cosv, sa, sb = _rope_tables(seq)
    h = x.reshape(batch * seq, d)
    for l in range(depth):
        qT, k, vT, iqT, ikw, ikwT, uc, sg = _proj_call(
            h, norm_mix[l][None, :], _arrange_w_in(w_in[l]), cosv, sa, sb, conv_w[l],
            batch=batch, seq=seq)
        attn = _attn_call(qT, k, vT, iqT, ikw, ikwT, batch=batch, seq=seq)
        h = _post_call(
            h, attn.reshape(batch * seq, ATTN_WIDTH), uc, sg,
            w_attn_out[l].astype(BF16), w_conv_out[l].astype(BF16), w_mix_out[l].astype(BF16),
            norm_mlp[l][None, :], w_mlp_up[l].astype(BF16), w_mlp_down[l].astype(BF16),
            norm_final[None, :], final=(l == depth - 1))
    return h.reshape(batch, seq, d)
```

```python
import functools

import jax
import jax.numpy as jnp
import numpy as np
from jax import lax
from jax.experimental import pallas as pl
from jax.experimental.pallas import tpu as pltpu

F32 = jnp.float32
BF16 = jnp.bfloat16

D_MODEL = 1024
CHUNK = 64
EPS = 1e-6
HEAD_DIM = 64
ATTN_WIDTH = 512
ATTN_HEADS = 8
ROPE_DIM = 16
ROPE_THETA = 500000.0
IDX_HEADS = 4
IDX_DIM = 64
IDX_SCALE = (IDX_DIM ** -0.5) * (IDX_HEADS ** -0.5)
TOPK_MAX = 256
CONV_WIDTH = 512
CONV_K = 3
MLP_HIDDEN = 4 * D_MODEL

LANES = 128
SUBLANES = 8

OFF_Q = 0
OFF_K = OFF_Q + ATTN_WIDTH
OFF_V = OFF_K + ATTN_WIDTH
OFF_IQ = OFF_V + ATTN_WIDTH
OFF_IKW = OFF_IQ + IDX_HEADS * LANES
IW_LANE = 96
OFF_CB = OFF_IKW + LANES
OFF_CC = OFF_CB + CONV_WIDTH
OFF_CH = OFF_CC + CONV_WIDTH
OFF_G = OFF_CH + CONV_WIDTH
PROJ_COLS = OFF_G + 2 * D_MODEL

TM_PROJ = 512
TQ = 256
TK = 256
TM_POST = 256
MLP_CHUNK = 1024
PASSES_PER_CHECK = 2
MAX_CHECKS = 256
NEG_BIAS = -2e30
M_INIT = -1e30
VMEM_LIMIT = 48 * 1024 * 1024


def _rmsnorm(x, g):
    ms = jnp.mean(x * x, axis=-1, keepdims=True)
    return x * lax.rsqrt(ms + EPS) * g


def _proj_kernel(x_ref, g_ref, wt_ref, cos_ref, sa_ref, sb_ref, cw_ref,
                 qT_ref, k_ref, vT_ref, iqT_ref, ikw_ref, ikwT_ref, uc_ref, sg_ref,
                 zbuf, *, tiles_per_seq):
    i = pl.program_id(0)
    tm = x_ref.shape[0]
    u = _rmsnorm(x_ref[...], g_ref[...]).astype(BF16)
    cosv = cos_ref[...]
    sa = sa_ref[...]
    sb = sb_ref[...]

    def proj(c0, n):
        return lax.dot_general(u, wt_ref[c0:c0 + n, :], (((1,), (1,)), ((), ())),
                               preferred_element_type=F32)

    def rope(a):
        outs = []
        for gidx in range(a.shape[1] // LANES):
            ag = a[:, gidx * LANES:(gidx + 1) * LANES]
            outs.append(ag * cosv
                        + pltpu.roll(ag, LANES - ROPE_DIM // 2, 1) * sa
                        + pltpu.roll(ag, ROPE_DIM // 2, 1) * sb)
        return outs[0] if len(outs) == 1 else jnp.concatenate(outs, axis=1)

    q = rope(proj(OFF_Q, ATTN_WIDTH)) * (HEAD_DIM ** -0.5)
    qT_ref[0] = q.T.astype(BF16)
    k_ref[0] = rope(proj(OFF_K, ATTN_WIDTH)).astype(BF16)
    v_t = proj(OFF_V, ATTN_WIDTH).T.astype(BF16)
    for t in range(tm // TK):
        vT_ref[0, t] = v_t[:, t * TK:(t + 1) * TK]
    iqT_ref[0] = rope(proj(OFF_IQ, IDX_HEADS * LANES)).T.astype(BF16)
    ikw = rope(proj(OFF_IKW, LANES))
    ikw_ref[0] = ikw.astype(BF16)
    ikwT_ref[0] = ikw.T

    z = proj(OFF_CC, CONV_WIDTH) * proj(OFF_CH, CONV_WIDTH)

    @pl.when(i % tiles_per_seq == 0)
    def _():
        zbuf[0:SUBLANES, :] = jnp.zeros((SUBLANES, CONV_WIDTH), F32)

    @pl.when(i % tiles_per_seq != 0)
    def _():
        zbuf[0:SUBLANES, :] = zbuf[tm:tm + SUBLANES, :]

    zbuf[SUBLANES:SUBLANES + tm, :] = z
    z1 = zbuf[SUBLANES - 1:SUBLANES - 1 + tm, :]
    z2 = zbuf[SUBLANES - 2:SUBLANES - 2 + tm, :]
    conv = z2 * cw_ref[0:1, :] + z1 * cw_ref[1:2, :] + z * cw_ref[2:3, :]
    uc_ref[...] = (proj(OFF_CB, CONV_WIDTH) * conv).astype(BF16)

    for c in range(4):
        gc = proj(OFF_G + c * 512, 512)
        sg_ref[:, c * 512:(c + 1) * 512] = jax.nn.sigmoid(gc).astype(BF16)


def _proj_call(x2d, gain, wt, cosv, sa, sb, cw, *, batch, seq):
    m = x2d.shape[0]
    tm = TM_PROJ
    nt = seq // tm
    kt_per_step = tm // TK
    const = lambda i: (0, 0)
    out_shape = (
        jax.ShapeDtypeStruct((batch, ATTN_WIDTH, seq), BF16),
        jax.ShapeDtypeStruct((batch, seq, ATTN_WIDTH), BF16),
        jax.ShapeDtypeStruct((batch, seq // TK, ATTN_WIDTH, TK), BF16),
        jax.ShapeDtypeStruct((batch, IDX_HEADS * LANES, seq), BF16),
        jax.ShapeDtypeStruct((batch, seq, LANES), BF16),
        jax.ShapeDtypeStruct((batch, LANES, seq), F32),
        jax.ShapeDtypeStruct((m, CONV_WIDTH), BF16),
        jax.ShapeDtypeStruct((m, 2 * D_MODEL), BF16),
    )
    in_specs = [
        pl.BlockSpec((tm, D_MODEL), lambda i: (i, 0)),
        pl.BlockSpec((1, D_MODEL), const),
        pl.BlockSpec((PROJ_COLS, D_MODEL), const, pipeline_mode=pl.Buffered(1)),
        pl.BlockSpec((tm, LANES), lambda i: (i % nt, 0)),
        pl.BlockSpec((tm, LANES), lambda i: (i % nt, 0)),
        pl.BlockSpec((tm, LANES), lambda i: (i % nt, 0)),
        pl.BlockSpec((CONV_K, CONV_WIDTH), const),
    ]
    out_specs = (
        pl.BlockSpec((1, ATTN_WIDTH, tm), lambda i: (i // nt, 0, i % nt)),
        pl.BlockSpec((1, tm, ATTN_WIDTH), lambda i: (i // nt, i % nt, 0)),
        pl.BlockSpec((1, kt_per_step, ATTN_WIDTH, TK), lambda i: (i // nt, i % nt, 0, 0)),
        pl.BlockSpec((1, IDX_HEADS * LANES, tm), lambda i: (i // nt, 0, i % nt)),
        pl.BlockSpec((1, tm, LANES), lambda i: (i // nt, i % nt, 0)),
        pl.BlockSpec((1, LANES, tm), lambda i: (i // nt, 0, i % nt)),
        pl.BlockSpec((tm, CONV_WIDTH), lambda i: (i, 0)),
        pl.BlockSpec((tm, 2 * D_MODEL), lambda i: (i, 0)),
    )
    return pl.pallas_call(
        functools.partial(_proj_kernel, tiles_per_seq=nt),
        grid=(m // tm,),
        in_specs=in_specs,
        out_specs=out_specs,
        out_shape=out_shape,
        scratch_shapes=[pltpu.VMEM((tm + SUBLANES, CONV_WIDTH), F32)],
        compiler_params=pltpu.CompilerParams(
            dimension_semantics=("arbitrary",), vmem_limit_bytes=VMEM_LIMIT),
        name="proj",
    )(x2d, gain, wt, cosv, sa, sb, cw)


def _attn_kernel(qT_ref, k_ref, vT_ref, iqT_ref, ikw_ref, ikwT_ref, o_ref,
                 s_ref, sb_ref, qz_ref, acc_ref, m_ref, l_ref):
    j = pl.program_id(1)
    nkt = j + 1
    grp = TK // SUBLANES

    def tile_start(kt):
        return pl.multiple_of(kt * TK, TK)

    def fold(x):
        return x.reshape(grp, SUBLANES, TQ)

    w_t = ikwT_ref[0, IW_LANE:IW_LANE + SUBLANES, :] * IDX_SCALE
    tpos = j * TQ + lax.broadcasted_iota(jnp.int32, (1, TQ), 1)
    limit = (tpos // CHUNK + 1) * CHUNK
    nadm = limit.astype(F32)
    kvec = jnp.minimum(limit, TOPK_MAX).astype(F32)

    def p1(kt, amax):
        k0 = tile_start(kt)
        ikb = ikw_ref[0, pl.ds(k0, TK), :]
        lgs = [jnp.dot(ikb, iqT_ref[0, h * LANES:(h + 1) * LANES, :], preferred_element_type=F32)
               for h in range(IDX_HEADS)]
        sc = jnp.maximum(lgs[0], 0.0) * w_t[0:1, :]
        for h in range(1, IDX_HEADS):
            sc = sc + jnp.maximum(lgs[h], 0.0) * w_t[h:h + 1, :]
        kpos = k0 + lax.broadcasted_iota(jnp.int32, (TK, 1), 0)
        s_ref[pl.ds(k0, TK), :] = jnp.where(kpos < limit, sc, -jnp.inf)
        return jnp.maximum(amax, fold(jnp.abs(sc)).max(axis=0))

    amax8 = lax.fori_loop(0, nkt, p1, jnp.zeros((SUBLANES, TQ), F32))
    maxabs = amax8.max(axis=0, keepdims=True)

    def count_gt(bound):
        b8 = jnp.broadcast_to(bound, (SUBLANES, TQ))

        def body(kt, a):
            blk = fold(s_ref[pl.ds(tile_start(kt), TK), :])
            return a + jnp.where(blk > b8[None], 1.0, 0.0).sum(axis=0)

        a = lax.fori_loop(0, nkt, body, jnp.zeros((SUBLANES, TQ), F32))
        return a.sum(axis=0, keepdims=True)

    def zero_counts():
        def body(kt, carry):
            g, e = carry
            blk = fold(s_ref[pl.ds(tile_start(kt), TK), :])
            return (g + jnp.where(blk > 0.0, 1.0, 0.0).sum(axis=0),
                    e + jnp.where(blk >= 0.0, 1.0, 0.0).sum(axis=0))

        z8 = jnp.zeros((SUBLANES, TQ), F32)
        g, e = lax.fori_loop(0, nkt, body, (z8, z8))
        return g.sum(axis=0, keepdims=True), e.sum(axis=0, keepdims=True)

    cg0, ce0 = zero_counts()
    one = jnp.ones((1, TQ), F32)
    zero = jnp.zeros((1, TQ), F32)
    tgt = kvec - 0.5
    allsel = nadm <= kvec
    tie0 = jnp.logical_and(cg0 < kvec, ce0 >= kvec)
    pos = cg0 > kvec
    done0 = jnp.logical_or(allsel, jnp.logical_or(tie0, cg0 == kvec))
    theta0 = jnp.where(allsel, -jnp.inf, 0.0)
    tie_thr0 = jnp.where(jnp.logical_and(tie0, jnp.logical_not(allsel)), 0.0, jnp.inf)
    init = (
        jnp.int32(0), jnp.int32(1),
        jnp.where(pos, 0.0, -maxabs), jnp.where(pos, maxabs, 0.0),
        jnp.where(pos, cg0, nadm) - tgt, jnp.where(pos, 0.0, ce0) - tgt,
        jnp.where(pos, 0.0, cg0), zero,
        jnp.where(done0, one, zero), theta0, tie_thr0, cg0,
    )

    def one_pass(st):
        lo, hi, flo, fhi, chi, side, done, theta, tie_thr, tie_cgt = st
        interp = lo + (hi - lo) * (flo / (flo - fhi))
        bis = 0.5 * lo + 0.5 * hi
        mid = jnp.where(jnp.logical_and(interp > lo, interp < hi), interp, bis)
        inside = jnp.logical_and(mid > lo, mid < hi)
        c = count_gt(mid)
        active = done < 0.5
        live = jnp.logical_and(active, inside)
        hit = jnp.logical_and(live, c == kvec)
        stuck = jnp.logical_and(active, jnp.logical_not(inside))
        up = jnp.logical_and(live, c > kvec)
        dn = jnp.logical_and(live, c < kvec)
        theta = jnp.where(hit, mid, jnp.where(stuck, hi, theta))
        tie_thr = jnp.where(stuck, hi, tie_thr)
        tie_cgt = jnp.where(stuck, chi, tie_cgt)
        done = jnp.where(jnp.logical_or(hit, stuck), 1.0, done)
        fhi_n = jnp.where(dn, c - tgt, jnp.where(jnp.logical_and(up, side > 0.5), fhi * 0.5, fhi))
        flo_n = jnp.where(up, c - tgt, jnp.where(jnp.logical_and(dn, side < -0.5), flo * 0.5, flo))
        return (jnp.where(up, mid, lo), jnp.where(dn, mid, hi), flo_n, fhi_n,
                jnp.where(dn, c, chi), jnp.where(up, 1.0, jnp.where(dn, -1.0, side)),
                done, theta, tie_thr, tie_cgt)

    def check_body(carry):
        st = carry[2:]
        for _ in range(PASSES_PER_CHECK):
            st = one_pass(st)
        pending = jnp.max(jnp.where(st[6] < 0.5, 1, 0).astype(jnp.int32))
        return (carry[0] + 1, pending) + st

    def check_cond(carry):
        return jnp.logical_and(carry[1] > 0, carry[0] < MAX_CHECKS)

    init = init[:1] + (jnp.max(jnp.where(done0, 0, 1).astype(jnp.int32)),) + init[2:]
    final = lax.while_loop(check_cond, check_body, init)
    theta, tie_thr, tie_cgt = final[9], final[10], final[11]
    tie_budget = kvec - tie_cgt
    any_tie = jnp.max(jnp.where(tie_thr < jnp.inf, 1, 0).astype(jnp.int32))

    @pl.when(any_tie == 0)
    def _():
        def body(kt, c):
            k0 = tile_start(kt)
            s_ref[pl.ds(k0, TK), :] = jnp.where(s_ref[pl.ds(k0, TK), :] > theta, 0.0, NEG_BIAS)
            return c

        lax.fori_loop(0, nkt, body, 0)

    @pl.when(any_tie > 0)
    def _():
        ri = lax.broadcasted_iota(jnp.int32, (TK, TK), 0)
        ci = lax.broadcasted_iota(jnp.int32, (TK, TK), 1)
        lower = jnp.where(ci < ri, 1.0, 0.0).astype(BF16)

        def body(kt, ties_before):
            k0 = tile_start(kt)
            blk = s_ref[pl.ds(k0, TK), :]
            eq = jnp.where(blk == tie_thr, 1.0, 0.0)
            rank = jnp.dot(lower, eq.astype(BF16), preferred_element_type=F32) + ties_before
            keep = jnp.where(blk > theta, 1.0, jnp.where(rank < tie_budget, eq, 0.0))
            s_ref[pl.ds(k0, TK), :] = jnp.where(keep > 0.5, 0.0, NEG_BIAS)
            return ties_before + eq.sum(axis=0, keepdims=True)

        lax.fori_loop(0, nkt, body, zero)

    rows = lax.broadcasted_iota(jnp.int32, (LANES, TQ), 0)
    for h in range(ATTN_HEADS):
        pair = qT_ref[0, (h // 2) * LANES:(h // 2 + 1) * LANES, :]
        qz_ref[h] = jnp.where((rows // HEAD_DIM) == (h % 2), pair, jnp.zeros_like(pair))
    acc_ref[...] = jnp.zeros_like(acc_ref)
    m_ref[...] = jnp.full_like(m_ref, M_INIT)
    l_ref[...] = jnp.zeros_like(l_ref)

    def p3(kt, c):
        k0 = tile_start(kt)
        bias = s_ref[pl.ds(k0, TK), :]
        kb = k_ref[0, pl.ds(k0, TK), :]
        colmax = []
        for h in range(ATTN_HEADS):
            kp = kb[:, (h // 2) * LANES:(h // 2 + 1) * LANES]
            s = jnp.dot(kp, qz_ref[h], preferred_element_type=F32) + bias
            sb_ref[h] = s
            colmax.append(fold(s).max(axis=0).max(axis=0, keepdims=True))
        m_old = m_ref[...]
        m_new = jnp.maximum(m_old, jnp.concatenate(colmax, axis=0))
        alpha = jnp.exp(m_old - m_new)
        m_ref[...] = m_new
        colsum = []
        for h in range(ATTN_HEADS):
            p = jnp.exp(sb_ref[h] - m_new[h:h + 1, :])
            colsum.append(fold(p).sum(axis=0).sum(axis=0, keepdims=True))
            v_h = vT_ref[0, kt, h * HEAD_DIM:(h + 1) * HEAD_DIM, :]
            pv = jnp.dot(v_h, p.astype(BF16), preferred_element_type=F32)
            acc_ref[h * HEAD_DIM:(h + 1) * HEAD_DIM, :] = (
                alpha[h:h + 1, :] * acc_ref[h * HEAD_DIM:(h + 1) * HEAD_DIM, :] + pv)
        l_ref[...] = alpha * l_ref[...] + jnp.concatenate(colsum, axis=0)
        return c

    lax.fori_loop(0, nkt, p3, 0)

    l_all = l_ref[...]
    outs = []
    for h in range(ATTN_HEADS):
        outs.append(acc_ref[h * HEAD_DIM:(h + 1) * HEAD_DIM, :] / l_all[h:h + 1, :])
    o_ref[0] = jnp.concatenate(outs, axis=0).T.astype(BF16)


def _attn_call(qT, k, vT, iqT, ikw, ikwT, *, batch, seq):
    nq = seq // TQ
    nkt = seq // TK
    return pl.pallas_call(
        _attn_kernel,
        grid=(batch, nq),
        in_specs=[
            pl.BlockSpec((1, ATTN_WIDTH, TQ), lambda b, j: (b, 0, j)),
            pl.BlockSpec((1, seq, ATTN_WIDTH), lambda b, j: (b, 0, 0)),
            pl.BlockSpec((1, nkt, ATTN_WIDTH, TK), lambda b, j: (b, 0, 0, 0)),
            pl.BlockSpec((1, IDX_HEADS * LANES, TQ), lambda b, j: (b, 0, j)),
            pl.BlockSpec((1, seq, LANES), lambda b, j: (b, 0, 0)),
            pl.BlockSpec((1, LANES, TQ), lambda b, j: (b, 0, j)),
        ],
        out_specs=pl.BlockSpec((1, TQ, ATTN_WIDTH), lambda b, j: (b, j, 0)),
        out_shape=jax.ShapeDtypeStruct((batch, seq, ATTN_WIDTH), BF16),
        scratch_shapes=[
            pltpu.VMEM((seq, TQ), F32),
            pltpu.VMEM((ATTN_HEADS, TK, TQ), F32),
            pltpu.VMEM((ATTN_HEADS, LANES, TQ), BF16),
            pltpu.VMEM((ATTN_WIDTH, TQ), F32),
            pltpu.VMEM((ATTN_HEADS, TQ), F32),
            pltpu.VMEM((ATTN_HEADS, TQ), F32),
        ],
        compiler_params=pltpu.CompilerParams(
            dimension_semantics=("arbitrary", "arbitrary"), vmem_limit_bytes=VMEM_LIMIT),
        name="dsa_attn",
    )(qT, k, vT, iqT, ikw, ikwT)


def _post_kernel(x_ref, at_ref, uc_ref, sg_ref, wao_ref, wco_ref, wmix_ref, gm_ref, wup_ref, wdn_ref,
                 gf_ref, o_ref, *, final):
    ya = jnp.dot(at_ref[...], wao_ref[...], preferred_element_type=F32)
    yc = jnp.dot(uc_ref[...], wco_ref[...], preferred_element_type=F32)
    merged = (sg_ref[:, 0:D_MODEL].astype(F32) * ya
              + sg_ref[:, D_MODEL:2 * D_MODEL].astype(F32) * yc)
    x1 = x_ref[...] + jnp.dot(merged.astype(BF16), wmix_ref[...], preferred_element_type=F32)
    u = _rmsnorm(x1, gm_ref[...]).astype(BF16)
    x2 = x1
    for c in range(MLP_HIDDEN // MLP_CHUNK):
        hid = jnp.dot(u, wup_ref[:, c * MLP_CHUNK:(c + 1) * MLP_CHUNK], preferred_element_type=F32)
        hid = jnp.square(jnp.maximum(hid, 0.0)).astype(BF16)
        x2 = x2 + jnp.dot(hid, wdn_ref[c * MLP_CHUNK:(c + 1) * MLP_CHUNK, :], preferred_element_type=F32)
    if final:
        x2 = _rmsnorm(x2, gf_ref[...])
    o_ref[...] = x2


def _post_call(x2d, attn, uc, sg, wao, wco, wmix, gm, wup, wdn, gf, *, final):
    m = x2d.shape[0]
    tm = TM_POST
    const = lambda i: (0, 0)
    row = lambda i: (i, 0)
    resident = lambda shape: pl.BlockSpec(shape, const, pipeline_mode=pl.Buffered(1))
    return pl.pallas_call(
        functools.partial(_post_kernel, final=final),
        grid=(m // tm,),
        in_specs=[
            pl.BlockSpec((tm, D_MODEL), row),
            pl.BlockSpec((tm, ATTN_WIDTH), row),
            pl.BlockSpec((tm, CONV_WIDTH), row),
            pl.BlockSpec((tm, 2 * D_MODEL), row),
            resident((ATTN_WIDTH, D_MODEL)),
            resident((CONV_WIDTH, D_MODEL)),
            resident((D_MODEL, D_MODEL)),
            pl.BlockSpec((1, D_MODEL), const),
            resident((D_MODEL, MLP_HIDDEN)),
            resident((MLP_HIDDEN, D_MODEL)),
            pl.BlockSpec((1, D_MODEL), const),
        ],
        out_specs=pl.BlockSpec((tm, D_MODEL), row),
        out_shape=jax.ShapeDtypeStruct((m, D_MODEL), F32),
        compiler_params=pltpu.CompilerParams(
            dimension_semantics=("arbitrary",), vmem_limit_bytes=VMEM_LIMIT),
        name="mixer_tail",
    )(x2d, attn, uc, sg, wao, wco, wmix, gm, wup, wdn, gf)


def _rope_tables(seq):
    half = ROPE_DIM // 2
    inv = 1.0 / (ROPE_THETA ** (jnp.arange(0, ROPE_DIM, 2, dtype=F32) / ROPE_DIM))
    ang = jnp.arange(seq, dtype=F32)[:, None] * inv[None, :]
    cos, sin = jnp.cos(ang), jnp.sin(ang)
    ones = jnp.ones((seq, HEAD_DIM - ROPE_DIM), F32)
    zeros = jnp.zeros((seq, HEAD_DIM - ROPE_DIM), F32)
    zh = jnp.zeros((seq, half), F32)
    cos_h = jnp.concatenate([cos, cos, ones], axis=1)
    sa_h = jnp.concatenate([-sin, zh, zeros], axis=1)
    sb_h = jnp.concatenate([zh, sin, zeros], axis=1)
    two = lambda t: jnp.concatenate([t, t], axis=1)
    return two(cos_h), two(sa_h), two(sb_h)


def _arrange_w_in_t(w):
    wt = w.T
    sizes = (ATTN_WIDTH, ATTN_WIDTH, ATTN_WIDTH, IDX_HEADS * IDX_DIM, IDX_DIM, IDX_HEADS,
             CONV_WIDTH, CONV_WIDTH, CONV_WIDTH, 2 * D_MODEL)
    offs = np.cumsum((0,) + sizes)
    q, k, v, iq, ik, iw, cb, cc, ch, g = [wt[offs[n]:offs[n + 1]] for n in range(len(sizes))]
    z = lambda n: jnp.zeros((n, wt.shape[1]), wt.dtype)
    iq_rows = []
    for h in range(IDX_HEADS):
        iq_rows += [iq[h * IDX_DIM:(h + 1) * IDX_DIM], z(LANES - IDX_DIM)]
    ikw = [ik, z(IW_LANE - IDX_DIM), iw, z(LANES - IW_LANE - IDX_HEADS)]
    return jnp.concatenate([q, k, v] + iq_rows + ikw + [cb, cc, ch, g], axis=0).astype(BF16)


def kernel(x, norm_mix, w_in, conv_w, w_attn_out, w_conv_out, w_mix_out, norm_mlp, w_mlp_up, w_mlp_down,
           norm_final):
    batch, seq, d = x.shape
    depth = w_in.shape[0]
    assert d == D_MODEL and seq % TM_PROJ == 0 and TM_PROJ % TK == 0 and TQ == TK
    assert seq // 4 >= TOPK_MAX
    cosv, sa, sb = _rope_tables(seq)
    h = x.reshape(batch * seq, d)
    for l in range(depth):
        qT, k, vT, iqT, ikw, ikwT, uc, sg = _proj_call(
            h, norm_mix[l][None, :], _arrange_w_in_t(w_in[l]), cosv, sa, sb, conv_w[l],
            batch=batch, seq=seq)
        attn = _attn_call(qT, k, vT, iqT, ikw, ikwT, batch=batch, seq=seq)
        h = _post_call(
            h, attn.reshape(batch * seq, ATTN_WIDTH), uc, sg,
            w_attn_out[l].astype(BF16), w_conv_out[l].astype(BF16), w_mix_out[l].astype(BF16),
            norm_mlp[l][None, :], w_mlp_up[l].astype(BF16), w_mlp_down[l].astype(BF16),
            norm_final[None, :], final=(l == depth - 1))
    return h.reshape(batch, seq, d)
```

```python
import functools

import jax
import jax.numpy as jnp
import numpy as np
from jax import lax
from jax.experimental import pallas as pl
from jax.experimental.pallas import tpu as pltpu

F32 = jnp.float32
BF16 = jnp.bfloat16

D_MODEL = 1024
CHUNK = 64
EPS = 1e-6
HEAD_DIM = 64
ATTN_WIDTH = 512
ATTN_HEADS = 8
ROPE_DIM = 16
ROPE_THETA = 500000.0
IDX_HEADS = 4
IDX_DIM = 64
IDX_SCALE = (IDX_DIM ** -0.5) * (IDX_HEADS ** -0.5)
TOPK_MAX = 256
CONV_WIDTH = 512
CONV_K = 3
MLP_HIDDEN = 4 * D_MODEL

LANES = 128
SUBLANES = 8
BF16_ROWS = 16
QK_SCALE = (HEAD_DIM ** -0.5) * float(np.log2(np.e))

OFF_Q = 0
OFF_K = OFF_Q + ATTN_WIDTH
OFF_V = OFF_K + ATTN_WIDTH
OFF_IQ = OFF_V + ATTN_WIDTH
OFF_IKW = OFF_IQ + IDX_HEADS * LANES
IW_LANE = 96
OFF_CB = OFF_IKW + LANES
OFF_CC = OFF_CB + CONV_WIDTH
OFF_CH = OFF_CC + CONV_WIDTH
OFF_G = OFF_CH + CONV_WIDTH
PROJ_COLS = OFF_G + 2 * D_MODEL

TM_PROJ = 512
TQ = 256
TK = 256
TM_POST = 256
MLP_CHUNK = 1024
PASSES_PER_CHECK = 2
MAX_CHECKS = 256
NEG_BIAS = -2e30
M_INIT = -1e30
VMEM_LIMIT = 48 * 1024 * 1024


def _rmsnorm(x, g):
    ms = jnp.mean(x * x, axis=-1, keepdims=True)
    return x * lax.rsqrt(ms + EPS) * g


def _proj_kernel(x_ref, g_ref, wt_ref, cos_ref, sa_ref, sb_ref, cw_ref,
                 qT_ref, k_ref, vT_ref, iqT_ref, ikw_ref, ikwT_ref, uc_ref, sg_ref,
                 zbuf, *, tiles_per_seq):
    i = pl.program_id(0)
    tm = x_ref.shape[0]
    u = _rmsnorm(x_ref[...], g_ref[...]).astype(BF16)
    cosv = cos_ref[...]
    sa = sa_ref[...]
    sb = sb_ref[...]

    def proj(c0, n):
        return lax.dot_general(u, wt_ref[c0:c0 + n, :], (((1,), (1,)), ((), ())),
                               preferred_element_type=F32)

    def rope(a):
        outs = []
        for gidx in range(a.shape[1] // LANES):
            ag = a[:, gidx * LANES:(gidx + 1) * LANES]
            outs.append(ag * cosv
                        + pltpu.roll(ag, LANES - ROPE_DIM // 2, 1) * sa
                        + pltpu.roll(ag, ROPE_DIM // 2, 1) * sb)
        return outs[0] if len(outs) == 1 else jnp.concatenate(outs, axis=1)

    q = rope(proj(OFF_Q, ATTN_WIDTH)) * QK_SCALE
    qT_ref[0] = q.T.astype(BF16)
    k_ref[0] = rope(proj(OFF_K, ATTN_WIDTH)).astype(BF16)
    v_t = proj(OFF_V, ATTN_WIDTH).T.astype(BF16)
    for t in range(tm // TK):
        vT_ref[0, t] = v_t[:, t * TK:(t + 1) * TK]
    iqT_ref[0] = rope(proj(OFF_IQ, IDX_HEADS * LANES)).T.astype(BF16)
    ikw = rope(proj(OFF_IKW, LANES))
    ikw_ref[0] = ikw.astype(BF16)
    ikwT_ref[0] = ikw.T

    z = proj(OFF_CC, CONV_WIDTH) * proj(OFF_CH, CONV_WIDTH)

    @pl.when(i % tiles_per_seq == 0)
    def _():
        zbuf[0:SUBLANES, :] = jnp.zeros((SUBLANES, CONV_WIDTH), F32)

    @pl.when(i % tiles_per_seq != 0)
    def _():
        zbuf[0:SUBLANES, :] = zbuf[tm:tm + SUBLANES, :]

    zbuf[SUBLANES:SUBLANES + tm, :] = z
    z1 = zbuf[SUBLANES - 1:SUBLANES - 1 + tm, :]
    z2 = zbuf[SUBLANES - 2:SUBLANES - 2 + tm, :]
    conv = z2 * cw_ref[0:1, :] + z1 * cw_ref[1:2, :] + z * cw_ref[2:3, :]
    uc_ref[...] = (proj(OFF_CB, CONV_WIDTH) * conv).astype(BF16)

    for c in range(4):
        gc = proj(OFF_G + c * 512, 512)
        sg_ref[:, c * 512:(c + 1) * 512] = jax.nn.sigmoid(gc).astype(BF16)


def _proj_call(x2d, gain, wt, cosv, sa, sb, cw, *, batch, seq):
    m = x2d.shape[0]
    tm = TM_PROJ
    nt = seq // tm
    kt_per_step = tm // TK
    const = lambda i: (0, 0)
    out_shape = (
        jax.ShapeDtypeStruct((batch, ATTN_WIDTH, seq), BF16),
        jax.ShapeDtypeStruct((batch, seq, ATTN_WIDTH), BF16),
        jax.ShapeDtypeStruct((batch, seq // TK, ATTN_WIDTH, TK), BF16),
        jax.ShapeDtypeStruct((batch, IDX_HEADS * LANES, seq), BF16),
        jax.ShapeDtypeStruct((batch, seq, LANES), BF16),
        jax.ShapeDtypeStruct((batch, LANES, seq), F32),
        jax.ShapeDtypeStruct((m, CONV_WIDTH), BF16),
        jax.ShapeDtypeStruct((m, 2 * D_MODEL), BF16),
    )
    in_specs = [
        pl.BlockSpec((tm, D_MODEL), lambda i: (i, 0)),
        pl.BlockSpec((1, D_MODEL), const),
        pl.BlockSpec((PROJ_COLS, D_MODEL), const, pipeline_mode=pl.Buffered(1)),
        pl.BlockSpec((tm, LANES), lambda i: (i % nt, 0)),
        pl.BlockSpec((tm, LANES), lambda i: (i % nt, 0)),
        pl.BlockSpec((tm, LANES), lambda i: (i % nt, 0)),
        pl.BlockSpec((CONV_K, CONV_WIDTH), const),
    ]
    out_specs = (
        pl.BlockSpec((1, ATTN_WIDTH, tm), lambda i: (i // nt, 0, i % nt)),
        pl.BlockSpec((1, tm, ATTN_WIDTH), lambda i: (i // nt, i % nt, 0)),
        pl.BlockSpec((1, kt_per_step, ATTN_WIDTH, TK), lambda i: (i // nt, i % nt, 0, 0)),
        pl.BlockSpec((1, IDX_HEADS * LANES, tm), lambda i: (i // nt, 0, i % nt)),
        pl.BlockSpec((1, tm, LANES), lambda i: (i // nt, i % nt, 0)),
        pl.BlockSpec((1, LANES, tm), lambda i: (i // nt, 0, i % nt)),
        pl.BlockSpec((tm, CONV_WIDTH), lambda i: (i, 0)),
        pl.BlockSpec((tm, 2 * D_MODEL), lambda i: (i, 0)),
    )
    return pl.pallas_call(
        functools.partial(_proj_kernel, tiles_per_seq=nt),
        grid=(m // tm,),
        in_specs=in_specs,
        out_specs=out_specs,
        out_shape=out_shape,
        scratch_shapes=[pltpu.VMEM((tm + SUBLANES, CONV_WIDTH), F32)],
        compiler_params=pltpu.CompilerParams(
            dimension_semantics=("arbitrary",), vmem_limit_bytes=VMEM_LIMIT),
        name="proj",
    )(x2d, gain, wt, cosv, sa, sb, cw)


def _attn_kernel(qT_ref, k_ref, vT_ref, iqT_ref, ikw_ref, ikwT_ref, o_ref,
                 s_ref, sb_ref, qz_ref, acc_ref, m_ref, a_ref, l_ref):
    j = pl.program_id(1)
    nkt = j + 1
    grp = TK // SUBLANES

    def tile_start(kt):
        return pl.multiple_of(kt * TK, TK)

    def vreduce(x, op):
        parts = [x[g * SUBLANES:(g + 1) * SUBLANES, :] for g in range(grp)]
        while len(parts) > 1:
            parts = [op(parts[i], parts[i + 1]) for i in range(0, len(parts), 2)]
        return parts[0]

    w_t = ikwT_ref[0, IW_LANE:IW_LANE + SUBLANES, :] * IDX_SCALE
    tpos = j * TQ + lax.broadcasted_iota(jnp.int32, (1, TQ), 1)
    limit = (tpos // CHUNK + 1) * CHUNK
    nadm = limit.astype(F32)
    kvec = jnp.minimum(limit, TOPK_MAX).astype(F32)

    def p1(kt, amax):
        k0 = tile_start(kt)
        ikb = ikw_ref[0, pl.ds(k0, TK), :]
        lgs = [jnp.dot(ikb, iqT_ref[0, h * LANES:(h + 1) * LANES, :], preferred_element_type=F32)
               for h in range(IDX_HEADS)]
        sc = jnp.maximum(lgs[0], 0.0) * w_t[0:1, :]
        for h in range(1, IDX_HEADS):
            sc = sc + jnp.maximum(lgs[h], 0.0) * w_t[h:h + 1, :]
        kpos = k0 + lax.broadcasted_iota(jnp.int32, (TK, 1), 0)
        s_ref[pl.ds(k0, TK), :] = jnp.where(kpos < limit, sc, -jnp.inf)
        return jnp.maximum(amax, vreduce(jnp.abs(sc), jnp.maximum))

    amax8 = lax.fori_loop(0, nkt, p1, jnp.zeros((SUBLANES, TQ), F32))
    maxabs = amax8.max(axis=0, keepdims=True)

    def count_gt(bound):
        def body(kt, a):
            blk = s_ref[pl.ds(tile_start(kt), TK), :]
            return a + vreduce(jnp.where(blk > bound, 1.0, 0.0), jnp.add)

        a = lax.fori_loop(0, nkt, body, jnp.zeros((SUBLANES, TQ), F32))
        return a.sum(axis=0, keepdims=True)

    def zero_counts():
        def body(kt, carry):
            g, e = carry
            blk = s_ref[pl.ds(tile_start(kt), TK), :]
            return (g + vreduce(jnp.where(blk > 0.0, 1.0, 0.0), jnp.add),
                    e + vreduce(jnp.where(blk >= 0.0, 1.0, 0.0), jnp.add))

        z8 = jnp.zeros((SUBLANES, TQ), F32)
        g, e = lax.fori_loop(0, nkt, body, (z8, z8))
        return g.sum(axis=0, keepdims=True), e.sum(axis=0, keepdims=True)

    cg0, ce0 = zero_counts()
    one = jnp.ones((1, TQ), F32)
    zero = jnp.zeros((1, TQ), F32)
    tgt = kvec - 0.5
    allsel = nadm <= kvec
    tie0 = jnp.logical_and(cg0 < kvec, ce0 >= kvec)
    pos = cg0 > kvec
    done0 = jnp.logical_or(allsel, jnp.logical_or(tie0, cg0 == kvec))
    theta0 = jnp.where(allsel, -jnp.inf, 0.0)
    tie_thr0 = jnp.where(jnp.logical_and(tie0, jnp.logical_not(allsel)), 0.0, jnp.inf)
    init = (
        jnp.int32(0), jnp.int32(1),
        jnp.where(pos, 0.0, -2.0 * maxabs), jnp.where(pos, maxabs, 0.0),
        jnp.where(pos, cg0, nadm) - tgt, jnp.where(pos, 0.0, ce0) - tgt,
        jnp.where(pos, 0.0, cg0), zero,
        jnp.where(done0, one, zero), theta0, tie_thr0, cg0,
    )

    def one_pass(st):
        lo, hi, flo, fhi, chi, side, done, theta, tie_thr, tie_cgt = st
        interp = lo + (hi - lo) * (flo / (flo - fhi))
        bis = 0.5 * lo + 0.5 * hi
        mid = jnp.where(jnp.logical_and(interp > lo, interp < hi), interp, bis)
        inside = jnp.logical_and(mid > lo, mid < hi)
        c = count_gt(mid)
        active = done < 0.5
        live = jnp.logical_and(active, inside)
        hit = jnp.logical_and(live, c == kvec)
        stuck = jnp.logical_and(active, jnp.logical_not(inside))
        up = jnp.logical_and(live, c > kvec)
        dn = jnp.logical_and(live, c < kvec)
        theta = jnp.where(hit, mid, jnp.where(stuck, hi, theta))
        tie_thr = jnp.where(stuck, hi, tie_thr)
        tie_cgt = jnp.where(stuck, chi, tie_cgt)
        done = jnp.where(jnp.logical_or(hit, stuck), 1.0, done)
        fhi_n = jnp.where(dn, c - tgt, jnp.where(jnp.logical_and(up, side > 0.5), fhi * 0.5, fhi))
        flo_n = jnp.where(up, c - tgt, jnp.where(jnp.logical_and(dn, side < -0.5), flo * 0.5, flo))
        return (jnp.where(up, mid, lo), jnp.where(dn, mid, hi), flo_n, fhi_n,
                jnp.where(dn, c, chi), jnp.where(up, 1.0, jnp.where(dn, -1.0, side)),
                done, theta, tie_thr, tie_cgt)

    def check_body(carry):
        st = carry[2:]
        for _ in range(PASSES_PER_CHECK):
            st = one_pass(st)
        pending = jnp.max(jnp.where(st[6] < 0.5, 1, 0).astype(jnp.int32))
        return (carry[0] + 1, pending) + st

    def check_cond(carry):
        return jnp.logical_and(carry[1] > 0, carry[0] < MAX_CHECKS)

    init = init[:1] + (jnp.max(jnp.where(done0, 0, 1).astype(jnp.int32)),) + init[2:]
    final = lax.while_loop(check_cond, check_body, init)
    theta, tie_thr, tie_cgt = final[9], final[10], final[11]
    tie_budget = kvec - tie_cgt
    any_tie = jnp.max(jnp.where(tie_thr < jnp.inf, 1, 0).astype(jnp.int32))

    @pl.when(any_tie == 0)
    def _():
        def body(kt, c):
            k0 = tile_start(kt)
            s_ref[pl.ds(k0, TK), :] = jnp.where(s_ref[pl.ds(k0, TK), :] > theta, 0.0, NEG_BIAS)
            return c

        lax.fori_loop(0, nkt, body, 0)

    @pl.when(any_tie > 0)
    def _():
        ri = lax.broadcasted_iota(jnp.int32, (TK, TK), 0)
        ci = lax.broadcasted_iota(jnp.int32, (TK, TK), 1)
        lower = jnp.where(ci < ri, 1.0, 0.0).astype(BF16)

        def body(kt, ties_before):
            k0 = tile_start(kt)
            blk = s_ref[pl.ds(k0, TK), :]
            eq = jnp.where(blk == tie_thr, 1.0, 0.0)
            rank = jnp.dot(lower, eq.astype(BF16), preferred_element_type=F32) + ties_before
            keep = jnp.where(blk > theta, 1.0, jnp.where(rank < tie_budget, eq, 0.0))
            s_ref[pl.ds(k0, TK), :] = jnp.where(keep > 0.5, 0.0, NEG_BIAS)
            return ties_before + eq.sum(axis=0, keepdims=True)

        lax.fori_loop(0, nkt, body, zero)

    rows = lax.broadcasted_iota(jnp.int32, (LANES, TQ), 0)
    for h in range(ATTN_HEADS):
        pair = qT_ref[0, (h // 2) * LANES:(h // 2 + 1) * LANES, :]
        qz_ref[h] = jnp.where((rows // HEAD_DIM) == (h % 2), pair, jnp.zeros_like(pair))
    acc_ref[...] = jnp.zeros_like(acc_ref)
    l_ref[...] = jnp.zeros_like(l_ref)

    def scores(kt, slot, h, m_old):
        k0 = tile_start(kt)
        kp = k_ref[0, pl.ds(k0, TK), (h // 2) * LANES:(h // 2 + 1) * LANES]
        s = jnp.dot(kp, qz_ref[h], preferred_element_type=F32) + s_ref[pl.ds(k0, TK), :]
        sb_ref[slot, h] = s
        m_h = jnp.maximum(m_old, vreduce(s, jnp.maximum).max(axis=0, keepdims=True))
        m_ref[slot, h:h + 1, :] = m_h
        a_ref[slot, h:h + 1, :] = jnp.exp2(m_old - m_h)

    ones_rows = jnp.ones((BF16_ROWS, TK), BF16)

    def values(kt, slot, h):
        alpha = a_ref[slot, h:h + 1, :]
        p = jnp.exp2(sb_ref[slot, h] - m_ref[slot, h:h + 1, :]).astype(BF16)
        v_h = jnp.concatenate([vT_ref[0, kt, h * HEAD_DIM:(h + 1) * HEAD_DIM, :], ones_rows], axis=0)
        pv = jnp.dot(v_h, p, preferred_element_type=F32)
        l_ref[h:h + 1, :] = alpha * l_ref[h:h + 1, :] + pv[HEAD_DIM:HEAD_DIM + 1, :]
        acc_ref[h * HEAD_DIM:(h + 1) * HEAD_DIM, :] = (
            alpha * acc_ref[h * HEAD_DIM:(h + 1) * HEAD_DIM, :] + pv[0:HEAD_DIM, :])

    def step(kt, slot):
        for h in range(ATTN_HEADS):
            scores(kt + 1, 1 - slot, h, m_ref[slot, h:h + 1, :])
            values(kt, slot, h)

    for h in range(ATTN_HEADS):
        scores(0, 0, h, jnp.full((1, TQ), M_INIT, F32))

    def p3(i, c):
        step(2 * i, 0)
        step(2 * i + 1, 1)
        return c

    lax.fori_loop(0, (nkt - 1) // 2, p3, 0)

    @pl.when(nkt % 2 == 0)
    def _():
        step(nkt - 2, 0)
        for h in range(ATTN_HEADS):
            values(nkt - 1, 1, h)

    @pl.when(nkt % 2 == 1)
    def _():
        for h in range(ATTN_HEADS):
            values(nkt - 1, 0, h)

    l_all = l_ref[...]
    outs = []
    for h in range(ATTN_HEADS):
        outs.append(acc_ref[h * HEAD_DIM:(h + 1) * HEAD_DIM, :] / l_all[h:h + 1, :])
    o_ref[0] = jnp.concatenate(outs, axis=0).T.astype(BF16)


def _attn_call(qT, k, vT, iqT, ikw, ikwT, *, batch, seq):
    nq = seq // TQ
    nkt = seq // TK
    return pl.pallas_call(
        _attn_kernel,
        grid=(batch, nq),
        in_specs=[
            pl.BlockSpec((1, ATTN_WIDTH, TQ), lambda b, j: (b, 0, j)),
            pl.BlockSpec((1, seq, ATTN_WIDTH), lambda b, j: (b, 0, 0)),
            pl.BlockSpec((1, nkt, ATTN_WIDTH, TK), lambda b, j: (b, 0, 0, 0)),
            pl.BlockSpec((1, IDX_HEADS * LANES, TQ), lambda b, j: (b, 0, j)),
            pl.BlockSpec((1, seq, LANES), lambda b, j: (b, 0, 0)),
            pl.BlockSpec((1, LANES, TQ), lambda b, j: (b, 0, j)),
        ],
        out_specs=pl.BlockSpec((1, TQ, ATTN_WIDTH), lambda b, j: (b, j, 0)),
        out_shape=jax.ShapeDtypeStruct((batch, seq, ATTN_WIDTH), BF16),
        scratch_shapes=[
            pltpu.VMEM((seq, TQ), F32),
            pltpu.VMEM((2, ATTN_HEADS, TK, TQ), F32),
            pltpu.VMEM((ATTN_HEADS, LANES, TQ), BF16),
            pltpu.VMEM((ATTN_WIDTH, TQ), F32),
            pltpu.VMEM((2, ATTN_HEADS, TQ), F32),
            pltpu.VMEM((2, ATTN_HEADS, TQ), F32),
            pltpu.VMEM((ATTN_HEADS, TQ), F32),
        ],
        compiler_params=pltpu.CompilerParams(
            dimension_semantics=("arbitrary", "arbitrary"), vmem_limit_bytes=VMEM_LIMIT),
        name="dsa_attn",
    )(qT, k, vT, iqT, ikw, ikwT)


def _post_kernel(x_ref, at_ref, uc_ref, sg_ref, wao_ref, wco_ref, wmix_ref, gm_ref, wup_ref, wdn_ref,
                 gf_ref, o_ref, *, final):
    ya = jnp.dot(at_ref[...], wao_ref[...], preferred_element_type=F32)
    yc = jnp.dot(uc_ref[...], wco_ref[...], preferred_element_type=F32)
    merged = (sg_ref[:, 0:D_MODEL].astype(F32) * ya
              + sg_ref[:, D_MODEL:2 * D_MODEL].astype(F32) * yc)
    x1 = x_ref[...] + jnp.dot(merged.astype(BF16), wmix_ref[...], preferred_element_type=F32)
    u = _rmsnorm(x1, gm_ref[...]).astype(BF16)
    x2 = x1
    for c in range(MLP_HIDDEN // MLP_CHUNK):
        hid = jnp.dot(u, wup_ref[:, c * MLP_CHUNK:(c + 1) * MLP_CHUNK], preferred_element_type=F32)
        hid = jnp.square(jnp.maximum(hid, 0.0)).astype(BF16)
        x2 = x2 + jnp.dot(hid, wdn_ref[c * MLP_CHUNK:(c + 1) * MLP_CHUNK, :], preferred_element_type=F32)
    if final:
        x2 = _rmsnorm(x2, gf_ref[...])
    o_ref[...] = x2


def _post_call(x2d, attn, uc, sg, wao, wco, wmix, gm, wup, wdn, gf, *, final):
    m = x2d.shape[0]
    tm = TM_POST
    const = lambda i: (0, 0)
    row = lambda i: (i, 0)
    resident = lambda shape: pl.BlockSpec(shape, const, pipeline_mode=pl.Buffered(1))
    return pl.pallas_call(
        functools.partial(_post_kernel, final=final),
        grid=(m // tm,),
        in_specs=[
            pl.BlockSpec((tm, D_MODEL), row),
            pl.BlockSpec((tm, ATTN_WIDTH), row),
            pl.BlockSpec((tm, CONV_WIDTH), row),
            pl.BlockSpec((tm, 2 * D_MODEL), row),
            resident((ATTN_WIDTH, D_MODEL)),
            resident((CONV_WIDTH, D_MODEL)),
            resident((D_MODEL, D_MODEL)),
            pl.BlockSpec((1, D_MODEL), const),
            resident((D_MODEL, MLP_HIDDEN)),
            resident((MLP_HIDDEN, D_MODEL)),
            pl.BlockSpec((1, D_MODEL), const),
        ],
        out_specs=pl.BlockSpec((tm, D_MODEL), row),
        out_shape=jax.ShapeDtypeStruct((m, D_MODEL), F32),
        compiler_params=pltpu.CompilerParams(
            dimension_semantics=("arbitrary",), vmem_limit_bytes=VMEM_LIMIT),
        name="mixer_tail",
    )(x2d, attn, uc, sg, wao, wco, wmix, gm, wup, wdn, gf)


def _rope_tables(seq):
    half = ROPE_DIM // 2
    inv = 1.0 / (ROPE_THETA ** (jnp.arange(0, ROPE_DIM, 2, dtype=F32) / ROPE_DIM))
    ang = jnp.arange(seq, dtype=F32)[:, None] * inv[None, :]
    cos, sin = jnp.cos(ang), jnp.sin(ang)
    ones = jnp.ones((seq, HEAD_DIM - ROPE_DIM), F32)
    zeros = jnp.zeros((seq, HEAD_DIM - ROPE_DIM), F32)
    zh = jnp.zeros((seq, half), F32)
    cos_h = jnp.concatenate([cos, cos, ones], axis=1)
    sa_h = jnp.concatenate([-sin, zh, zeros], axis=1)
    sb_h = jnp.concatenate([zh, sin, zeros], axis=1)
    two = lambda t: jnp.concatenate([t, t], axis=1)
    return two(cos_h), two(sa_h), two(sb_h)


def _arrange_w_in_t(w):
    wt = w.T
    sizes = (ATTN_WIDTH, ATTN_WIDTH, ATTN_WIDTH, IDX_HEADS * IDX_DIM, IDX_DIM, IDX_HEADS,
             CONV_WIDTH, CONV_WIDTH, CONV_WIDTH, 2 * D_MODEL)
    offs = np.cumsum((0,) + sizes)
    q, k, v, iq, ik, iw, cb, cc, ch, g = [wt[offs[n]:offs[n + 1]] for n in range(len(sizes))]
    z = lambda n: jnp.zeros((n, wt.shape[1]), wt.dtype)
    iq_rows = []
    for h in range(IDX_HEADS):
        iq_rows += [iq[h * IDX_DIM:(h + 1) * IDX_DIM], z(LANES - IDX_DIM)]
    ikw = [ik, z(IW_LANE - IDX_DIM), iw, z(LANES - IW_LANE - IDX_HEADS)]
    return jnp.concatenate([q, k, v] + iq_rows + ikw + [cb, cc, ch, g], axis=0).astype(BF16)


def kernel(x, norm_mix, w_in, conv_w, w_attn_out, w_conv_out, w_mix_out, norm_mlp, w_mlp_up, w_mlp_down,
           norm_final):
    batch, seq, d = x.shape
    depth = w_in.shape[0]
    assert d == D_MODEL and seq % TM_PROJ == 0 and TM_PROJ % TK == 0 and TQ == TK
    assert seq // 4 >= TOPK_MAX
    cosv, sa, sb = _rope_tables(seq)
    h = x.reshape(batch * seq, d)
    for l in range(depth):
        qT, k, vT, iqT, ikw, ikwT, uc, sg = _proj_call(
            h, norm_mix[l][None, :], _arrange_w_in_t(w_in[l]), cosv, sa, sb, conv_w[l],
            batch=batch, seq=seq)
        attn = _attn_call(qT, k, vT, iqT, ikw, ikwT, batch=batch, seq=seq)
        h = _post_call(
            h, attn.reshape(batch * seq, ATTN_WIDTH), uc, sg,
            w_attn_out[l].astype(BF16), w_conv_out[l].astype(BF16), w_mix_out[l].astype(BF16),
            norm_mlp[l][None, :], w_mlp_up[l].astype(BF16), w_mlp_down[l].astype(BF16),
            norm_final[None, :], final=(l == depth - 1))
    return h.reshape(batch, seq, d)
```

```python
import functools

import jax
import jax.numpy as jnp
import numpy as np
from jax import lax
from jax.experimental import pallas as pl
from jax.experimental.pallas import tpu as pltpu

F32 = jnp.float32
BF16 = jnp.bfloat16

D_MODEL = 1024
CHUNK = 64
EPS = 1e-6
HEAD_DIM = 64
ATTN_WIDTH = 512
ATTN_HEADS = 8
ROPE_DIM = 16
ROPE_THETA = 500000.0
IDX_HEADS = 4
IDX_DIM = 64
IDX_SCALE = (IDX_DIM ** -0.5) * (IDX_HEADS ** -0.5)
TOPK_MAX = 256
CONV_WIDTH = 512
CONV_K = 3
MLP_HIDDEN = 4 * D_MODEL

LANES = 128
SUBLANES = 8
BF16_ROWS = 16
QK_SCALE = (HEAD_DIM ** -0.5) * float(np.log2(np.e))

OFF_Q = 0
OFF_K = OFF_Q + ATTN_WIDTH
OFF_V = OFF_K + ATTN_WIDTH
OFF_IQ = OFF_V + ATTN_WIDTH
OFF_IKW = OFF_IQ + IDX_HEADS * LANES
IW_LANE = 96
OFF_CB = OFF_IKW + LANES
OFF_CC = OFF_CB + CONV_WIDTH
OFF_CH = OFF_CC + CONV_WIDTH
OFF_G = OFF_CH + CONV_WIDTH
PROJ_COLS = OFF_G + 2 * D_MODEL

TM_PROJ = 512
TQ = 256
TK = 256
TM_POST = 256
MLP_CHUNK = 1024
PASSES_UNCHECKED = 10
PASSES_PER_CHECK = 3
MAX_CHECKS = 192
NEG_BIAS = -2e30
M_INIT = -1e30
VMEM_LIMIT = 48 * 1024 * 1024


def _rmsnorm(x, g):
    ms = jnp.mean(x * x, axis=-1, keepdims=True)
    return x * lax.rsqrt(ms + EPS) * g


def _proj_kernel(x_ref, g_ref, wt_ref, cos_ref, sa_ref, sb_ref, cw_ref,
                 qT_ref, k_ref, vT_ref, iqT_ref, ikw_ref, ikwT_ref, uc_ref, sg_ref,
                 zbuf, *, tiles_per_seq):
    i = pl.program_id(0)
    tm = x_ref.shape[0]
    u = _rmsnorm(x_ref[...], g_ref[...]).astype(BF16)
    cosv = cos_ref[...]
    sa = sa_ref[...]
    sb = sb_ref[...]

    def proj(c0, n):
        return lax.dot_general(u, wt_ref[c0:c0 + n, :], (((1,), (1,)), ((), ())),
                               preferred_element_type=F32)

    def rope(a):
        outs = []
        for gidx in range(a.shape[1] // LANES):
            ag = a[:, gidx * LANES:(gidx + 1) * LANES]
            outs.append(ag * cosv
                        + pltpu.roll(ag, LANES - ROPE_DIM // 2, 1) * sa
                        + pltpu.roll(ag, ROPE_DIM // 2, 1) * sb)
        return outs[0] if len(outs) == 1 else jnp.concatenate(outs, axis=1)

    q = rope(proj(OFF_Q, ATTN_WIDTH)) * QK_SCALE
    qT_ref[0] = q.T.astype(BF16)
    k_ref[0] = rope(proj(OFF_K, ATTN_WIDTH)).astype(BF16)
    v_t = proj(OFF_V, ATTN_WIDTH).T.astype(BF16)
    for t in range(tm // TK):
        vT_ref[0, t] = v_t[:, t * TK:(t + 1) * TK]
    iqT_ref[0] = rope(proj(OFF_IQ, IDX_HEADS * LANES)).T.astype(BF16)
    ikw = rope(proj(OFF_IKW, LANES))
    ikw_ref[0] = ikw.astype(BF16)
    ikwT_ref[0] = ikw.T

    z = proj(OFF_CC, CONV_WIDTH) * proj(OFF_CH, CONV_WIDTH)

    @pl.when(i % tiles_per_seq == 0)
    def _():
        zbuf[0:SUBLANES, :] = jnp.zeros((SUBLANES, CONV_WIDTH), F32)

    @pl.when(i % tiles_per_seq != 0)
    def _():
        zbuf[0:SUBLANES, :] = zbuf[tm:tm + SUBLANES, :]

    zbuf[SUBLANES:SUBLANES + tm, :] = z
    z1 = zbuf[SUBLANES - 1:SUBLANES - 1 + tm, :]
    z2 = zbuf[SUBLANES - 2:SUBLANES - 2 + tm, :]
    conv = z2 * cw_ref[0:1, :] + z1 * cw_ref[1:2, :] + z * cw_ref[2:3, :]
    uc_ref[...] = (proj(OFF_CB, CONV_WIDTH) * conv).astype(BF16)

    for c in range(4):
        gc = proj(OFF_G + c * 512, 512)
        sg_ref[:, c * 512:(c + 1) * 512] = jax.nn.sigmoid(gc).astype(BF16)


def _proj_call(x2d, gain, wt, cosv, sa, sb, cw, *, batch, seq):
    m = x2d.shape[0]
    tm = TM_PROJ
    nt = seq // tm
    kt_per_step = tm // TK
    const = lambda i: (0, 0)
    out_shape = (
        jax.ShapeDtypeStruct((batch, ATTN_WIDTH, seq), BF16),
        jax.ShapeDtypeStruct((batch, seq, ATTN_WIDTH), BF16),
        jax.ShapeDtypeStruct((batch, seq // TK, ATTN_WIDTH, TK), BF16),
        jax.ShapeDtypeStruct((batch, IDX_HEADS * LANES, seq), BF16),
        jax.ShapeDtypeStruct((batch, seq, LANES), BF16),
        jax.ShapeDtypeStruct((batch, LANES, seq), F32),
        jax.ShapeDtypeStruct((m, CONV_WIDTH), BF16),
        jax.ShapeDtypeStruct((m, 2 * D_MODEL), BF16),
    )
    in_specs = [
        pl.BlockSpec((tm, D_MODEL), lambda i: (i, 0)),
        pl.BlockSpec((1, D_MODEL), const),
        pl.BlockSpec((PROJ_COLS, D_MODEL), const, pipeline_mode=pl.Buffered(1)),
        pl.BlockSpec((tm, LANES), lambda i: (i % nt, 0)),
        pl.BlockSpec((tm, LANES), lambda i: (i % nt, 0)),
        pl.BlockSpec((tm, LANES), lambda i: (i % nt, 0)),
        pl.BlockSpec((CONV_K, CONV_WIDTH), const),
    ]
    out_specs = (
        pl.BlockSpec((1, ATTN_WIDTH, tm), lambda i: (i // nt, 0, i % nt)),
        pl.BlockSpec((1, tm, ATTN_WIDTH), lambda i: (i // nt, i % nt, 0)),
        pl.BlockSpec((1, kt_per_step, ATTN_WIDTH, TK), lambda i: (i // nt, i % nt, 0, 0)),
        pl.BlockSpec((1, IDX_HEADS * LANES, tm), lambda i: (i // nt, 0, i % nt)),
        pl.BlockSpec((1, tm, LANES), lambda i: (i // nt, i % nt, 0)),
        pl.BlockSpec((1, LANES, tm), lambda i: (i // nt, 0, i % nt)),
        pl.BlockSpec((tm, CONV_WIDTH), lambda i: (i, 0)),
        pl.BlockSpec((tm, 2 * D_MODEL), lambda i: (i, 0)),
    )
    return pl.pallas_call(
        functools.partial(_proj_kernel, tiles_per_seq=nt),
        grid=(m // tm,),
        in_specs=in_specs,
        out_specs=out_specs,
        out_shape=out_shape,
        scratch_shapes=[pltpu.VMEM((tm + SUBLANES, CONV_WIDTH), F32)],
        compiler_params=pltpu.CompilerParams(
            dimension_semantics=("arbitrary",), vmem_limit_bytes=VMEM_LIMIT),
        name="proj",
    )(x2d, gain, wt, cosv, sa, sb, cw)


def _attn_kernel(qT_ref, k_ref, vT_ref, iqT_ref, ikw_ref, ikwT_ref, o_ref,
                 s_ref, sb_ref, iq_ref, qz_ref, acc_ref, m_ref, a_ref, l_ref):
    j = pl.program_id(1)
    nkt = j + 1
    grp = TK // SUBLANES

    def tile_start(kt):
        return pl.multiple_of(kt * TK, TK)

    def vreduce(x, op):
        parts = [x[g * SUBLANES:(g + 1) * SUBLANES, :] for g in range(grp)]
        while len(parts) > 1:
            parts = [op(parts[i], parts[i + 1]) for i in range(0, len(parts), 2)]
        return parts[0]

    w_t = ikwT_ref[0, IW_LANE:IW_LANE + SUBLANES, :] * IDX_SCALE
    tpos = j * TQ + lax.broadcasted_iota(jnp.int32, (1, TQ), 1)
    limit = (tpos // CHUNK + 1) * CHUNK
    nadm = limit.astype(F32)
    kvec = jnp.minimum(limit, TOPK_MAX).astype(F32)

    iq_ref[...] = jnp.concatenate(
        [iqT_ref[0, h * LANES:(h + 1) * LANES, :] for h in range(IDX_HEADS)], axis=1)

    def idx_logits(kt):
        return jnp.dot(ikw_ref[0, pl.ds(tile_start(kt), TK), :], iq_ref[...],
                       preferred_element_type=F32)

    def idx_scores(kt, lg, amax):
        sc = jnp.maximum(lg[:, 0:TQ], 0.0) * w_t[0:1, :]
        for h in range(1, IDX_HEADS):
            sc = sc + jnp.maximum(lg[:, h * TQ:(h + 1) * TQ], 0.0) * w_t[h:h + 1, :]
        k0 = tile_start(kt)
        kpos = k0 + lax.broadcasted_iota(jnp.int32, (TK, 1), 0)
        s_ref[pl.ds(k0, TK), :] = jnp.where(kpos < limit, sc, -jnp.inf)
        return jnp.maximum(amax, vreduce(jnp.abs(sc), jnp.maximum))

    def p1_pair(i, amax):
        lg0 = idx_logits(2 * i)
        lg1 = idx_logits(2 * i + 1)
        return idx_scores(2 * i + 1, lg1, idx_scores(2 * i, lg0, amax))

    amax8 = lax.fori_loop(0, nkt // 2, p1_pair, jnp.zeros((SUBLANES, TQ), F32))
    amax8 = lax.cond(nkt % 2 == 1,
                     lambda a: idx_scores(nkt - 1, idx_logits(nkt - 1), a),
                     lambda a: a, amax8)
    maxabs = amax8.max(axis=0, keepdims=True)

    def count_gt(bound):
        def body(kt, a):
            blk = s_ref[pl.ds(tile_start(kt), TK), :]
            return a + vreduce(jnp.where(blk > bound, 1.0, 0.0), jnp.add)

        a = lax.fori_loop(0, nkt, body, jnp.zeros((SUBLANES, TQ), F32))
        return a.sum(axis=0, keepdims=True)

    def zero_counts():
        def body(kt, carry):
            g, e = carry
            blk = s_ref[pl.ds(tile_start(kt), TK), :]
            return (g + vreduce(jnp.where(blk > 0.0, 1.0, 0.0), jnp.add),
                    e + vreduce(jnp.where(blk >= 0.0, 1.0, 0.0), jnp.add))

        z8 = jnp.zeros((SUBLANES, TQ), F32)
        g, e = lax.fori_loop(0, nkt, body, (z8, z8))
        return g.sum(axis=0, keepdims=True), e.sum(axis=0, keepdims=True)

    cg0, ce0 = zero_counts()
    one = jnp.ones((1, TQ), F32)
    zero = jnp.zeros((1, TQ), F32)
    tgt = kvec - 0.5
    allsel = nadm <= kvec
    tie0 = jnp.logical_and(cg0 < kvec, ce0 >= kvec)
    pos = cg0 > kvec
    done0 = jnp.logical_or(allsel, jnp.logical_or(tie0, cg0 == kvec))
    theta0 = jnp.where(allsel, -jnp.inf, 0.0)
    tie_thr0 = jnp.where(jnp.logical_and(tie0, jnp.logical_not(allsel)), 0.0, jnp.inf)
    init = (
        jnp.int32(0), jnp.int32(1),
        jnp.where(pos, 0.0, -2.0 * maxabs), jnp.where(pos, maxabs, 0.0),
        jnp.where(pos, cg0, nadm) - tgt, jnp.where(pos, 0.0, ce0) - tgt,
        jnp.where(pos, 0.0, cg0), zero,
        jnp.where(done0, one, zero), theta0, tie_thr0, cg0,
    )

    def one_pass(st):
        lo, hi, flo, fhi, chi, side, done, theta, tie_thr, tie_cgt = st
        interp = lo + (hi - lo) * (flo / (flo - fhi))
        bis = 0.5 * lo + 0.5 * hi
        mid = jnp.where(jnp.logical_and(interp > lo, interp < hi), interp, bis)
        inside = jnp.logical_and(mid > lo, mid < hi)
        c = count_gt(mid)
        active = done < 0.5
        live = jnp.logical_and(active, inside)
        hit = jnp.logical_and(live, c == kvec)
        stuck = jnp.logical_and(active, jnp.logical_not(inside))
        up = jnp.logical_and(live, c > kvec)
        dn = jnp.logical_and(live, c < kvec)
        theta = jnp.where(hit, mid, jnp.where(stuck, hi, theta))
        tie_thr = jnp.where(stuck, hi, tie_thr)
        tie_cgt = jnp.where(stuck, chi, tie_cgt)
        done = jnp.where(jnp.logical_or(hit, stuck), 1.0, done)
        fhi_n = jnp.where(dn, c - tgt, jnp.where(jnp.logical_and(up, side > 0.5), fhi * 0.5, fhi))
        flo_n = jnp.where(up, c - tgt, jnp.where(jnp.logical_and(dn, side < -0.5), flo * 0.5, flo))
        return (jnp.where(up, mid, lo), jnp.where(dn, mid, hi), flo_n, fhi_n,
                jnp.where(dn, c, chi), jnp.where(up, 1.0, jnp.where(dn, -1.0, side)),
                done, theta, tie_thr, tie_cgt)

    st = lax.fori_loop(0, PASSES_UNCHECKED, lambda _, s: one_pass(s), init[2:])

    def pending_of(s):
        return jnp.max(jnp.where(s[6] < 0.5, 1, 0).astype(jnp.int32))

    def check_body(carry):
        s = carry[2:]
        for _ in range(PASSES_PER_CHECK):
            s = one_pass(s)
        return (carry[0] + 1, pending_of(s)) + s

    def check_cond(carry):
        return jnp.logical_and(carry[1] > 0, carry[0] < MAX_CHECKS)

    final = lax.while_loop(check_cond, check_body, (jnp.int32(0), pending_of(st)) + st)
    theta, tie_thr, tie_cgt = final[9], final[10], final[11]
    tie_budget = kvec - tie_cgt
    any_tie = jnp.max(jnp.where(tie_thr < jnp.inf, 1, 0).astype(jnp.int32))

    @pl.when(any_tie == 0)
    def _():
        def body(kt, c):
            k0 = tile_start(kt)
            s_ref[pl.ds(k0, TK), :] = jnp.where(s_ref[pl.ds(k0, TK), :] > theta, 0.0, NEG_BIAS)
            return c

        lax.fori_loop(0, nkt, body, 0)

    @pl.when(any_tie > 0)
    def _():
        ri = lax.broadcasted_iota(jnp.int32, (TK, TK), 0)
        ci = lax.broadcasted_iota(jnp.int32, (TK, TK), 1)
        lower = jnp.where(ci < ri, 1.0, 0.0).astype(BF16)

        def body(kt, ties_before):
            k0 = tile_start(kt)
            blk = s_ref[pl.ds(k0, TK), :]
            eq = jnp.where(blk == tie_thr, 1.0, 0.0)
            rank = jnp.dot(lower, eq.astype(BF16), preferred_element_type=F32) + ties_before
            keep = jnp.where(blk > theta, 1.0, jnp.where(rank < tie_budget, eq, 0.0))
            s_ref[pl.ds(k0, TK), :] = jnp.where(keep > 0.5, 0.0, NEG_BIAS)
            return ties_before + eq.sum(axis=0, keepdims=True)

        lax.fori_loop(0, nkt, body, zero)

    rows = lax.broadcasted_iota(jnp.int32, (LANES, TQ), 0)
    for h in range(ATTN_HEADS):
        pair = qT_ref[0, (h // 2) * LANES:(h // 2 + 1) * LANES, :]
        qz_ref[h] = jnp.where((rows // HEAD_DIM) == (h % 2), pair, jnp.zeros_like(pair))
    acc_ref[...] = jnp.zeros_like(acc_ref)
    l_ref[...] = jnp.zeros_like(l_ref)

    def scores(kt, slot, h, m_old):
        k0 = tile_start(kt)
        kp = k_ref[0, pl.ds(k0, TK), (h // 2) * LANES:(h // 2 + 1) * LANES]
        s = jnp.dot(kp, qz_ref[h], preferred_element_type=F32) + s_ref[pl.ds(k0, TK), :]
        sb_ref[slot, h] = s
        m_h = jnp.maximum(m_old, vreduce(s, jnp.maximum).max(axis=0, keepdims=True))
        m_ref[slot, h:h + 1, :] = m_h
        a_ref[slot, h:h + 1, :] = jnp.exp2(m_old - m_h)

    ones_rows = jnp.ones((BF16_ROWS, TK), BF16)

    def values(kt, slot, h):
        alpha = a_ref[slot, h:h + 1, :]
        p = jnp.exp2(sb_ref[slot, h] - m_ref[slot, h:h + 1, :]).astype(BF16)
        v_h = jnp.concatenate([vT_ref[0, kt, h * HEAD_DIM:(h + 1) * HEAD_DIM, :], ones_rows], axis=0)
        pv = jnp.dot(v_h, p, preferred_element_type=F32)
        l_ref[h:h + 1, :] = alpha * l_ref[h:h + 1, :] + pv[HEAD_DIM:HEAD_DIM + 1, :]
        acc_ref[h * HEAD_DIM:(h + 1) * HEAD_DIM, :] = (
            alpha * acc_ref[h * HEAD_DIM:(h + 1) * HEAD_DIM, :] + pv[0:HEAD_DIM, :])

    def step(kt, slot):
        for h in range(ATTN_HEADS):
            scores(kt + 1, 1 - slot, h, m_ref[slot, h:h + 1, :])
            values(kt, slot, h)

    for h in range(ATTN_HEADS):
        scores(0, 0, h, jnp.full((1, TQ), M_INIT, F32))

    def p3(i, c):
        step(2 * i, 0)
        step(2 * i + 1, 1)
        return c

    lax.fori_loop(0, (nkt - 1) // 2, p3, 0)

    @pl.when(nkt % 2 == 0)
    def _():
        step(nkt - 2, 0)
        for h in range(ATTN_HEADS):
            values(nkt - 1, 1, h)

    @pl.when(nkt % 2 == 1)
    def _():
        for h in range(ATTN_HEADS):
            values(nkt - 1, 0, h)

    l_all = l_ref[...]
    outs = []
    for h in range(ATTN_HEADS):
        outs.append(acc_ref[h * HEAD_DIM:(h + 1) * HEAD_DIM, :] / l_all[h:h + 1, :])
    o_ref[0] = jnp.concatenate(outs, axis=0).T.astype(BF16)


def _attn_call(qT, k, vT, iqT, ikw, ikwT, *, batch, seq):
    nq = seq // TQ
    nkt = seq // TK
    return pl.pallas_call(
        _attn_kernel,
        grid=(batch, nq),
        in_specs=[
            pl.BlockSpec((1, ATTN_WIDTH, TQ), lambda b, j: (b, 0, j)),
            pl.BlockSpec((1, seq, ATTN_WIDTH), lambda b, j: (b, 0, 0)),
            pl.BlockSpec((1, nkt, ATTN_WIDTH, TK), lambda b, j: (b, 0, 0, 0)),
            pl.BlockSpec((1, IDX_HEADS * LANES, TQ), lambda b, j: (b, 0, j)),
            pl.BlockSpec((1, seq, LANES), lambda b, j: (b, 0, 0)),
            pl.BlockSpec((1, LANES, TQ), lambda b, j: (b, 0, j)),
        ],
        out_specs=pl.BlockSpec((1, TQ, ATTN_WIDTH), lambda b, j: (b, j, 0)),
        out_shape=jax.ShapeDtypeStruct((batch, seq, ATTN_WIDTH), BF16),
        scratch_shapes=[
            pltpu.VMEM((seq, TQ), F32),
            pltpu.VMEM((2, ATTN_HEADS, TK, TQ), F32),
            pltpu.VMEM((LANES, IDX_HEADS * TQ), BF16),
            pltpu.VMEM((ATTN_HEADS, LANES, TQ), BF16),
            pltpu.VMEM((ATTN_WIDTH, TQ), F32),
            pltpu.VMEM((2, ATTN_HEADS, TQ), F32),
            pltpu.VMEM((2, ATTN_HEADS, TQ), F32),
            pltpu.VMEM((ATTN_HEADS, TQ), F32),
        ],
        compiler_params=pltpu.CompilerParams(
            dimension_semantics=("arbitrary", "arbitrary"), vmem_limit_bytes=VMEM_LIMIT),
        name="dsa_attn",
    )(qT, k, vT, iqT, ikw, ikwT)


def _post_kernel(x_ref, at_ref, uc_ref, sg_ref, wao_ref, wco_ref, wmix_ref, gm_ref, wup_ref, wdn_ref,
                 gf_ref, o_ref, *, final):
    ya = jnp.dot(at_ref[...], wao_ref[...], preferred_element_type=F32)
    yc = jnp.dot(uc_ref[...], wco_ref[...], preferred_element_type=F32)
    merged = (sg_ref[:, 0:D_MODEL].astype(F32) * ya
              + sg_ref[:, D_MODEL:2 * D_MODEL].astype(F32) * yc)
    x1 = x_ref[...] + jnp.dot(merged.astype(BF16), wmix_ref[...], preferred_element_type=F32)
    u = _rmsnorm(x1, gm_ref[...]).astype(BF16)
    x2 = x1
    for c in range(MLP_HIDDEN // MLP_CHUNK):
        hid = jnp.dot(u, wup_ref[:, c * MLP_CHUNK:(c + 1) * MLP_CHUNK], preferred_element_type=F32)
        hid = jnp.square(jnp.maximum(hid, 0.0)).astype(BF16)
        x2 = x2 + jnp.dot(hid, wdn_ref[c * MLP_CHUNK:(c + 1) * MLP_CHUNK, :], preferred_element_type=F32)
    if final:
        x2 = _rmsnorm(x2, gf_ref[...])
    o_ref[...] = x2


def _post_call(x2d, attn, uc, sg, wao, wco, wmix, gm, wup, wdn, gf, *, final):
    m = x2d.shape[0]
    tm = TM_POST
    const = lambda i: (0, 0)
    row = lambda i: (i, 0)
    resident = lambda shape: pl.BlockSpec(shape, const, pipeline_mode=pl.Buffered(1))
    return pl.pallas_call(
        functools.partial(_post_kernel, final=final),
        grid=(m // tm,),
        in_specs=[
            pl.BlockSpec((tm, D_MODEL), row),
            pl.BlockSpec((tm, ATTN_WIDTH), row),
            pl.BlockSpec((tm, CONV_WIDTH), row),
            pl.BlockSpec((tm, 2 * D_MODEL), row),
            resident((ATTN_WIDTH, D_MODEL)),
            resident((CONV_WIDTH, D_MODEL)),
            resident((D_MODEL, D_MODEL)),
            pl.BlockSpec((1, D_MODEL), const),
            resident((D_MODEL, MLP_HIDDEN)),
            resident((MLP_HIDDEN, D_MODEL)),
            pl.BlockSpec((1, D_MODEL), const),
        ],
        out_specs=pl.BlockSpec((tm, D_MODEL), row),
        out_shape=jax.ShapeDtypeStruct((m, D_MODEL), F32),
        compiler_params=pltpu.CompilerParams(
            dimension_semantics=("arbitrary",), vmem_limit_bytes=VMEM_LIMIT),
        name="mixer_tail",
    )(x2d, attn, uc, sg, wao, wco, wmix, gm, wup, wdn, gf)


def _rope_tables(seq):
    half = ROPE_DIM // 2
    inv = 1.0 / (ROPE_THETA ** (jnp.arange(0, ROPE_DIM, 2, dtype=F32) / ROPE_DIM))
    ang = jnp.arange(seq, dtype=F32)[:, None] * inv[None, :]
    cos, sin = jnp.cos(ang), jnp.sin(ang)
    ones = jnp.ones((seq, HEAD_DIM - ROPE_DIM), F32)
    zeros = jnp.zeros((seq, HEAD_DIM - ROPE_DIM), F32)
    zh = jnp.zeros((seq, half), F32)
    cos_h = jnp.concatenate([cos, cos, ones], axis=1)
    sa_h = jnp.concatenate([-sin, zh, zeros], axis=1)
    sb_h = jnp.concatenate([zh, sin, zeros], axis=1)
    two = lambda t: jnp.concatenate([t, t], axis=1)
    return two(cos_h), two(sa_h), two(sb_h)


def _arrange_w_in_t(w):
    wt = w.T
    sizes = (ATTN_WIDTH, ATTN_WIDTH, ATTN_WIDTH, IDX_HEADS * IDX_DIM, IDX_DIM, IDX_HEADS,
             CONV_WIDTH, CONV_WIDTH, CONV_WIDTH, 2 * D_MODEL)
    offs = np.cumsum((0,) + sizes)
    q, k, v, iq, ik, iw, cb, cc, ch, g = [wt[offs[n]:offs[n + 1]] for n in range(len(sizes))]
    z = lambda n: jnp.zeros((n, wt.shape[1]), wt.dtype)
    iq_rows = []
    for h in range(IDX_HEADS):
        iq_rows += [iq[h * IDX_DIM:(h + 1) * IDX_DIM], z(LANES - IDX_DIM)]
    ikw = [ik, z(IW_LANE - IDX_DIM), iw, z(LANES - IW_LANE - IDX_HEADS)]
    return jnp.concatenate([q, k, v] + iq_rows + ikw + [cb, cc, ch, g], axis=0).astype(BF16)


def kernel(x, norm_mix, w_in, conv_w, w_attn_out, w_conv_out, w_mix_out, norm_mlp, w_mlp_up, w_mlp_down,
           norm_final):
    batch, seq, d = x.shape
    depth = w_in.shape[0]
    assert d == D_MODEL and seq % TM_PROJ == 0 and TM_PROJ % TK == 0 and TQ == TK
    assert seq // 4 >= TOPK_MAX
    cosv, sa, sb = _rope_tables(seq)
    h = x.reshape(batch * seq, d)
    for l in range(depth):
        qT, k, vT, iqT, ikw, ikwT, uc, sg = _proj_call(
            h, norm_mix[l][None, :], _arrange_w_in_t(w_in[l]), cosv, sa, sb, conv_w[l],
            batch=batch, seq=seq)
        attn = _attn_call(qT, k, vT, iqT, ikw, ikwT, batch=batch, seq=seq)
        h = _post_call(
            h, attn.reshape(batch * seq, ATTN_WIDTH), uc, sg,
            w_attn_out[l].astype(BF16), w_conv_out[l].astype(BF16), w_mix_out[l].astype(BF16),
            norm_mlp[l][None, :], w_mlp_up[l].astype(BF16), w_mlp_down[l].astype(BF16),
            norm_final[None, :], final=(l == depth - 1))
    return h.reshape(batch, seq, d)
```

```python
import functools

import jax
import jax.numpy as jnp
import numpy as np
from jax import lax
from jax.experimental import pallas as pl
from jax.experimental.pallas import tpu as pltpu

F32 = jnp.float32
BF16 = jnp.bfloat16

D_MODEL = 1024
CHUNK = 64
EPS = 1e-6
HEAD_DIM = 64
ATTN_WIDTH = 512
ATTN_HEADS = 8
ROPE_DIM = 16
ROPE_THETA = 500000.0
IDX_HEADS = 4
IDX_DIM = 64
IDX_SCALE = (IDX_DIM ** -0.5) * (IDX_HEADS ** -0.5)
TOPK_MAX = 256
CONV_WIDTH = 512
CONV_K = 3
MLP_HIDDEN = 4 * D_MODEL

LANES = 128
SUBLANES = 8
BF16_ROWS = 16
QK_SCALE = (HEAD_DIM ** -0.5) * float(np.log2(np.e))

OFF_Q = 0
OFF_K = OFF_Q + ATTN_WIDTH
OFF_V = OFF_K + ATTN_WIDTH
OFF_IQ = OFF_V + ATTN_WIDTH
OFF_IKW = OFF_IQ + IDX_HEADS * LANES
IW_LANE = 96
OFF_CB = OFF_IKW + LANES
OFF_CC = OFF_CB + CONV_WIDTH
OFF_CH = OFF_CC + CONV_WIDTH
OFF_G = OFF_CH + CONV_WIDTH
PROJ_COLS = OFF_G + 2 * D_MODEL

TM_PROJ = 512
TQ = 256
TK = 256
TM_POST = 256
MLP_CHUNK = 1024
PASSES_UNCHECKED = 8
PASSES_PER_CHECK = 3
MAX_CHECKS = 192
GUESS_WIDEN = 1.3
STATUS_TIE = 1024.0
NEG_BIAS = -2e30
M_INIT = -1e30
VMEM_LIMIT = 48 * 1024 * 1024


def _rmsnorm(x, g):
    ms = jnp.mean(x * x, axis=-1, keepdims=True)
    return x * lax.rsqrt(ms + EPS) * g


def _proj_kernel(x_ref, g_ref, wt_ref, cos_ref, sa_ref, sb_ref, cw_ref,
                 qT_ref, k_ref, vT_ref, iqT_ref, ikw_ref, ikwT_ref, uc_ref, sg_ref,
                 zbuf, *, tiles_per_seq):
    i = pl.program_id(0)
    tm = x_ref.shape[0]
    u = _rmsnorm(x_ref[...], g_ref[...]).astype(BF16)
    cosv = cos_ref[...]
    sa = sa_ref[...]
    sb = sb_ref[...]

    def proj(c0, n):
        return lax.dot_general(u, wt_ref[c0:c0 + n, :], (((1,), (1,)), ((), ())),
                               preferred_element_type=F32)

    def rope(a):
        outs = []
        for gidx in range(a.shape[1] // LANES):
            ag = a[:, gidx * LANES:(gidx + 1) * LANES]
            outs.append(ag * cosv
                        + pltpu.roll(ag, LANES - ROPE_DIM // 2, 1) * sa
                        + pltpu.roll(ag, ROPE_DIM // 2, 1) * sb)
        return outs[0] if len(outs) == 1 else jnp.concatenate(outs, axis=1)

    q = rope(proj(OFF_Q, ATTN_WIDTH)) * QK_SCALE
    qT_ref[0] = q.T.astype(BF16)
    k_ref[0] = rope(proj(OFF_K, ATTN_WIDTH)).astype(BF16)
    v_t = proj(OFF_V, ATTN_WIDTH).T.astype(BF16)
    for t in range(tm // TK):
        vT_ref[0, t] = v_t[:, t * TK:(t + 1) * TK]
    iqT_ref[0] = rope(proj(OFF_IQ, IDX_HEADS * LANES)).T.astype(BF16)
    ikw = rope(proj(OFF_IKW, LANES))
    ikw_ref[0] = ikw.astype(BF16)
    ikwT_ref[0] = ikw.T

    z = proj(OFF_CC, CONV_WIDTH) * proj(OFF_CH, CONV_WIDTH)

    @pl.when(i % tiles_per_seq == 0)
    def _():
        zbuf[0:SUBLANES, :] = jnp.zeros((SUBLANES, CONV_WIDTH), F32)

    @pl.when(i % tiles_per_seq != 0)
    def _():
        zbuf[0:SUBLANES, :] = zbuf[tm:tm + SUBLANES, :]

    zbuf[SUBLANES:SUBLANES + tm, :] = z
    z1 = zbuf[SUBLANES - 1:SUBLANES - 1 + tm, :]
    z2 = zbuf[SUBLANES - 2:SUBLANES - 2 + tm, :]
    conv = z2 * cw_ref[0:1, :] + z1 * cw_ref[1:2, :] + z * cw_ref[2:3, :]
    uc_ref[...] = (proj(OFF_CB, CONV_WIDTH) * conv).astype(BF16)

    for c in range(4):
        gc = proj(OFF_G + c * 512, 512)
        sg_ref[:, c * 512:(c + 1) * 512] = jax.nn.sigmoid(gc).astype(BF16)


def _proj_call(x2d, gain, wt, cosv, sa, sb, cw, *, batch, seq):
    m = x2d.shape[0]
    tm = TM_PROJ
    nt = seq // tm
    kt_per_step = tm // TK
    const = lambda i: (0, 0)
    out_shape = (
        jax.ShapeDtypeStruct((batch, ATTN_WIDTH, seq), BF16),
        jax.ShapeDtypeStruct((batch, seq, ATTN_WIDTH), BF16),
        jax.ShapeDtypeStruct((batch, seq // TK, ATTN_WIDTH, TK), BF16),
        jax.ShapeDtypeStruct((batch, IDX_HEADS * LANES, seq), BF16),
        jax.ShapeDtypeStruct((batch, seq, LANES), BF16),
        jax.ShapeDtypeStruct((batch, LANES, seq), F32),
        jax.ShapeDtypeStruct((m, CONV_WIDTH), BF16),
        jax.ShapeDtypeStruct((m, 2 * D_MODEL), BF16),
    )
    in_specs = [
        pl.BlockSpec((tm, D_MODEL), lambda i: (i, 0)),
        pl.BlockSpec((1, D_MODEL), const),
        pl.BlockSpec((PROJ_COLS, D_MODEL), const, pipeline_mode=pl.Buffered(1)),
        pl.BlockSpec((tm, LANES), lambda i: (i % nt, 0)),
        pl.BlockSpec((tm, LANES), lambda i: (i % nt, 0)),
        pl.BlockSpec((tm, LANES), lambda i: (i % nt, 0)),
        pl.BlockSpec((CONV_K, CONV_WIDTH), const),
    ]
    out_specs = (
        pl.BlockSpec((1, ATTN_WIDTH, tm), lambda i: (i // nt, 0, i % nt)),
        pl.BlockSpec((1, tm, ATTN_WIDTH), lambda i: (i // nt, i % nt, 0)),
        pl.BlockSpec((1, kt_per_step, ATTN_WIDTH, TK), lambda i: (i // nt, i % nt, 0, 0)),
        pl.BlockSpec((1, IDX_HEADS * LANES, tm), lambda i: (i // nt, 0, i % nt)),
        pl.BlockSpec((1, tm, LANES), lambda i: (i // nt, i % nt, 0)),
        pl.BlockSpec((1, LANES, tm), lambda i: (i // nt, 0, i % nt)),
        pl.BlockSpec((tm, CONV_WIDTH), lambda i: (i, 0)),
        pl.BlockSpec((tm, 2 * D_MODEL), lambda i: (i, 0)),
    )
    return pl.pallas_call(
        functools.partial(_proj_kernel, tiles_per_seq=nt),
        grid=(m // tm,),
        in_specs=in_specs,
        out_specs=out_specs,
        out_shape=out_shape,
        scratch_shapes=[pltpu.VMEM((tm + SUBLANES, CONV_WIDTH), F32)],
        compiler_params=pltpu.CompilerParams(
            dimension_semantics=("arbitrary",), vmem_limit_bytes=VMEM_LIMIT),
        name="proj",
    )(x2d, gain, wt, cosv, sa, sb, cw)


def _attn_kernel(qT_ref, k_ref, vT_ref, iqT_ref, ikw_ref, ikwT_ref, o_ref,
                 s_ref, sb_ref, iq_ref, qz_ref, acc_ref, m_ref, a_ref, l_ref):
    j = pl.program_id(1)
    nkt = j + 1
    grp = TK // SUBLANES

    def tile_start(kt):
        return pl.multiple_of(kt * TK, TK)

    def vreduce(x, op):
        parts = [x[g * SUBLANES:(g + 1) * SUBLANES, :] for g in range(grp)]
        while len(parts) > 1:
            parts = [op(parts[i], parts[i + 1]) for i in range(0, len(parts), 2)]
        return parts[0]

    w_t = ikwT_ref[0, IW_LANE:IW_LANE + SUBLANES, :] * IDX_SCALE
    tpos = j * TQ + lax.broadcasted_iota(jnp.int32, (1, TQ), 1)
    limit = (tpos // CHUNK + 1) * CHUNK
    nadm = limit.astype(F32)
    kvec = jnp.minimum(limit, TOPK_MAX).astype(F32)

    iq_ref[...] = jnp.concatenate(
        [iqT_ref[0, h * LANES:(h + 1) * LANES, :] for h in range(IDX_HEADS)], axis=1)

    def idx_logits(kt):
        return jnp.dot(ikw_ref[0, pl.ds(tile_start(kt), TK), :], iq_ref[...],
                       preferred_element_type=F32)

    def idx_scores(kt, lg, stats):
        amax, cgt, cge, ssum, sabs = stats
        sc = jnp.maximum(lg[:, 0:TQ], 0.0) * w_t[0:1, :]
        for h in range(1, IDX_HEADS):
            sc = sc + jnp.maximum(lg[:, h * TQ:(h + 1) * TQ], 0.0) * w_t[h:h + 1, :]
        k0 = tile_start(kt)
        adm = (k0 + lax.broadcasted_iota(jnp.int32, (TK, 1), 0)) < limit
        sm = jnp.where(adm, sc, -jnp.inf)
        s_ref[pl.ds(k0, TK), :] = sm
        sz = jnp.where(adm, sc, 0.0)
        az = jnp.abs(sz)
        return (jnp.maximum(amax, vreduce(az, jnp.maximum)),
                cgt + vreduce(jnp.where(sm > 0.0, 1.0, 0.0), jnp.add),
                cge + vreduce(jnp.where(sm >= 0.0, 1.0, 0.0), jnp.add),
                ssum + vreduce(sz, jnp.add),
                sabs + vreduce(az, jnp.add))

    def p1_pair(i, stats):
        lg0 = idx_logits(2 * i)
        lg1 = idx_logits(2 * i + 1)
        return idx_scores(2 * i + 1, lg1, idx_scores(2 * i, lg0, stats))

    z8 = jnp.zeros((SUBLANES, TQ), F32)
    stats = lax.fori_loop(0, nkt // 2, p1_pair, (z8, z8, z8, z8, z8))
    stats = lax.cond(nkt % 2 == 1,
                     lambda st: idx_scores(nkt - 1, idx_logits(nkt - 1), st),
                     lambda st: st, stats)
    maxabs = stats[0].max(axis=0, keepdims=True)
    cg0, ce0, ssum, sabs = [x.sum(axis=0, keepdims=True) for x in stats[1:]]

    def count_gt(bound):
        b8 = jnp.broadcast_to(bound, (SUBLANES, TQ))

        def body(kt, accs):
            blk = s_ref[pl.ds(tile_start(kt), TK), :]
            accs = list(accs)
            for g in range(grp):
                part = blk[g * SUBLANES:(g + 1) * SUBLANES, :]
                accs[g % 4] = jnp.where(part > b8, accs[g % 4] + 1.0, accs[g % 4])
            return tuple(accs)

        z8 = jnp.zeros((SUBLANES, TQ), F32)
        accs = lax.fori_loop(0, nkt, body, (z8, z8, z8, z8))
        a = (accs[0] + accs[1]) + (accs[2] + accs[3])
        return a.sum(axis=0, keepdims=True)

    one = jnp.ones((1, TQ), F32)
    zero = jnp.zeros((1, TQ), F32)
    tgt = kvec - 0.5
    allsel = nadm <= kvec
    tie0 = jnp.logical_and(cg0 < kvec, ce0 >= kvec)
    pos = cg0 > kvec
    done0 = jnp.logical_or(allsel, jnp.logical_or(tie0, cg0 == kvec))
    theta0 = jnp.where(allsel, -jnp.inf, 0.0)
    tie_thr0 = jnp.where(jnp.logical_and(tie0, jnp.logical_not(allsel)), 0.0, jnp.inf)
    init = (
        jnp.int32(0), jnp.int32(1),
        jnp.where(pos, 0.0, -2.0 * maxabs), jnp.where(pos, maxabs, 0.0),
        jnp.where(pos, cg0, nadm) - tgt, jnp.where(pos, 0.0, ce0) - tgt,
        jnp.where(pos, 0.0, cg0), zero,
        jnp.where(done0, one, zero), theta0, tie_thr0, cg0,
    )

    def one_pass(st, hint=None):
        lo, hi, flo, fhi, chi, side, done, theta, tie_thr, tie_cgt = st
        interp = lo + (hi - lo) * (flo / (flo - fhi))
        bis = 0.5 * lo + 0.5 * hi
        mid = jnp.where(jnp.logical_and(interp > lo, interp < hi), interp, bis)
        if hint is not None:
            mid = jnp.where(jnp.logical_and(hint > lo, hint < hi), hint, mid)
        inside = jnp.logical_and(mid > lo, mid < hi)
        c = count_gt(mid)
        active = done < 0.5
        live = jnp.logical_and(active, inside)
        hit = jnp.logical_and(live, c == kvec)
        stuck = jnp.logical_and(active, jnp.logical_not(inside))
        up = jnp.logical_and(live, c > kvec)
        dn = jnp.logical_and(live, c < kvec)
        theta = jnp.where(hit, mid, jnp.where(stuck, hi, theta))
        tie_thr = jnp.where(stuck, hi, tie_thr)
        tie_cgt = jnp.where(stuck, chi, tie_cgt)
        done = jnp.where(jnp.logical_or(hit, stuck), 1.0, done)
        fhi_n = jnp.where(dn, c - tgt, jnp.where(jnp.logical_and(up, side > 0.5), fhi * 0.5, fhi))
        flo_n = jnp.where(up, c - tgt, jnp.where(jnp.logical_and(dn, side < -0.5), flo * 0.5, flo))
        return (jnp.where(up, mid, lo), jnp.where(dn, mid, hi), flo_n, fhi_n,
                jnp.where(dn, c, chi), jnp.where(up, 1.0, jnp.where(dn, -1.0, side)),
                done, theta, tie_thr, tie_cgt)

    npos = jnp.maximum(cg0, 1.0)
    nneg = jnp.maximum(nadm - ce0, 1.0)
    mean_pos = 0.5 * (sabs + ssum) / npos
    mean_neg = 0.5 * (sabs - ssum) / nneg
    frac_neg = jnp.minimum(jnp.maximum((kvec - ce0) / nneg, 1e-6), 1.0 - 1e-6)
    guess = jnp.where(pos, mean_pos * jnp.log(npos / kvec), mean_neg * jnp.log(1.0 - frac_neg))
    st = one_pass(init[2:], hint=guess)
    widen = jnp.where(pos, GUESS_WIDEN, 1.0 / GUESS_WIDEN)
    st = one_pass(st, hint=jnp.where(st[5] > 0.5, st[0] * widen, st[1] / widen))
    st = lax.fori_loop(0, PASSES_UNCHECKED - 2, lambda _, s: one_pass(s), st)

    def status_of(s):
        code = jnp.where(s[6] < 0.5, 1.0, 0.0) + jnp.where(s[8] < jnp.inf, STATUS_TIE, 0.0)
        return jnp.sum(code).astype(jnp.int32)

    def check_body(carry):
        s = carry[2:]
        for _ in range(PASSES_PER_CHECK):
            s = one_pass(s)
        return (carry[0] + 1, status_of(s)) + s

    def check_cond(carry):
        return jnp.logical_and(carry[1] % int(STATUS_TIE) > 0, carry[0] < MAX_CHECKS)

    final = lax.while_loop(check_cond, check_body, (jnp.int32(0), jnp.int32(1)) + st)
    theta, tie_thr, tie_cgt = final[9], final[10], final[11]
    tie_budget = kvec - tie_cgt
    any_tie = final[1] // int(STATUS_TIE)

    @pl.when(any_tie == 0)
    def _():
        def body(kt, c):
            k0 = tile_start(kt)
            s_ref[pl.ds(k0, TK), :] = jnp.where(s_ref[pl.ds(k0, TK), :] > theta, 0.0, NEG_BIAS)
            return c

        lax.fori_loop(0, nkt, body, 0)

    @pl.when(any_tie > 0)
    def _():
        ri = lax.broadcasted_iota(jnp.int32, (TK, TK), 0)
        ci = lax.broadcasted_iota(jnp.int32, (TK, TK), 1)
        lower = jnp.where(ci < ri, 1.0, 0.0).astype(BF16)

        def tie_rank(kt):
            is_tie = s_ref[pl.ds(tile_start(kt), TK), :] == tie_thr
            return jnp.dot(lower, jnp.where(is_tie, 1.0, 0.0).astype(BF16), preferred_element_type=F32)

        def write_bias(kt, rank, budget_left):
            k0 = tile_start(kt)
            blk = s_ref[pl.ds(k0, TK), :]
            is_tie = blk == tie_thr
            tie_bias = jnp.where(rank < budget_left, jnp.where(is_tie, 0.0, NEG_BIAS), NEG_BIAS)
            s_ref[pl.ds(k0, TK), :] = jnp.where(blk > theta, 0.0, tie_bias)
            return budget_left - (rank[TK - 1:TK, :] + jnp.where(is_tie[TK - 1:TK, :], 1.0, 0.0))

        def pair(i, budget_left):
            r0 = tie_rank(2 * i)
            r1 = tie_rank(2 * i + 1)
            return write_bias(2 * i + 1, r1, write_bias(2 * i, r0, budget_left))

        left = lax.fori_loop(0, nkt // 2, pair, tie_budget)

        @pl.when(nkt % 2 == 1)
        def _():
            write_bias(nkt - 1, tie_rank(nkt - 1), left)

    rows = lax.broadcasted_iota(jnp.int32, (LANES, TQ), 0)
    for h in range(ATTN_HEADS):
        pair = qT_ref[0, (h // 2) * LANES:(h // 2 + 1) * LANES, :]
        qz_ref[h] = jnp.where((rows // HEAD_DIM) == (h % 2), pair, jnp.zeros_like(pair))
    acc_ref[...] = jnp.zeros_like(acc_ref)
    l_ref[...] = jnp.zeros_like(l_ref)

    def scores(kt, slot, h, m_old):
        k0 = tile_start(kt)
        kp = k_ref[0, pl.ds(k0, TK), (h // 2) * LANES:(h // 2 + 1) * LANES]
        s = jnp.dot(kp, qz_ref[h], preferred_element_type=F32) + s_ref[pl.ds(k0, TK), :]
        sb_ref[slot, h] = s
        m_h = jnp.maximum(m_old, vreduce(s, jnp.maximum).max(axis=0, keepdims=True))
        m_ref[slot, h:h + 1, :] = m_h
        a_ref[slot, h:h + 1, :] = jnp.exp2(m_old - m_h)

    ones_rows = jnp.ones((BF16_ROWS, TK), BF16)

    def values(kt, slot, h):
        alpha = a_ref[slot, h:h + 1, :]
        p = jnp.exp2(sb_ref[slot, h] - m_ref[slot, h:h + 1, :]).astype(BF16)
        v_h = jnp.concatenate([vT_ref[0, kt, h * HEAD_DIM:(h + 1) * HEAD_DIM, :], ones_rows], axis=0)
        pv = jnp.dot(v_h, p, preferred_element_type=F32)
        l_ref[h:h + 1, :] = alpha * l_ref[h:h + 1, :] + pv[HEAD_DIM:HEAD_DIM + 1, :]
        acc_ref[h * HEAD_DIM:(h + 1) * HEAD_DIM, :] = (
            alpha * acc_ref[h * HEAD_DIM:(h + 1) * HEAD_DIM, :] + pv[0:HEAD_DIM, :])

    def step(kt, slot):
        for h in range(ATTN_HEADS):
            scores(kt + 1, 1 - slot, h, m_ref[slot, h:h + 1, :])
            values(kt, slot, h)

    for h in range(ATTN_HEADS):
        scores(0, 0, h, jnp.full((1, TQ), M_INIT, F32))

    def p3(i, c):
        step(2 * i, 0)
        step(2 * i + 1, 1)
        return c

    lax.fori_loop(0, (nkt - 1) // 2, p3, 0)

    @pl.when(nkt % 2 == 0)
    def _():
        step(nkt - 2, 0)
        for h in range(ATTN_HEADS):
            values(nkt - 1, 1, h)

    @pl.when(nkt % 2 == 1)
    def _():
        for h in range(ATTN_HEADS):
            values(nkt - 1, 0, h)

    l_all = l_ref[...]
    outs = []
    for h in range(ATTN_HEADS):
        outs.append(acc_ref[h * HEAD_DIM:(h + 1) * HEAD_DIM, :] / l_all[h:h + 1, :])
    o_ref[0] = jnp.concatenate(outs, axis=0).T.astype(BF16)


def _attn_call(qT, k, vT, iqT, ikw, ikwT, *, batch, seq):
    nq = seq // TQ
    nkt = seq // TK
    return pl.pallas_call(
        _attn_kernel,
        grid=(batch, nq),
        in_specs=[
            pl.BlockSpec((1, ATTN_WIDTH, TQ), lambda b, j: (b, 0, j)),
            pl.BlockSpec((1, seq, ATTN_WIDTH), lambda b, j: (b, 0, 0)),
            pl.BlockSpec((1, nkt, ATTN_WIDTH, TK), lambda b, j: (b, 0, 0, 0)),
            pl.BlockSpec((1, IDX_HEADS * LANES, TQ), lambda b, j: (b, 0, j)),
            pl.BlockSpec((1, seq, LANES), lambda b, j: (b, 0, 0)),
            pl.BlockSpec((1, LANES, TQ), lambda b, j: (b, 0, j)),
        ],
        out_specs=pl.BlockSpec((1, TQ, ATTN_WIDTH), lambda b, j: (b, j, 0)),
        out_shape=jax.ShapeDtypeStruct((batch, seq, ATTN_WIDTH), BF16),
        scratch_shapes=[
            pltpu.VMEM((seq, TQ), F32),
            pltpu.VMEM((2, ATTN_HEADS, TK, TQ), F32),
            pltpu.VMEM((LANES, IDX_HEADS * TQ), BF16),
            pltpu.VMEM((ATTN_HEADS, LANES, TQ), BF16),
            pltpu.VMEM((ATTN_WIDTH, TQ), F32),
            pltpu.VMEM((2, ATTN_HEADS, TQ), F32),
            pltpu.VMEM((2, ATTN_HEADS, TQ), F32),
            pltpu.VMEM((ATTN_HEADS, TQ), F32),
        ],
        compiler_params=pltpu.CompilerParams(
            dimension_semantics=("arbitrary", "arbitrary"), vmem_limit_bytes=VMEM_LIMIT),
        name="dsa_attn",
    )(qT, k, vT, iqT, ikw, ikwT)


def _post_kernel(x_ref, at_ref, uc_ref, sg_ref, wao_ref, wco_ref, wmix_ref, gm_ref, wup_ref, wdn_ref,
                 gf_ref, o_ref, *, final):
    ya = jnp.dot(at_ref[...], wao_ref[...], preferred_element_type=F32)
    yc = jnp.dot(uc_ref[...], wco_ref[...], preferred_element_type=F32)
    merged = (sg_ref[:, 0:D_MODEL].astype(F32) * ya
              + sg_ref[:, D_MODEL:2 * D_MODEL].astype(F32) * yc)
    x1 = x_ref[...] + jnp.dot(merged.astype(BF16), wmix_ref[...], preferred_element_type=F32)
    u = _rmsnorm(x1, gm_ref[...]).astype(BF16)
    x2 = x1
    for c in range(MLP_HIDDEN // MLP_CHUNK):
        hid = jnp.dot(u, wup_ref[:, c * MLP_CHUNK:(c + 1) * MLP_CHUNK], preferred_element_type=F32)
        hid = jnp.square(jnp.maximum(hid, 0.0)).astype(BF16)
        x2 = x2 + jnp.dot(hid, wdn_ref[c * MLP_CHUNK:(c + 1) * MLP_CHUNK, :], preferred_element_type=F32)
    if final:
        x2 = _rmsnorm(x2, gf_ref[...])
    o_ref[...] = x2


def _post_call(x2d, attn, uc, sg, wao, wco, wmix, gm, wup, wdn, gf, *, final):
    m = x2d.shape[0]
    tm = TM_POST
    const = lambda i: (0, 0)
    row = lambda i: (i, 0)
    resident = lambda shape: pl.BlockSpec(shape, const, pipeline_mode=pl.Buffered(1))
    return pl.pallas_call(
        functools.partial(_post_kernel, final=final),
        grid=(m // tm,),
        in_specs=[
            pl.BlockSpec((tm, D_MODEL), row),
            pl.BlockSpec((tm, ATTN_WIDTH), row),
            pl.BlockSpec((tm, CONV_WIDTH), row),
            pl.BlockSpec((tm, 2 * D_MODEL), row),
            resident((ATTN_WIDTH, D_MODEL)),
            resident((CONV_WIDTH, D_MODEL)),
            resident((D_MODEL, D_MODEL)),
            pl.BlockSpec((1, D_MODEL), const),
            resident((D_MODEL, MLP_HIDDEN)),
            resident((MLP_HIDDEN, D_MODEL)),
            pl.BlockSpec((1, D_MODEL), const),
        ],
        out_specs=pl.BlockSpec((tm, D_MODEL), row),
        out_shape=jax.ShapeDtypeStruct((m, D_MODEL), F32),
        compiler_params=pltpu.CompilerParams(
            dimension_semantics=("arbitrary",), vmem_limit_bytes=VMEM_LIMIT),
        name="mixer_tail",
    )(x2d, attn, uc, sg, wao, wco, wmix, gm, wup, wdn, gf)


def _rope_tables(seq):
    half = ROPE_DIM // 2
    inv = 1.0 / (ROPE_THETA ** (jnp.arange(0, ROPE_DIM, 2, dtype=F32) / ROPE_DIM))
    ang = jnp.arange(seq, dtype=F32)[:, None] * inv[None, :]
    cos, sin = jnp.cos(ang), jnp.sin(ang)
    ones = jnp.ones((seq, HEAD_DIM - ROPE_DIM), F32)
    zeros = jnp.zeros((seq, HEAD_DIM - ROPE_DIM), F32)
    zh = jnp.zeros((seq, half), F32)
    cos_h = jnp.concatenate([cos, cos, ones], axis=1)
    sa_h = jnp.concatenate([-sin, zh, zeros], axis=1)
    sb_h = jnp.concatenate([zh, sin, zeros], axis=1)
    two = lambda t: jnp.concatenate([t, t], axis=1)
    return two(cos_h), two(sa_h), two(sb_h)


def _arrange_w_in_t(w):
    wt = w.T
    sizes = (ATTN_WIDTH, ATTN_WIDTH, ATTN_WIDTH, IDX_HEADS * IDX_DIM, IDX_DIM, IDX_HEADS,
             CONV_WIDTH, CONV_WIDTH, CONV_WIDTH, 2 * D_MODEL)
    offs = np.cumsum((0,) + sizes)
    q, k, v, iq, ik, iw, cb, cc, ch, g = [wt[offs[n]:offs[n + 1]] for n in range(len(sizes))]
    z = lambda n: jnp.zeros((n, wt.shape[1]), wt.dtype)
    iq_rows = []
    for h in range(IDX_HEADS):
        iq_rows += [iq[h * IDX_DIM:(h + 1) * IDX_DIM], z(LANES - IDX_DIM)]
    ikw = [ik, z(IW_LANE - IDX_DIM), iw, z(LANES - IW_LANE - IDX_HEADS)]
    return jnp.concatenate([q, k, v] + iq_rows + ikw + [cb, cc, ch, g], axis=0).astype(BF16)


def kernel(x, norm_mix, w_in, conv_w, w_attn_out, w_conv_out, w_mix_out, norm_mlp, w_mlp_up, w_mlp_down,
           norm_final):
    batch, seq, d = x.shape
    depth = w_in.shape[0]
    assert d == D_MODEL and seq % TM_PROJ == 0 and TM_PROJ % TK == 0 and TQ == TK
    assert seq // 4 >= TOPK_MAX
    cosv, sa, sb = _rope_tables(seq)
    h = x.reshape(batch * seq, d)
    for l in range(depth):
        qT, k, vT, iqT, ikw, ikwT, uc, sg = _proj_call(
            h, norm_mix[l][None, :], _arrange_w_in_t(w_in[l]), cosv, sa, sb, conv_w[l],
            batch=batch, seq=seq)
        attn = _attn_call(qT, k, vT, iqT, ikw, ikwT, batch=batch, seq=seq)
        h = _post_call(
            h, attn.reshape(batch * seq, ATTN_WIDTH), uc, sg,
            w_attn_out[l].astype(BF16), w_conv_out[l].astype(BF16), w_mix_out[l].astype(BF16),
            norm_mlp[l][None, :], w_mlp_up[l].astype(BF16), w_mlp_down[l].astype(BF16),
            norm_final[None, :], final=(l == depth - 1))
    return h.reshape(batch, seq, d)
```

```python
import functools

import jax
import jax.numpy as jnp
import numpy as np
from jax import lax
from jax.experimental import pallas as pl
from jax.experimental.pallas import tpu as pltpu

F32 = jnp.float32
BF16 = jnp.bfloat16

D_MODEL = 1024
CHUNK = 64
EPS = 1e-6
HEAD_DIM = 64
ATTN_WIDTH = 512
ATTN_HEADS = 8
ROPE_DIM = 16
ROPE_THETA = 500000.0
IDX_HEADS = 4
IDX_DIM = 64
IDX_SCALE = (IDX_DIM ** -0.5) * (IDX_HEADS ** -0.5)
TOPK_MAX = 256
CONV_WIDTH = 512
CONV_K = 3
MLP_HIDDEN = 4 * D_MODEL

LANES = 128
SUBLANES = 8
BF16_ROWS = 16
QK_SCALE = (HEAD_DIM ** -0.5) * float(np.log2(np.e))

WBLK = 512
OFF_Q = 0
OFF_K = OFF_Q + ATTN_WIDTH
OFF_V = OFF_K + ATTN_WIDTH
OFF_IQ = OFF_V + ATTN_WIDTH
OFF_IKW = OFF_IQ + IDX_HEADS * LANES
IW_LANE = 96
OFF_CB = OFF_IKW + WBLK
OFF_CC = OFF_CB + CONV_WIDTH
OFF_CH = OFF_CC + CONV_WIDTH
OFF_G = OFF_CH + CONV_WIDTH
PROJ_COLS = OFF_G + 2 * D_MODEL
SRC_IQ = 3 * ATTN_WIDTH
SRC_IK = SRC_IQ + IDX_HEADS * IDX_DIM
SRC_IW = SRC_IK + IDX_DIM
SRC_CB = SRC_IW + IDX_HEADS

TM_PROJ = 512
TQ = 256
TK = 256
TM_POST = 256
MLP_CHUNK = 1024
PASSES_UNCHECKED = 8
PASSES_PER_CHECK = 3
MAX_CHECKS = 192
GUESS_WIDEN = 1.3
STATUS_TIE = 1024.0
NEG_BIAS = -2e30
M_INIT = -1e30
VMEM_LIMIT = 48 * 1024 * 1024


def _rmsnorm(x, g):
    ms = jnp.mean(x * x, axis=-1, keepdims=True)
    return x * lax.rsqrt(ms + EPS) * g


def _wprep_kernel(a_ref, b_ref, o_ref):
    s = pl.program_id(0)
    n_plain = OFF_IQ // WBLK

    def zeros(n):
        return jnp.zeros((n, D_MODEL), BF16)

    @pl.when(s < n_plain)
    def _():
        o_ref[...] = a_ref[...]

    @pl.when(s == n_plain)
    def _():
        base = SRC_IQ % WBLK
        parts = []
        for h in range(IDX_HEADS):
            parts += [a_ref[base + h * IDX_DIM:base + (h + 1) * IDX_DIM, :], zeros(LANES - IDX_DIM)]
        o_ref[...] = jnp.concatenate(parts, axis=0)

    @pl.when(s == n_plain + 1)
    def _():
        ik0 = SRC_IK % WBLK
        iw0 = SRC_IW % WBLK
        rows = lax.broadcasted_iota(jnp.int32, (BF16_ROWS, D_MODEL), 0)
        iw = jnp.where(rows < IDX_HEADS, a_ref[iw0:iw0 + BF16_ROWS, :].astype(F32), 0.0).astype(BF16)
        o_ref[...] = jnp.concatenate(
            [a_ref[ik0:ik0 + IDX_DIM, :], zeros(IW_LANE - IDX_DIM), iw, zeros(WBLK - IW_LANE - BF16_ROWS)],
            axis=0)

    @pl.when(s > n_plain + 1)
    def _():
        shift = SRC_CB % WBLK
        for c in range(D_MODEL // (2 * LANES)):
            cols = slice(c * 2 * LANES, (c + 1) * 2 * LANES)
            both = jnp.concatenate([a_ref[:, cols].astype(F32), b_ref[:, cols].astype(F32)], axis=0)
            o_ref[:, cols] = pltpu.roll(both, 2 * WBLK - shift, 0)[0:WBLK, :].astype(BF16)


def _wprep_call(wt16):
    n_plain = OFF_IQ // WBLK
    blk0 = SRC_CB // WBLK
    assert SRC_IQ // WBLK == SRC_IK // WBLK == SRC_IW // WBLK == blk0 == n_plain
    assert SRC_IW % WBLK % BF16_ROWS == 0 and (SRC_IW % WBLK) + BF16_ROWS <= WBLK

    def blk_a(s):
        return jnp.where(s < n_plain, s, jnp.where(s < n_plain + 2, blk0, s - (n_plain + 2) + blk0))

    def blk_b(s):
        return jnp.where(s < n_plain + 2, 0, s - (n_plain + 2) + blk0 + 1)

    return pl.pallas_call(
        _wprep_kernel,
        grid=(PROJ_COLS // WBLK,),
        in_specs=[pl.BlockSpec((WBLK, D_MODEL), lambda s: (blk_a(s), 0)),
                  pl.BlockSpec((WBLK, D_MODEL), lambda s: (blk_b(s), 0))],
        out_specs=pl.BlockSpec((WBLK, D_MODEL), lambda s: (s, 0)),
        out_shape=jax.ShapeDtypeStruct((PROJ_COLS, D_MODEL), BF16),
        compiler_params=pltpu.CompilerParams(
            dimension_semantics=("arbitrary",), vmem_limit_bytes=VMEM_LIMIT),
        name="w_arrange",
    )(wt16, wt16)


def _proj_kernel(x_ref, g_ref, wt_ref, cos_ref, sa_ref, sb_ref, cw_ref,
                 qT_ref, k_ref, vT_ref, iqT_ref, ikw_ref, ikwT_ref, uc_ref, sg_ref,
                 zbuf, *, tiles_per_seq):
    i = pl.program_id(0)
    tm = x_ref.shape[0]
    u = _rmsnorm(x_ref[...], g_ref[...]).astype(BF16)
    cosv = cos_ref[...]
    sa = sa_ref[...]
    sb = sb_ref[...]

    def proj(c0, n):
        return lax.dot_general(u, wt_ref[c0:c0 + n, :], (((1,), (1,)), ((), ())),
                               preferred_element_type=F32)

    def rope(a):
        outs = []
        for gidx in range(a.shape[1] // LANES):
            ag = a[:, gidx * LANES:(gidx + 1) * LANES]
            outs.append(ag * cosv
                        + pltpu.roll(ag, LANES - ROPE_DIM // 2, 1) * sa
                        + pltpu.roll(ag, ROPE_DIM // 2, 1) * sb)
        return outs[0] if len(outs) == 1 else jnp.concatenate(outs, axis=1)

    q = rope(proj(OFF_Q, ATTN_WIDTH)) * QK_SCALE
    qT_ref[0] = q.T.astype(BF16)
    k_ref[0] = rope(proj(OFF_K, ATTN_WIDTH)).astype(BF16)
    v_t = proj(OFF_V, ATTN_WIDTH).T.astype(BF16)
    for t in range(tm // TK):
        vT_ref[0, t] = v_t[:, t * TK:(t + 1) * TK]
    iqT_ref[0] = rope(proj(OFF_IQ, IDX_HEADS * LANES)).T.astype(BF16)
    ikw = rope(proj(OFF_IKW, LANES))
    ikw_ref[0] = ikw.astype(BF16)
    ikwT_ref[0] = ikw.T

    z = proj(OFF_CC, CONV_WIDTH) * proj(OFF_CH, CONV_WIDTH)

    @pl.when(i % tiles_per_seq == 0)
    def _():
        zbuf[0:SUBLANES, :] = jnp.zeros((SUBLANES, CONV_WIDTH), F32)

    @pl.when(i % tiles_per_seq != 0)
    def _():
        zbuf[0:SUBLANES, :] = zbuf[tm:tm + SUBLANES, :]

    zbuf[SUBLANES:SUBLANES + tm, :] = z
    z1 = zbuf[SUBLANES - 1:SUBLANES - 1 + tm, :]
    z2 = zbuf[SUBLANES - 2:SUBLANES - 2 + tm, :]
    conv = z2 * cw_ref[0:1, :] + z1 * cw_ref[1:2, :] + z * cw_ref[2:3, :]
    uc_ref[...] = (proj(OFF_CB, CONV_WIDTH) * conv).astype(BF16)

    for c in range(4):
        gc = proj(OFF_G + c * 512, 512)
        sg_ref[:, c * 512:(c + 1) * 512] = jax.nn.sigmoid(gc).astype(BF16)


def _proj_call(x2d, gain, wt, cosv, sa, sb, cw, *, batch, seq):
    m = x2d.shape[0]
    tm = TM_PROJ
    nt = seq // tm
    kt_per_step = tm // TK
    const = lambda i: (0, 0)
    out_shape = (
        jax.ShapeDtypeStruct((batch, ATTN_WIDTH, seq), BF16),
        jax.ShapeDtypeStruct((batch, seq, ATTN_WIDTH), BF16),
        jax.ShapeDtypeStruct((batch, seq // TK, ATTN_WIDTH, TK), BF16),
        jax.ShapeDtypeStruct((batch, IDX_HEADS * LANES, seq), BF16),
        jax.ShapeDtypeStruct((batch, seq, LANES), BF16),
        jax.ShapeDtypeStruct((batch, LANES, seq), F32),
        jax.ShapeDtypeStruct((m, CONV_WIDTH), BF16),
        jax.ShapeDtypeStruct((m, 2 * D_MODEL), BF16),
    )
    in_specs = [
        pl.BlockSpec((tm, D_MODEL), lambda i: (i, 0)),
        pl.BlockSpec((1, D_MODEL), const),
        pl.BlockSpec((PROJ_COLS, D_MODEL), const, pipeline_mode=pl.Buffered(1)),
        pl.BlockSpec((tm, LANES), lambda i: (i % nt, 0)),
        pl.BlockSpec((tm, LANES), lambda i: (i % nt, 0)),
        pl.BlockSpec((tm, LANES), lambda i: (i % nt, 0)),
        pl.BlockSpec((CONV_K, CONV_WIDTH), const),
    ]
    out_specs = (
        pl.BlockSpec((1, ATTN_WIDTH, tm), lambda i: (i // nt, 0, i % nt)),
        pl.BlockSpec((1, tm, ATTN_WIDTH), lambda i: (i // nt, i % nt, 0)),
        pl.BlockSpec((1, kt_per_step, ATTN_WIDTH, TK), lambda i: (i // nt, i % nt, 0, 0)),
        pl.BlockSpec((1, IDX_HEADS * LANES, tm), lambda i: (i // nt, 0, i % nt)),
        pl.BlockSpec((1, tm, LANES), lambda i: (i // nt, i % nt, 0)),
        pl.BlockSpec((1, LANES, tm), lambda i: (i // nt, 0, i % nt)),
        pl.BlockSpec((tm, CONV_WIDTH), lambda i: (i, 0)),
        pl.BlockSpec((tm, 2 * D_MODEL), lambda i: (i, 0)),
    )
    return pl.pallas_call(
        functools.partial(_proj_kernel, tiles_per_seq=nt),
        grid=(m // tm,),
        in_specs=in_specs,
        out_specs=out_specs,
        out_shape=out_shape,
        scratch_shapes=[pltpu.VMEM((tm + SUBLANES, CONV_WIDTH), F32)],
        compiler_params=pltpu.CompilerParams(
            dimension_semantics=("arbitrary",), vmem_limit_bytes=VMEM_LIMIT),
        name="proj",
    )(x2d, gain, wt, cosv, sa, sb, cw)


def _attn_kernel(qT_ref, k_ref, vT_ref, iqT_ref, ikw_ref, ikwT_ref, o_ref,
                 s_ref, sb_ref, iq_ref, qz_ref, acc_ref, m_ref, a_ref, l_ref):
    j = pl.program_id(1)
    nkt = j + 1
    grp = TK // SUBLANES

    def tile_start(kt):
        return pl.multiple_of(kt * TK, TK)

    def vreduce(x, op):
        parts = [x[g * SUBLANES:(g + 1) * SUBLANES, :] for g in range(grp)]
        while len(parts) > 1:
            parts = [op(parts[i], parts[i + 1]) for i in range(0, len(parts), 2)]
        return parts[0]

    w_t = ikwT_ref[0, IW_LANE:IW_LANE + SUBLANES, :] * IDX_SCALE
    tpos = j * TQ + lax.broadcasted_iota(jnp.int32, (1, TQ), 1)
    limit = (tpos // CHUNK + 1) * CHUNK
    nadm = limit.astype(F32)
    kvec = jnp.minimum(limit, TOPK_MAX).astype(F32)

    iq_ref[...] = jnp.concatenate(
        [iqT_ref[0, h * LANES:(h + 1) * LANES, :] for h in range(IDX_HEADS)], axis=1)

    def idx_logits(kt):
        return jnp.dot(ikw_ref[0, pl.ds(tile_start(kt), TK), :], iq_ref[...],
                       preferred_element_type=F32)

    def idx_scores(kt, lg, stats):
        amax, cgt, cge, ssum, sabs = stats
        sc = jnp.maximum(lg[:, 0:TQ], 0.0) * w_t[0:1, :]
        for h in range(1, IDX_HEADS):
            sc = sc + jnp.maximum(lg[:, h * TQ:(h + 1) * TQ], 0.0) * w_t[h:h + 1, :]
        k0 = tile_start(kt)
        adm = (k0 + lax.broadcasted_iota(jnp.int32, (TK, 1), 0)) < limit
        sm = jnp.where(adm, sc, -jnp.inf)
        s_ref[pl.ds(k0, TK), :] = sm
        sz = jnp.where(adm, sc, 0.0)
        az = jnp.abs(sz)
        return (jnp.maximum(amax, vreduce(az, jnp.maximum)),
                cgt + vreduce(jnp.where(sm > 0.0, 1.0, 0.0), jnp.add),
                cge + vreduce(jnp.where(sm >= 0.0, 1.0, 0.0), jnp.add),
                ssum + vreduce(sz, jnp.add),
                sabs + vreduce(az, jnp.add))

    def p1_pair(i, stats):
        lg0 = idx_logits(2 * i)
        lg1 = idx_logits(2 * i + 1)
        return idx_scores(2 * i + 1, lg1, idx_scores(2 * i, lg0, stats))

    z8 = jnp.zeros((SUBLANES, TQ), F32)
    stats = lax.fori_loop(0, nkt // 2, p1_pair, (z8, z8, z8, z8, z8))
    stats = lax.cond(nkt % 2 == 1,
                     lambda st: idx_scores(nkt - 1, idx_logits(nkt - 1), st),
                     lambda st: st, stats)
    maxabs = stats[0].max(axis=0, keepdims=True)
    cg0, ce0, ssum, sabs = [x.sum(axis=0, keepdims=True) for x in stats[1:]]

    def count_gt(bound):
        b8 = jnp.broadcast_to(bound, (SUBLANES, TQ))

        def body(kt, accs):
            blk = s_ref[pl.ds(tile_start(kt), TK), :]
            accs = list(accs)
            for g in range(grp):
                part = blk[g * SUBLANES:(g + 1) * SUBLANES, :]
                accs[g % 4] = jnp.where(part > b8, accs[g % 4] + 1.0, accs[g % 4])
            return tuple(accs)

        z8 = jnp.zeros((SUBLANES, TQ), F32)
        accs = lax.fori_loop(0, nkt, body, (z8, z8, z8, z8))
        a = (accs[0] + accs[1]) + (accs[2] + accs[3])
        return a.sum(axis=0, keepdims=True)

    one = jnp.ones((1, TQ), F32)
    zero = jnp.zeros((1, TQ), F32)
    tgt = kvec - 0.5
    allsel = nadm <= kvec
    tie0 = jnp.logical_and(cg0 < kvec, ce0 >= kvec)
    pos = cg0 > kvec
    done0 = jnp.logical_or(allsel, jnp.logical_or(tie0, cg0 == kvec))
    theta0 = jnp.where(allsel, -jnp.inf, 0.0)
    tie_thr0 = jnp.where(jnp.logical_and(tie0, jnp.logical_not(allsel)), 0.0, jnp.inf)
    init = (
        jnp.int32(0), jnp.int32(1),
        jnp.where(pos, 0.0, -2.0 * maxabs), jnp.where(pos, maxabs, 0.0),
        jnp.where(pos, cg0, nadm) - tgt, jnp.where(pos, 0.0, ce0) - tgt,
        jnp.where(pos, 0.0, cg0), zero,
        jnp.where(done0, one, zero), theta0, tie_thr0, cg0,
    )

    def one_pass(st, hint=None):
        lo, hi, flo, fhi, chi, side, done, theta, tie_thr, tie_cgt = st
        interp = lo + (hi - lo) * (flo / (flo - fhi))
        bis = 0.5 * lo + 0.5 * hi
        mid = jnp.where(jnp.logical_and(interp > lo, interp < hi), interp, bis)
        if hint is not None:
            mid = jnp.where(jnp.logical_and(hint > lo, hint < hi), hint, mid)
        inside = jnp.logical_and(mid > lo, mid < hi)
        c = count_gt(mid)
        active = done < 0.5
        live = jnp.logical_and(active, inside)
        hit = jnp.logical_and(live, c == kvec)
        stuck = jnp.logical_and(active, jnp.logical_not(inside))
        up = jnp.logical_and(live, c > kvec)
        dn = jnp.logical_and(live, c < kvec)
        theta = jnp.where(hit, mid, jnp.where(stuck, hi, theta))
        tie_thr = jnp.where(stuck, hi, tie_thr)
        tie_cgt = jnp.where(stuck, chi, tie_cgt)
        done = jnp.where(jnp.logical_or(hit, stuck), 1.0, done)
        fhi_n = jnp.where(dn, c - tgt, jnp.where(jnp.logical_and(up, side > 0.5), fhi * 0.5, fhi))
        flo_n = jnp.where(up, c - tgt, jnp.where(jnp.logical_and(dn, side < -0.5), flo * 0.5, flo))
        return (jnp.where(up, mid, lo), jnp.where(dn, mid, hi), flo_n, fhi_n,
                jnp.where(dn, c, chi), jnp.where(up, 1.0, jnp.where(dn, -1.0, side)),
                done, theta, tie_thr, tie_cgt)

    npos = jnp.maximum(cg0, 1.0)
    nneg = jnp.maximum(nadm - ce0, 1.0)
    mean_pos = 0.5 * (sabs + ssum) / npos
    mean_neg = 0.5 * (sabs - ssum) / nneg
    frac_neg = jnp.minimum(jnp.maximum((kvec - ce0) / nneg, 1e-6), 1.0 - 1e-6)
    guess = jnp.where(pos, mean_pos * jnp.log(npos / kvec), mean_neg * jnp.log(1.0 - frac_neg))
    st = one_pass(init[2:], hint=guess)
    widen = jnp.where(pos, GUESS_WIDEN, 1.0 / GUESS_WIDEN)
    st = one_pass(st, hint=jnp.where(st[5] > 0.5, st[0] * widen, st[1] / widen))
    st = lax.fori_loop(0, PASSES_UNCHECKED - 2, lambda _, s: one_pass(s), st)

    def status_of(s):
        code = jnp.where(s[6] < 0.5, 1.0, 0.0) + jnp.where(s[8] < jnp.inf, STATUS_TIE, 0.0)
        return jnp.sum(code).astype(jnp.int32)

    def check_body(carry):
        s = carry[2:]
        for _ in range(PASSES_PER_CHECK):
            s = one_pass(s)
        return (carry[0] + 1, status_of(s)) + s

    def check_cond(carry):
        return jnp.logical_and(carry[1] % int(STATUS_TIE) > 0, carry[0] < MAX_CHECKS)

    final = lax.while_loop(check_cond, check_body, (jnp.int32(0), jnp.int32(1)) + st)
    theta, tie_thr, tie_cgt = final[9], final[10], final[11]
    tie_budget = kvec - tie_cgt
    any_tie = final[1] // int(STATUS_TIE)

    @pl.when(any_tie == 0)
    def _():
        def body(kt, c):
            k0 = tile_start(kt)
            s_ref[pl.ds(k0, TK), :] = jnp.where(s_ref[pl.ds(k0, TK), :] > theta, 0.0, NEG_BIAS)
            return c

        lax.fori_loop(0, nkt, body, 0)

    @pl.when(any_tie > 0)
    def _():
        ri = lax.broadcasted_iota(jnp.int32, (TK, TK), 0)
        ci = lax.broadcasted_iota(jnp.int32, (TK, TK), 1)
        lower = jnp.where(ci < ri, 1.0, 0.0).astype(BF16)

        def tie_rank(kt):
            is_tie = s_ref[pl.ds(tile_start(kt), TK), :] == tie_thr
            return jnp.dot(lower, jnp.where(is_tie, 1.0, 0.0).astype(BF16), preferred_element_type=F32)

        def write_bias(kt, rank, budget_left):
            k0 = tile_start(kt)
            blk = s_ref[pl.ds(k0, TK), :]
            is_tie = blk == tie_thr
            tie_bias = jnp.where(rank < budget_left, jnp.where(is_tie, 0.0, NEG_BIAS), NEG_BIAS)
            s_ref[pl.ds(k0, TK), :] = jnp.where(blk > theta, 0.0, tie_bias)
            return budget_left - (rank[TK - 1:TK, :] + jnp.where(is_tie[TK - 1:TK, :], 1.0, 0.0))

        def pair(i, budget_left):
            r0 = tie_rank(2 * i)
            r1 = tie_rank(2 * i + 1)
            return write_bias(2 * i + 1, r1, write_bias(2 * i, r0, budget_left))

        left = lax.fori_loop(0, nkt // 2, pair, tie_budget)

        @pl.when(nkt % 2 == 1)
        def _():
            write_bias(nkt - 1, tie_rank(nkt - 1), left)

    rows = lax.broadcasted_iota(jnp.int32, (LANES, TQ), 0)
    for h in range(ATTN_HEADS):
        pair = qT_ref[0, (h // 2) * LANES:(h // 2 + 1) * LANES, :]
        qz_ref[h] = jnp.where((rows // HEAD_DIM) == (h % 2), pair, jnp.zeros_like(pair))
    acc_ref[...] = jnp.zeros_like(acc_ref)
    l_ref[...] = jnp.zeros_like(l_ref)

    def scores(kt, slot, h, m_old):
        k0 = tile_start(kt)
        kp = k_ref[0, pl.ds(k0, TK), (h // 2) * LANES:(h // 2 + 1) * LANES]
        s = jnp.dot(kp, qz_ref[h], preferred_element_type=F32) + s_ref[pl.ds(k0, TK), :]
        sb_ref[slot, h] = s
        m_h = jnp.maximum(m_old, vreduce(s, jnp.maximum).max(axis=0, keepdims=True))
        m_ref[slot, h:h + 1, :] = m_h
        a_ref[slot, h:h + 1, :] = jnp.exp2(m_old - m_h)

    ones_rows = jnp.ones((BF16_ROWS, TK), BF16)

    def values(kt, slot, h):
        alpha = a_ref[slot, h:h + 1, :]
        p = jnp.exp2(sb_ref[slot, h] - m_ref[slot, h:h + 1, :]).astype(BF16)
        v_h = jnp.concatenate([vT_ref[0, kt, h * HEAD_DIM:(h + 1) * HEAD_DIM, :], ones_rows], axis=0)
        pv = jnp.dot(v_h, p, preferred_element_type=F32)
        l_ref[h:h + 1, :] = alpha * l_ref[h:h + 1, :] + pv[HEAD_DIM:HEAD_DIM + 1, :]
        acc_ref[h * HEAD_DIM:(h + 1) * HEAD_DIM, :] = (
            alpha * acc_ref[h * HEAD_DIM:(h + 1) * HEAD_DIM, :] + pv[0:HEAD_DIM, :])

    def step(kt, slot):
        for h in range(ATTN_HEADS):
            scores(kt + 1, 1 - slot, h, m_ref[slot, h:h + 1, :])
            values(kt, slot, h)

    for h in range(ATTN_HEADS):
        scores(0, 0, h, jnp.full((1, TQ), M_INIT, F32))

    def p3(i, c):
        step(2 * i, 0)
        step(2 * i + 1, 1)
        return c

    lax.fori_loop(0, (nkt - 1) // 2, p3, 0)

    @pl.when(nkt % 2 == 0)
    def _():
        step(nkt - 2, 0)
        for h in range(ATTN_HEADS):
            values(nkt - 1, 1, h)

    @pl.when(nkt % 2 == 1)
    def _():
        for h in range(ATTN_HEADS):
            values(nkt - 1, 0, h)

    l_all = l_ref[...]
    outs = []
    for h in range(ATTN_HEADS):
        outs.append(acc_ref[h * HEAD_DIM:(h + 1) * HEAD_DIM, :] / l_all[h:h + 1, :])
    o_ref[0] = jnp.concatenate(outs, axis=0).T.astype(BF16)


def _attn_call(qT, k, vT, iqT, ikw, ikwT, *, batch, seq):
    nq = seq // TQ
    nkt = seq // TK
    return pl.pallas_call(
        _attn_kernel,
        grid=(batch, nq),
        in_specs=[
            pl.BlockSpec((1, ATTN_WIDTH, TQ), lambda b, j: (b, 0, j)),
            pl.BlockSpec((1, seq, ATTN_WIDTH), lambda b, j: (b, 0, 0)),
            pl.BlockSpec((1, nkt, ATTN_WIDTH, TK), lambda b, j: (b, 0, 0, 0)),
            pl.BlockSpec((1, IDX_HEADS * LANES, TQ), lambda b, j: (b, 0, j)),
            pl.BlockSpec((1, seq, LANES), lambda b, j: (b, 0, 0)),
            pl.BlockSpec((1, LANES, TQ), lambda b, j: (b, 0, j)),
        ],
        out_specs=pl.BlockSpec((1, TQ, ATTN_WIDTH), lambda b, j: (b, j, 0)),
        out_shape=jax.ShapeDtypeStruct((batch, seq, ATTN_WIDTH), BF16),
        scratch_shapes=[
            pltpu.VMEM((seq, TQ), F32),
            pltpu.VMEM((2, ATTN_HEADS, TK, TQ), F32),
            pltpu.VMEM((LANES, IDX_HEADS * TQ), BF16),
            pltpu.VMEM((ATTN_HEADS, LANES, TQ), BF16),
            pltpu.VMEM((ATTN_WIDTH, TQ), F32),
            pltpu.VMEM((2, ATTN_HEADS, TQ), F32),
            pltpu.VMEM((2, ATTN_HEADS, TQ), F32),
            pltpu.VMEM((ATTN_HEADS, TQ), F32),
        ],
        compiler_params=pltpu.CompilerParams(
            dimension_semantics=("arbitrary", "arbitrary"), vmem_limit_bytes=VMEM_LIMIT),
        name="dsa_attn",
    )(qT, k, vT, iqT, ikw, ikwT)


def _post_kernel(x_ref, at_ref, uc_ref, sg_ref, wao_ref, wco_ref, wmix_ref, gm_ref, wup_ref, wdn_ref,
                 gf_ref, o_ref, *, final):
    ya = jnp.dot(at_ref[...], wao_ref[...], preferred_element_type=F32)
    yc = jnp.dot(uc_ref[...], wco_ref[...], preferred_element_type=F32)
    merged = (sg_ref[:, 0:D_MODEL].astype(F32) * ya
              + sg_ref[:, D_MODEL:2 * D_MODEL].astype(F32) * yc)
    x1 = x_ref[...] + jnp.dot(merged.astype(BF16), wmix_ref[...], preferred_element_type=F32)
    u = _rmsnorm(x1, gm_ref[...]).astype(BF16)
    x2 = x1
    for c in range(MLP_HIDDEN // MLP_CHUNK):
        hid = jnp.dot(u, wup_ref[:, c * MLP_CHUNK:(c + 1) * MLP_CHUNK], preferred_element_type=F32)
        hid = jnp.square(jnp.maximum(hid, 0.0)).astype(BF16)
        x2 = x2 + jnp.dot(hid, wdn_ref[c * MLP_CHUNK:(c + 1) * MLP_CHUNK, :], preferred_element_type=F32)
    if final:
        x2 = _rmsnorm(x2, gf_ref[...])
    o_ref[...] = x2


def _post_call(x2d, attn, uc, sg, wao, wco, wmix, gm, wup, wdn, gf, *, final):
    m = x2d.shape[0]
    tm = TM_POST
    const = lambda i: (0, 0)
    row = lambda i: (i, 0)
    resident = lambda shape: pl.BlockSpec(shape, const, pipeline_mode=pl.Buffered(1))
    return pl.pallas_call(
        functools.partial(_post_kernel, final=final),
        grid=(m // tm,),
        in_specs=[
            pl.BlockSpec((tm, D_MODEL), row),
            pl.BlockSpec((tm, ATTN_WIDTH), row),
            pl.BlockSpec((tm, CONV_WIDTH), row),
            pl.BlockSpec((tm, 2 * D_MODEL), row),
            resident((ATTN_WIDTH, D_MODEL)),
            resident((CONV_WIDTH, D_MODEL)),
            resident((D_MODEL, D_MODEL)),
            pl.BlockSpec((1, D_MODEL), const),
            resident((D_MODEL, MLP_HIDDEN)),
            resident((MLP_HIDDEN, D_MODEL)),
            pl.BlockSpec((1, D_MODEL), const),
        ],
        out_specs=pl.BlockSpec((tm, D_MODEL), row),
        out_shape=jax.ShapeDtypeStruct((m, D_MODEL), F32),
        compiler_params=pltpu.CompilerParams(
            dimension_semantics=("arbitrary",), vmem_limit_bytes=VMEM_LIMIT),
        name="mixer_tail",
    )(x2d, attn, uc, sg, wao, wco, wmix, gm, wup, wdn, gf)


def _rope_tables(seq):
    half = ROPE_DIM // 2
    inv = 1.0 / (ROPE_THETA ** (jnp.arange(0, ROPE_DIM, 2, dtype=F32) / ROPE_DIM))
    ang = jnp.arange(seq, dtype=F32)[:, None] * inv[None, :]
    cos, sin = jnp.cos(ang), jnp.sin(ang)
    ones = jnp.ones((seq, HEAD_DIM - ROPE_DIM), F32)
    zeros = jnp.zeros((seq, HEAD_DIM - ROPE_DIM), F32)
    zh = jnp.zeros((seq, half), F32)
    cos_h = jnp.concatenate([cos, cos, ones], axis=1)
    sa_h = jnp.concatenate([-sin, zh, zeros], axis=1)
    sb_h = jnp.concatenate([zh, sin, zeros], axis=1)
    two = lambda t: jnp.concatenate([t, t], axis=1)
    return two(cos_h), two(sa_h), two(sb_h)


def kernel(x, norm_mix, w_in, conv_w, w_attn_out, w_conv_out, w_mix_out, norm_mlp, w_mlp_up, w_mlp_down,
           norm_final):
    batch, seq, d = x.shape
    depth = w_in.shape[0]
    assert d == D_MODEL and seq % TM_PROJ == 0 and TM_PROJ % TK == 0 and TQ == TK
    assert seq // 4 >= TOPK_MAX
    cosv, sa, sb = _rope_tables(seq)
    h = x.reshape(batch * seq, d)
    for l in range(depth):
        qT, k, vT, iqT, ikw, ikwT, uc, sg = _proj_call(
            h, norm_mix[l][None, :], _wprep_call(w_in[l].T.astype(BF16)), cosv, sa, sb, conv_w[l],
            batch=batch, seq=seq)
        attn = _attn_call(qT, k, vT, iqT, ikw, ikwT, batch=batch, seq=seq)
        h = _post_call(
            h, attn.reshape(batch * seq, ATTN_WIDTH), uc, sg,
            w_attn_out[l].astype(BF16), w_conv_out[l].astype(BF16), w_mix_out[l].astype(BF16),
            norm_mlp[l][None, :], w_mlp_up[l].astype(BF16), w_mlp_down[l].astype(BF16),
            norm_final[None, :], final=(l == depth - 1))
    return h.reshape(batch, seq, d)
```

```python
import functools

import jax
import jax.numpy as jnp
import numpy as np
from jax import lax
from jax.experimental import pallas as pl
from jax.experimental.pallas import tpu as pltpu

F32 = jnp.float32
BF16 = jnp.bfloat16

D_MODEL = 1024
CHUNK = 64
EPS = 1e-6
HEAD_DIM = 64
ATTN_WIDTH = 512
ATTN_HEADS = 8
ROPE_DIM = 16
ROPE_THETA = 500000.0
IDX_HEADS = 4
IDX_DIM = 64
IDX_SCALE = (IDX_DIM ** -0.5) * (IDX_HEADS ** -0.5)
TOPK_MAX = 256
CONV_WIDTH = 512
CONV_K = 3
MLP_HIDDEN = 4 * D_MODEL

LANES = 128
SUBLANES = 8
BF16_ROWS = 16
QK_SCALE = (HEAD_DIM ** -0.5) * float(np.log2(np.e))

WBLK = 512
OFF_Q = 0
OFF_K = OFF_Q + ATTN_WIDTH
OFF_V = OFF_K + ATTN_WIDTH
OFF_IQ = OFF_V + ATTN_WIDTH
OFF_IKW = OFF_IQ + IDX_HEADS * LANES
IW_LANE = 96
OFF_CB = OFF_IKW + WBLK
OFF_CC = OFF_CB + CONV_WIDTH
OFF_CH = OFF_CC + CONV_WIDTH
OFF_G = OFF_CH + CONV_WIDTH
PROJ_COLS = OFF_G + 2 * D_MODEL
SRC_IQ = 3 * ATTN_WIDTH
SRC_IK = SRC_IQ + IDX_HEADS * IDX_DIM
SRC_IW = SRC_IK + IDX_DIM
SRC_CB = SRC_IW + IDX_HEADS

TM_PROJ = 1024
TQ = 256
TK = 256
TM_POST = 512
MLP_CHUNK = 1024
PASSES_UNCHECKED = 8
PASSES_PER_CHECK = 3
MAX_CHECKS = 192
GUESS_WIDEN = 1.3
STATUS_TIE = 1024.0
NEG_BIAS = -2e30
M_INIT = -1e30
VMEM_LIMIT = 48 * 1024 * 1024


def _rmsnorm(x, g):
    ms = jnp.mean(x * x, axis=-1, keepdims=True)
    return x * lax.rsqrt(ms + EPS) * g


def _wprep_kernel(a_ref, b_ref, o_ref):
    s = pl.program_id(0)
    n_plain = OFF_IQ // WBLK

    def zeros(n):
        return jnp.zeros((n, D_MODEL), BF16)

    @pl.when(s < n_plain)
    def _():
        o_ref[...] = a_ref[...]

    @pl.when(s == n_plain)
    def _():
        base = SRC_IQ % WBLK
        parts = []
        for h in range(IDX_HEADS):
            parts += [a_ref[base + h * IDX_DIM:base + (h + 1) * IDX_DIM, :], zeros(LANES - IDX_DIM)]
        o_ref[...] = jnp.concatenate(parts, axis=0)

    @pl.when(s == n_plain + 1)
    def _():
        ik0 = SRC_IK % WBLK
        iw0 = SRC_IW % WBLK
        rows = lax.broadcasted_iota(jnp.int32, (BF16_ROWS, D_MODEL), 0)
        iw = jnp.where(rows < IDX_HEADS, a_ref[iw0:iw0 + BF16_ROWS, :].astype(F32), 0.0).astype(BF16)
        o_ref[...] = jnp.concatenate(
            [a_ref[ik0:ik0 + IDX_DIM, :], zeros(IW_LANE - IDX_DIM), iw, zeros(WBLK - IW_LANE - BF16_ROWS)],
            axis=0)

    @pl.when(s > n_plain + 1)
    def _():
        shift = SRC_CB % WBLK
        for c in range(D_MODEL // (2 * LANES)):
            cols = slice(c * 2 * LANES, (c + 1) * 2 * LANES)
            both = jnp.concatenate([a_ref[:, cols].astype(F32), b_ref[:, cols].astype(F32)], axis=0)
            o_ref[:, cols] = pltpu.roll(both, 2 * WBLK - shift, 0)[0:WBLK, :].astype(BF16)


def _wprep_call(wt16):
    n_plain = OFF_IQ // WBLK
    blk0 = SRC_CB // WBLK
    assert SRC_IQ // WBLK == SRC_IK // WBLK == SRC_IW // WBLK == blk0 == n_plain
    assert SRC_IW % WBLK % BF16_ROWS == 0 and (SRC_IW % WBLK) + BF16_ROWS <= WBLK

    def blk_a(s):
        return jnp.where(s < n_plain, s, jnp.where(s < n_plain + 2, blk0, s - (n_plain + 2) + blk0))

    def blk_b(s):
        return jnp.where(s < n_plain + 2, 0, s - (n_plain + 2) + blk0 + 1)

    return pl.pallas_call(
        _wprep_kernel,
        grid=(PROJ_COLS // WBLK,),
        in_specs=[pl.BlockSpec((WBLK, D_MODEL), lambda s: (blk_a(s), 0)),
                  pl.BlockSpec((WBLK, D_MODEL), lambda s: (blk_b(s), 0))],
        out_specs=pl.BlockSpec((WBLK, D_MODEL), lambda s: (s, 0)),
        out_shape=jax.ShapeDtypeStruct((PROJ_COLS, D_MODEL), BF16),
        compiler_params=pltpu.CompilerParams(
            dimension_semantics=("arbitrary",), vmem_limit_bytes=VMEM_LIMIT),
        name="w_arrange",
    )(wt16, wt16)


def _proj_kernel(x_ref, g_ref, wt_ref, cos_ref, sa_ref, sb_ref, cw_ref,
                 qT_ref, k_ref, vT_ref, iqT_ref, ikw_ref, ikwT_ref, uc_ref, sg_ref,
                 zbuf, *, tiles_per_seq):
    i = pl.program_id(0)
    tm = x_ref.shape[0]
    u = _rmsnorm(x_ref[...], g_ref[...]).astype(BF16)
    cosv = cos_ref[...]
    sa = sa_ref[...]
    sb = sb_ref[...]

    def proj(c0, n):
        return lax.dot_general(u, wt_ref[c0:c0 + n, :], (((1,), (1,)), ((), ())),
                               preferred_element_type=F32)

    def rope(a):
        outs = []
        for gidx in range(a.shape[1] // LANES):
            ag = a[:, gidx * LANES:(gidx + 1) * LANES]
            outs.append(ag * cosv
                        + pltpu.roll(ag, LANES - ROPE_DIM // 2, 1) * sa
                        + pltpu.roll(ag, ROPE_DIM // 2, 1) * sb)
        return outs[0] if len(outs) == 1 else jnp.concatenate(outs, axis=1)

    q = rope(proj(OFF_Q, ATTN_WIDTH)) * QK_SCALE
    qT_ref[0] = q.T.astype(BF16)
    k_ref[0] = rope(proj(OFF_K, ATTN_WIDTH)).astype(BF16)
    v_t = proj(OFF_V, ATTN_WIDTH).T.astype(BF16)
    for t in range(tm // TK):
        vT_ref[0, t] = v_t[:, t * TK:(t + 1) * TK]
    iqT_ref[0] = rope(proj(OFF_IQ, IDX_HEADS * LANES)).T.astype(BF16)
    ikw = rope(proj(OFF_IKW, LANES))
    ikw_ref[0] = ikw.astype(BF16)
    ikwT_ref[0] = ikw.T

    z = proj(OFF_CC, CONV_WIDTH) * proj(OFF_CH, CONV_WIDTH)

    @pl.when(i % tiles_per_seq == 0)
    def _():
        zbuf[0:SUBLANES, :] = jnp.zeros((SUBLANES, CONV_WIDTH), F32)

    @pl.when(i % tiles_per_seq != 0)
    def _():
        zbuf[0:SUBLANES, :] = zbuf[tm:tm + SUBLANES, :]

    zbuf[SUBLANES:SUBLANES + tm, :] = z
    z1 = zbuf[SUBLANES - 1:SUBLANES - 1 + tm, :]
    z2 = zbuf[SUBLANES - 2:SUBLANES - 2 + tm, :]
    conv = z2 * cw_ref[0:1, :] + z1 * cw_ref[1:2, :] + z * cw_ref[2:3, :]
    uc_ref[...] = (proj(OFF_CB, CONV_WIDTH) * conv).astype(BF16)

    for c in range(4):
        gc = proj(OFF_G + c * 512, 512)
        sg_ref[:, c * 512:(c + 1) * 512] = jax.nn.sigmoid(gc).astype(BF16)


def _proj_call(x2d, gain, wt, cosv, sa, sb, cw, *, batch, seq):
    m = x2d.shape[0]
    tm = TM_PROJ
    nt = seq // tm
    kt_per_step = tm // TK
    const = lambda i: (0, 0)
    out_shape = (
        jax.ShapeDtypeStruct((batch, ATTN_WIDTH, seq), BF16),
        jax.ShapeDtypeStruct((batch, seq, ATTN_WIDTH), BF16),
        jax.ShapeDtypeStruct((batch, seq // TK, ATTN_WIDTH, TK), BF16),
        jax.ShapeDtypeStruct((batch, IDX_HEADS * LANES, seq), BF16),
        jax.ShapeDtypeStruct((batch, seq, LANES), BF16),
        jax.ShapeDtypeStruct((batch, LANES, seq), F32),
        jax.ShapeDtypeStruct((m, CONV_WIDTH), BF16),
        jax.ShapeDtypeStruct((m, 2 * D_MODEL), BF16),
    )
    in_specs = [
        pl.BlockSpec((tm, D_MODEL), lambda i: (i, 0)),
        pl.BlockSpec((1, D_MODEL), const),
        pl.BlockSpec((PROJ_COLS, D_MODEL), const, pipeline_mode=pl.Buffered(1)),
        pl.BlockSpec((tm, LANES), lambda i: (i % nt, 0)),
        pl.BlockSpec((tm, LANES), lambda i: (i % nt, 0)),
        pl.BlockSpec((tm, LANES), lambda i: (i % nt, 0)),
        pl.BlockSpec((CONV_K, CONV_WIDTH), const),
    ]
    out_specs = (
        pl.BlockSpec((1, ATTN_WIDTH, tm), lambda i: (i // nt, 0, i % nt)),
        pl.BlockSpec((1, tm, ATTN_WIDTH), lambda i: (i // nt, i % nt, 0)),
        pl.BlockSpec((1, kt_per_step, ATTN_WIDTH, TK), lambda i: (i // nt, i % nt, 0, 0)),
        pl.BlockSpec((1, IDX_HEADS * LANES, tm), lambda i: (i // nt, 0, i % nt)),
        pl.BlockSpec((1, tm, LANES), lambda i: (i // nt, i % nt, 0)),
        pl.BlockSpec((1, LANES, tm), lambda i: (i // nt, 0, i % nt)),
        pl.BlockSpec((tm, CONV_WIDTH), lambda i: (i, 0)),
        pl.BlockSpec((tm, 2 * D_MODEL), lambda i: (i, 0)),
    )
    return pl.pallas_call(
        functools.partial(_proj_kernel, tiles_per_seq=nt),
        grid=(m // tm,),
        in_specs=in_specs,
        out_specs=out_specs,
        out_shape=out_shape,
        scratch_shapes=[pltpu.VMEM((tm + SUBLANES, CONV_WIDTH), F32)],
        compiler_params=pltpu.CompilerParams(
            dimension_semantics=("arbitrary",), vmem_limit_bytes=VMEM_LIMIT),
        name="proj",
    )(x2d, gain, wt, cosv, sa, sb, cw)


def _attn_kernel(qT_ref, k_ref, vT_ref, iqT_ref, ikw_ref, ikwT_ref, o_ref,
                 s_ref, sb_ref, iq_ref, qz_ref, acc_ref, m_ref, a_ref, l_ref):
    j = pl.program_id(1)
    nkt = j + 1
    grp = TK // SUBLANES

    def tile_start(kt):
        return pl.multiple_of(kt * TK, TK)

    def vreduce(x, op):
        parts = [x[g * SUBLANES:(g + 1) * SUBLANES, :] for g in range(grp)]
        while len(parts) > 1:
            parts = [op(parts[i], parts[i + 1]) for i in range(0, len(parts), 2)]
        return parts[0]

    w_t = ikwT_ref[0, IW_LANE:IW_LANE + SUBLANES, :] * IDX_SCALE
    tpos = j * TQ + lax.broadcasted_iota(jnp.int32, (1, TQ), 1)
    limit = (tpos // CHUNK + 1) * CHUNK
    nadm = limit.astype(F32)
    kvec = jnp.minimum(limit, TOPK_MAX).astype(F32)

    iq_ref[...] = jnp.concatenate(
        [iqT_ref[0, h * LANES:(h + 1) * LANES, :] for h in range(IDX_HEADS)], axis=1)

    def idx_logits(kt):
        return jnp.dot(ikw_ref[0, pl.ds(tile_start(kt), TK), :], iq_ref[...],
                       preferred_element_type=F32)

    def idx_scores(kt, lg, stats):
        amax, cgt, cge, ssum, sabs = stats
        sc = jnp.maximum(lg[:, 0:TQ], 0.0) * w_t[0:1, :]
        for h in range(1, IDX_HEADS):
            sc = sc + jnp.maximum(lg[:, h * TQ:(h + 1) * TQ], 0.0) * w_t[h:h + 1, :]
        k0 = tile_start(kt)
        adm = (k0 + lax.broadcasted_iota(jnp.int32, (TK, 1), 0)) < limit
        sm = jnp.where(adm, sc, -jnp.inf)
        s_ref[pl.ds(k0, TK), :] = sm
        sz = jnp.where(adm, sc, 0.0)
        az = jnp.abs(sz)
        return (jnp.maximum(amax, vreduce(az, jnp.maximum)),
                cgt + vreduce(jnp.where(sm > 0.0, 1.0, 0.0), jnp.add),
                cge + vreduce(jnp.where(sm >= 0.0, 1.0, 0.0), jnp.add),
                ssum + vreduce(sz, jnp.add),
                sabs + vreduce(az, jnp.add))

    def p1_pair(i, stats):
        lg0 = idx_logits(2 * i)
        lg1 = idx_logits(2 * i + 1)
        return idx_scores(2 * i + 1, lg1, idx_scores(2 * i, lg0, stats))

    z8 = jnp.zeros((SUBLANES, TQ), F32)
    stats = lax.fori_loop(0, nkt // 2, p1_pair, (z8, z8, z8, z8, z8))
    stats = lax.cond(nkt % 2 == 1,
                     lambda st: idx_scores(nkt - 1, idx_logits(nkt - 1), st),
                     lambda st: st, stats)
    maxabs = stats[0].max(axis=0, keepdims=True)
    cg0, ce0, ssum, sabs = [x.sum(axis=0, keepdims=True) for x in stats[1:]]

    def count_gt(bound):
        b8 = jnp.broadcast_to(bound, (SUBLANES, TQ))

        def body(kt, accs):
            blk = s_ref[pl.ds(tile_start(kt), TK), :]
            accs = list(accs)
            for g in range(grp):
                part = blk[g * SUBLANES:(g + 1) * SUBLANES, :]
                accs[g % 4] = jnp.where(part > b8, accs[g % 4] + 1.0, accs[g % 4])
            return tuple(accs)

        z8 = jnp.zeros((SUBLANES, TQ), F32)
        accs = lax.fori_loop(0, nkt, body, (z8, z8, z8, z8))
        a = (accs[0] + accs[1]) + (accs[2] + accs[3])
        return a.sum(axis=0, keepdims=True)

    one = jnp.ones((1, TQ), F32)
    zero = jnp.zeros((1, TQ), F32)
    tgt = kvec - 0.5
    allsel = nadm <= kvec
    tie0 = jnp.logical_and(cg0 < kvec, ce0 >= kvec)
    pos = cg0 > kvec
    done0 = jnp.logical_or(allsel, jnp.logical_or(tie0, cg0 == kvec))
    theta0 = jnp.where(allsel, -jnp.inf, 0.0)
    tie_thr0 = jnp.where(jnp.logical_and(tie0, jnp.logical_not(allsel)), 0.0, jnp.inf)
    init = (
        jnp.int32(0), jnp.int32(1),
        jnp.where(pos, 0.0, -2.0 * maxabs), jnp.where(pos, maxabs, 0.0),
        jnp.where(pos, cg0, nadm) - tgt, jnp.where(pos, 0.0, ce0) - tgt,
        jnp.where(pos, 0.0, cg0), zero,
        jnp.where(done0, one, zero), theta0, tie_thr0, cg0,
    )

    def one_pass(st, hint=None):
        lo, hi, flo, fhi, chi, side, done, theta, tie_thr, tie_cgt = st
        interp = lo + (hi - lo) * (flo / (flo - fhi))
        bis = 0.5 * lo + 0.5 * hi
        mid = jnp.where(jnp.logical_and(interp > lo, interp < hi), interp, bis)
        if hint is not None:
            mid = jnp.where(jnp.logical_and(hint > lo, hint < hi), hint, mid)
        inside = jnp.logical_and(mid > lo, mid < hi)
        c = count_gt(mid)
        active = done < 0.5
        live = jnp.logical_and(active, inside)
        hit = jnp.logical_and(live, c == kvec)
        stuck = jnp.logical_and(active, jnp.logical_not(inside))
        up = jnp.logical_and(live, c > kvec)
        dn = jnp.logical_and(live, c < kvec)
        theta = jnp.where(hit, mid, jnp.where(stuck, hi, theta))
        tie_thr = jnp.where(stuck, hi, tie_thr)
        tie_cgt = jnp.where(stuck, chi, tie_cgt)
        done = jnp.where(jnp.logical_or(hit, stuck), 1.0, done)
        fhi_n = jnp.where(dn, c - tgt, jnp.where(jnp.logical_and(up, side > 0.5), fhi * 0.5, fhi))
        flo_n = jnp.where(up, c - tgt, jnp.where(jnp.logical_and(dn, side < -0.5), flo * 0.5, flo))
        return (jnp.where(up, mid, lo), jnp.where(dn, mid, hi), flo_n, fhi_n,
                jnp.where(dn, c, chi), jnp.where(up, 1.0, jnp.where(dn, -1.0, side)),
                done, theta, tie_thr, tie_cgt)

    npos = jnp.maximum(cg0, 1.0)
    nneg = jnp.maximum(nadm - ce0, 1.0)
    mean_pos = 0.5 * (sabs + ssum) / npos
    mean_neg = 0.5 * (sabs - ssum) / nneg
    frac_neg = jnp.minimum(jnp.maximum((kvec - ce0) / nneg, 1e-6), 1.0 - 1e-6)
    guess = jnp.where(pos, mean_pos * jnp.log(npos / kvec), mean_neg * jnp.log(1.0 - frac_neg))
    st = one_pass(init[2:], hint=guess)
    widen = jnp.where(pos, GUESS_WIDEN, 1.0 / GUESS_WIDEN)
    st = one_pass(st, hint=jnp.where(st[5] > 0.5, st[0] * widen, st[1] / widen))
    st = lax.fori_loop(0, PASSES_UNCHECKED - 2, lambda _, s: one_pass(s), st)

    def status_of(s):
        code = jnp.where(s[6] < 0.5, 1.0, 0.0) + jnp.where(s[8] < jnp.inf, STATUS_TIE, 0.0)
        return jnp.sum(code).astype(jnp.int32)

    def check_body(carry):
        s = carry[2:]
        for _ in range(PASSES_PER_CHECK):
            s = one_pass(s)
        return (carry[0] + 1, status_of(s)) + s

    def check_cond(carry):
        return jnp.logical_and(carry[1] % int(STATUS_TIE) > 0, carry[0] < MAX_CHECKS)

    final = lax.while_loop(check_cond, check_body, (jnp.int32(0), jnp.int32(1)) + st)
    theta, tie_thr, tie_cgt = final[9], final[10], final[11]
    tie_budget = kvec - tie_cgt
    any_tie = final[1] // int(STATUS_TIE)

    @pl.when(any_tie == 0)
    def _():
        def body(kt, c):
            k0 = tile_start(kt)
            s_ref[pl.ds(k0, TK), :] = jnp.where(s_ref[pl.ds(k0, TK), :] > theta, 0.0, NEG_BIAS)
            return c

        lax.fori_loop(0, nkt, body, 0)

    @pl.when(any_tie > 0)
    def _():
        ri = lax.broadcasted_iota(jnp.int32, (TK, TK), 0)
        ci = lax.broadcasted_iota(jnp.int32, (TK, TK), 1)
        lower = jnp.where(ci < ri, 1.0, 0.0).astype(BF16)

        def tie_rank(kt):
            is_tie = s_ref[pl.ds(tile_start(kt), TK), :] == tie_thr
            return jnp.dot(lower, jnp.where(is_tie, 1.0, 0.0).astype(BF16), preferred_element_type=F32)

        def write_bias(kt, rank, budget_left):
            k0 = tile_start(kt)
            blk = s_ref[pl.ds(k0, TK), :]
            is_tie = blk == tie_thr
            tie_bias = jnp.where(rank < budget_left, jnp.where(is_tie, 0.0, NEG_BIAS), NEG_BIAS)
            s_ref[pl.ds(k0, TK), :] = jnp.where(blk > theta, 0.0, tie_bias)
            return budget_left - (rank[TK - 1:TK, :] + jnp.where(is_tie[TK - 1:TK, :], 1.0, 0.0))

        def pair(i, budget_left):
            r0 = tie_rank(2 * i)
            r1 = tie_rank(2 * i + 1)
            return write_bias(2 * i + 1, r1, write_bias(2 * i, r0, budget_left))

        left = lax.fori_loop(0, nkt // 2, pair, tie_budget)

        @pl.when(nkt % 2 == 1)
        def _():
            write_bias(nkt - 1, tie_rank(nkt - 1), left)

    rows = lax.broadcasted_iota(jnp.int32, (LANES, TQ), 0)
    for h in range(ATTN_HEADS):
        pair = qT_ref[0, (h // 2) * LANES:(h // 2 + 1) * LANES, :]
        qz_ref[h] = jnp.where((rows // HEAD_DIM) == (h % 2), pair, jnp.zeros_like(pair))
    acc_ref[...] = jnp.zeros_like(acc_ref)
    l_ref[...] = jnp.zeros_like(l_ref)

    def scores(kt, slot, h, m_old):
        k0 = tile_start(kt)
        kp = k_ref[0, pl.ds(k0, TK), (h // 2) * LANES:(h // 2 + 1) * LANES]
        s = jnp.dot(kp, qz_ref[h], preferred_element_type=F32) + s_ref[pl.ds(k0, TK), :]
        sb_ref[slot, h] = s
        m_h = jnp.maximum(m_old, vreduce(s, jnp.maximum).max(axis=0, keepdims=True))
        m_ref[slot, h:h + 1, :] = m_h
        a_ref[slot, h:h + 1, :] = jnp.exp2(m_old - m_h)

    ones_rows = jnp.ones((BF16_ROWS, TK), BF16)

    def values(kt, slot, h):
        alpha = a_ref[slot, h:h + 1, :]
        p = jnp.exp2(sb_ref[slot, h] - m_ref[slot, h:h + 1, :]).astype(BF16)
        v_h = jnp.concatenate([vT_ref[0, kt, h * HEAD_DIM:(h + 1) * HEAD_DIM, :], ones_rows], axis=0)
        pv = jnp.dot(v_h, p, preferred_element_type=F32)
        l_ref[h:h + 1, :] = alpha * l_ref[h:h + 1, :] + pv[HEAD_DIM:HEAD_DIM + 1, :]
        acc_ref[h * HEAD_DIM:(h + 1) * HEAD_DIM, :] = (
            alpha * acc_ref[h * HEAD_DIM:(h + 1) * HEAD_DIM, :] + pv[0:HEAD_DIM, :])

    def step(kt, slot):
        for h in range(ATTN_HEADS):
            scores(kt + 1, 1 - slot, h, m_ref[slot, h:h + 1, :])
            values(kt, slot, h)

    for h in range(ATTN_HEADS):
        scores(0, 0, h, jnp.full((1, TQ), M_INIT, F32))

    def p3(i, c):
        step(2 * i, 0)
        step(2 * i + 1, 1)
        return c

    lax.fori_loop(0, (nkt - 1) // 2, p3, 0)

    @pl.when(nkt % 2 == 0)
    def _():
        step(nkt - 2, 0)
        for h in range(ATTN_HEADS):
            values(nkt - 1, 1, h)

    @pl.when(nkt % 2 == 1)
    def _():
        for h in range(ATTN_HEADS):
            values(nkt - 1, 0, h)

    l_all = l_ref[...]
    outs = []
    for h in range(ATTN_HEADS):
        outs.append(acc_ref[h * HEAD_DIM:(h + 1) * HEAD_DIM, :] / l_all[h:h + 1, :])
    o_ref[0] = jnp.concatenate(outs, axis=0).T.astype(BF16)


def _attn_call(qT, k, vT, iqT, ikw, ikwT, *, batch, seq):
    nq = seq // TQ
    nkt = seq // TK
    return pl.pallas_call(
        _attn_kernel,
        grid=(batch, nq),
        in_specs=[
            pl.BlockSpec((1, ATTN_WIDTH, TQ), lambda b, j: (b, 0, j)),
            pl.BlockSpec((1, seq, ATTN_WIDTH), lambda b, j: (b, 0, 0)),
            pl.BlockSpec((1, nkt, ATTN_WIDTH, TK), lambda b, j: (b, 0, 0, 0)),
            pl.BlockSpec((1, IDX_HEADS * LANES, TQ), lambda b, j: (b, 0, j)),
            pl.BlockSpec((1, seq, LANES), lambda b, j: (b, 0, 0)),
            pl.BlockSpec((1, LANES, TQ), lambda b, j: (b, 0, j)),
        ],
        out_specs=pl.BlockSpec((1, TQ, ATTN_WIDTH), lambda b, j: (b, j, 0)),
        out_shape=jax.ShapeDtypeStruct((batch, seq, ATTN_WIDTH), BF16),
        scratch_shapes=[
            pltpu.VMEM((seq, TQ), F32),
            pltpu.VMEM((2, ATTN_HEADS, TK, TQ), F32),
            pltpu.VMEM((LANES, IDX_HEADS * TQ), BF16),
            pltpu.VMEM((ATTN_HEADS, LANES, TQ), BF16),
            pltpu.VMEM((ATTN_WIDTH, TQ), F32),
            pltpu.VMEM((2, ATTN_HEADS, TQ), F32),
            pltpu.VMEM((2, ATTN_HEADS, TQ), F32),
            pltpu.VMEM((ATTN_HEADS, TQ), F32),
        ],
        compiler_params=pltpu.CompilerParams(
            dimension_semantics=("arbitrary", "arbitrary"), vmem_limit_bytes=VMEM_LIMIT),
        name="dsa_attn",
    )(qT, k, vT, iqT, ikw, ikwT)


def _post_kernel(x_ref, at_ref, uc_ref, sg_ref, wao_ref, wco_ref, wmix_ref, gm_ref, wup_ref, wdn_ref,
                 gf_ref, o_ref, *, final):
    ya = jnp.dot(at_ref[...], wao_ref[...], preferred_element_type=F32)
    yc = jnp.dot(uc_ref[...], wco_ref[...], preferred_element_type=F32)
    merged = (sg_ref[:, 0:D_MODEL].astype(F32) * ya
              + sg_ref[:, D_MODEL:2 * D_MODEL].astype(F32) * yc)
    x1 = x_ref[...] + jnp.dot(merged.astype(BF16), wmix_ref[...], preferred_element_type=F32)
    u = _rmsnorm(x1, gm_ref[...]).astype(BF16)
    x2 = x1
    for c in range(MLP_HIDDEN // MLP_CHUNK):
        hid = jnp.dot(u, wup_ref[:, c * MLP_CHUNK:(c + 1) * MLP_CHUNK], preferred_element_type=F32)
        hid = jnp.square(jnp.maximum(hid, 0.0)).astype(BF16)
        x2 = x2 + jnp.dot(hid, wdn_ref[c * MLP_CHUNK:(c + 1) * MLP_CHUNK, :], preferred_element_type=F32)
    if final:
        x2 = _rmsnorm(x2, gf_ref[...])
    o_ref[...] = x2


def _post_call(x2d, attn, uc, sg, wao, wco, wmix, gm, wup, wdn, gf, *, final):
    m = x2d.shape[0]
    tm = TM_POST
    const = lambda i: (0, 0)
    row = lambda i: (i, 0)
    resident = lambda shape: pl.BlockSpec(shape, const, pipeline_mode=pl.Buffered(1))
    return pl.pallas_call(
        functools.partial(_post_kernel, final=final),
        grid=(m // tm,),
        in_specs=[
            pl.BlockSpec((tm, D_MODEL), row),
            pl.BlockSpec((tm, ATTN_WIDTH), row),
            pl.BlockSpec((tm, CONV_WIDTH), row),
            pl.BlockSpec((tm, 2 * D_MODEL), row),
            resident((ATTN_WIDTH, D_MODEL)),
            resident((CONV_WIDTH, D_MODEL)),
            resident((D_MODEL, D_MODEL)),
            pl.BlockSpec((1, D_MODEL), const),
            resident((D_MODEL, MLP_HIDDEN)),
            resident((MLP_HIDDEN, D_MODEL)),
            pl.BlockSpec((1, D_MODEL), const),
        ],
        out_specs=pl.BlockSpec((tm, D_MODEL), row),
        out_shape=jax.ShapeDtypeStruct((m, D_MODEL), F32),
        compiler_params=pltpu.CompilerParams(
            dimension_semantics=("arbitrary",), vmem_limit_bytes=VMEM_LIMIT),
        name="mixer_tail",
    )(x2d, attn, uc, sg, wao, wco, wmix, gm, wup, wdn, gf)


def _rope_tables(seq):
    half = ROPE_DIM // 2
    inv = 1.0 / (ROPE_THETA ** (jnp.arange(0, ROPE_DIM, 2, dtype=F32) / ROPE_DIM))
    lane = np.arange(LANES)
    in_head = lane % HEAD_DIM
    ang = jnp.arange(seq, dtype=F32)[:, None] * inv[lane % half][None, :]
    cos, sin = jnp.cos(ang), jnp.sin(ang)
    first = jnp.asarray(in_head < half)[None, :]
    second = jnp.asarray((in_head >= half) & (in_head < ROPE_DIM))[None, :]
    return (jnp.where(first | second, cos, 1.0), jnp.where(first, -sin, 0.0), jnp.where(second, sin, 0.0))


def kernel(x, norm_mix, w_in, conv_w, w_attn_out, w_conv_out, w_mix_out, norm_mlp, w_mlp_up, w_mlp_down,
           norm_final):
    batch, seq, d = x.shape
    depth = w_in.shape[0]
    assert d == D_MODEL and seq % TM_PROJ == 0 and TM_PROJ % TK == 0 and TQ == TK
    assert seq // 4 >= TOPK_MAX
    cosv, sa, sb = _rope_tables(seq)
    h = x.reshape(batch * seq, d)
    for l in range(depth):
        qT, k, vT, iqT, ikw, ikwT, uc, sg = _proj_call(
            h, norm_mix[l][None, :], _wprep_call(w_in[l].T.astype(BF16)), cosv, sa, sb, conv_w[l],
            batch=batch, seq=seq)
        attn = _attn_call(qT, k, vT, iqT, ikw, ikwT, batch=batch, seq=seq)
        h = _post_call(
            h, attn.reshape(batch * seq, ATTN_WIDTH), uc, sg,
            w_attn_out[l].astype(BF16), w_conv_out[l].astype(BF16), w_mix_out[l].astype(BF16),
            norm_mlp[l][None, :], w_mlp_up[l].astype(BF16), w_mlp_down[l].astype(BF16),
            norm_final[None, :], final=(l == depth - 1))
    return h.reshape(batch, seq, d)
```

```python
import functools

import jax
import jax.numpy as jnp
import numpy as np
from jax import lax
from jax.experimental import pallas as pl
from jax.experimental.pallas import tpu as pltpu

F32 = jnp.float32
BF16 = jnp.bfloat16

D_MODEL = 1024
CHUNK = 64
EPS = 1e-6
HEAD_DIM = 64
ATTN_WIDTH = 512
ATTN_HEADS = 8
ROPE_DIM = 16
ROPE_THETA = 500000.0
IDX_HEADS = 4
IDX_DIM = 64
IDX_SCALE = (IDX_DIM ** -0.5) * (IDX_HEADS ** -0.5)
TOPK_MAX = 256
CONV_WIDTH = 512
CONV_K = 3
MLP_HIDDEN = 4 * D_MODEL

LANES = 128
SUBLANES = 8
BF16_ROWS = 16
QK_SCALE = (HEAD_DIM ** -0.5) * float(np.log2(np.e))

WBLK = 512
OFF_Q = 0
OFF_K = OFF_Q + ATTN_WIDTH
OFF_V = OFF_K + ATTN_WIDTH
IQ_WIDTH = IDX_HEADS * IDX_DIM
OFF_IQ = OFF_V + ATTN_WIDTH
OFF_IKW = OFF_IQ + IQ_WIDTH
IW_LANE = 96
OFF_CB = OFF_IQ + WBLK
OFF_CC = OFF_CB + CONV_WIDTH
OFF_CH = OFF_CC + CONV_WIDTH
OFF_G = OFF_CH + CONV_WIDTH
PROJ_COLS = OFF_G + 2 * D_MODEL
SRC_IQ = 3 * ATTN_WIDTH
SRC_IK = SRC_IQ + IDX_HEADS * IDX_DIM
SRC_IW = SRC_IK + IDX_DIM
SRC_CB = SRC_IW + IDX_HEADS

TM_PROJ = 1024
TQ = 256
TK = 256
TM_POST = 512
MLP_CHUNK = 1024
PASSES_UNCHECKED = 8
PASSES_PER_CHECK = 3
MAX_CHECKS = 192
GUESS_WIDEN = 1.3
STATUS_TIE = 1024.0
NEG_BIAS = -2e30
M_INIT = -1e30
VMEM_LIMIT = 48 * 1024 * 1024


def _rmsnorm(x, g):
    ms = jnp.mean(x * x, axis=-1, keepdims=True)
    return x * lax.rsqrt(ms + EPS) * g


def _wprep_kernel(a3_ref, b3_ref, o_ref):
    a_ref = a3_ref.at[0]
    b_ref = b3_ref.at[0]
    s = pl.program_id(0)
    n_plain = OFF_IQ // WBLK

    def zeros(n):
        return jnp.zeros((n, D_MODEL), BF16)

    @pl.when(s < n_plain)
    def _():
        o_ref[...] = a_ref[...]

    @pl.when(s == n_plain)
    def _():
        iq0 = SRC_IQ % WBLK
        ik0 = SRC_IK % WBLK
        iw0 = SRC_IW % WBLK
        rows = lax.broadcasted_iota(jnp.int32, (BF16_ROWS, D_MODEL), 0)
        iw = jnp.where(rows < IDX_HEADS, a_ref[iw0:iw0 + BF16_ROWS, :].astype(F32), 0.0).astype(BF16)
        o_ref[...] = jnp.concatenate(
            [a_ref[iq0:iq0 + IQ_WIDTH, :], a_ref[ik0:ik0 + IDX_DIM, :], zeros(IW_LANE - IDX_DIM), iw,
             zeros(WBLK - IQ_WIDTH - IW_LANE - BF16_ROWS)], axis=0)

    @pl.when(s > n_plain)
    def _():
        shift = SRC_CB % WBLK
        for c in range(D_MODEL // (2 * LANES)):
            cols = slice(c * 2 * LANES, (c + 1) * 2 * LANES)
            both = jnp.concatenate([a_ref[:, cols].astype(F32), b_ref[:, cols].astype(F32)], axis=0)
            o_ref[:, cols] = pltpu.roll(both, 2 * WBLK - shift, 0)[0:WBLK, :].astype(BF16)


def _wprep_call(wt16, layer):
    n_plain = OFF_IQ // WBLK
    blk0 = SRC_CB // WBLK
    assert SRC_IQ // WBLK == SRC_IK // WBLK == SRC_IW // WBLK == blk0 == n_plain
    assert SRC_IW % WBLK % BF16_ROWS == 0 and (SRC_IW % WBLK) + BF16_ROWS <= WBLK

    def blk_a(s):
        return jnp.where(s < n_plain, s, jnp.where(s == n_plain, blk0, s - (n_plain + 1) + blk0))

    def blk_b(s):
        return jnp.where(s <= n_plain, 0, s - (n_plain + 1) + blk0 + 1)

    return pl.pallas_call(
        _wprep_kernel,
        grid=(PROJ_COLS // WBLK,),
        in_specs=[pl.BlockSpec((1, WBLK, D_MODEL), lambda s: (layer, blk_a(s), 0)),
                  pl.BlockSpec((1, WBLK, D_MODEL), lambda s: (layer, blk_b(s), 0))],
        out_specs=pl.BlockSpec((WBLK, D_MODEL), lambda s: (s, 0)),
        out_shape=jax.ShapeDtypeStruct((PROJ_COLS, D_MODEL), BF16),
        compiler_params=pltpu.CompilerParams(
            dimension_semantics=("arbitrary",), vmem_limit_bytes=VMEM_LIMIT),
        name="w_arrange",
    )(wt16, wt16)


def _proj_kernel(x_ref, g_ref, wt_ref, cos_ref, sa_ref, sb_ref, cw_ref,
                 qT_ref, k_ref, vT_ref, iqT_ref, ikw_ref, ikwT_ref, uc_ref, sg_ref,
                 zbuf, *, tiles_per_seq):
    i = pl.program_id(0)
    tm = x_ref.shape[0]
    u = _rmsnorm(x_ref[...], g_ref[...]).astype(BF16)
    cosv = cos_ref[...]
    sa = sa_ref[...]
    sb = sb_ref[...]

    def proj(c0, n):
        return lax.dot_general(u, wt_ref[c0:c0 + n, :], (((1,), (1,)), ((), ())),
                               preferred_element_type=F32)

    def rope(a):
        outs = []
        for gidx in range(a.shape[1] // LANES):
            ag = a[:, gidx * LANES:(gidx + 1) * LANES]
            outs.append(ag * cosv
                        + pltpu.roll(ag, LANES - ROPE_DIM // 2, 1) * sa
                        + pltpu.roll(ag, ROPE_DIM // 2, 1) * sb)
        return outs[0] if len(outs) == 1 else jnp.concatenate(outs, axis=1)

    q = rope(proj(OFF_Q, ATTN_WIDTH)) * QK_SCALE
    qT_ref[0] = q.T.astype(BF16)
    k_ref[0] = rope(proj(OFF_K, ATTN_WIDTH)).astype(BF16)
    v_t = proj(OFF_V, ATTN_WIDTH).T.astype(BF16)
    for t in range(tm // TK):
        vT_ref[0, t] = v_t[:, t * TK:(t + 1) * TK]
    iqT_ref[0] = rope(proj(OFF_IQ, IQ_WIDTH)).T.astype(BF16)
    ikw = rope(proj(OFF_IKW, LANES))
    ikw_ref[0] = ikw.astype(BF16)
    ikwT_ref[0] = ikw.T

    z = proj(OFF_CC, CONV_WIDTH) * proj(OFF_CH, CONV_WIDTH)

    @pl.when(i % tiles_per_seq == 0)
    def _():
        zbuf[0:SUBLANES, :] = jnp.zeros((SUBLANES, CONV_WIDTH), F32)

    @pl.when(i % tiles_per_seq != 0)
    def _():
        zbuf[0:SUBLANES, :] = zbuf[tm:tm + SUBLANES, :]

    zbuf[SUBLANES:SUBLANES + tm, :] = z
    z1 = zbuf[SUBLANES - 1:SUBLANES - 1 + tm, :]
    z2 = zbuf[SUBLANES - 2:SUBLANES - 2 + tm, :]
    conv = z2 * cw_ref[0:1, :] + z1 * cw_ref[1:2, :] + z * cw_ref[2:3, :]
    uc_ref[...] = (proj(OFF_CB, CONV_WIDTH) * conv).astype(BF16)

    for c in range(4):
        gc = proj(OFF_G + c * 512, 512)
        sg_ref[:, c * 512:(c + 1) * 512] = jax.nn.sigmoid(gc).astype(BF16)


def _proj_call(x2d, gain, wt, cosv, sa, sb, cw, *, batch, seq):
    m = x2d.shape[0]
    tm = TM_PROJ
    nt = seq // tm
    kt_per_step = tm // TK
    const = lambda i: (0, 0)
    out_shape = (
        jax.ShapeDtypeStruct((batch, ATTN_WIDTH, seq), BF16),
        jax.ShapeDtypeStruct((batch, seq, ATTN_WIDTH), BF16),
        jax.ShapeDtypeStruct((batch, seq // TK, ATTN_WIDTH, TK), BF16),
        jax.ShapeDtypeStruct((batch, IQ_WIDTH, seq), BF16),
        jax.ShapeDtypeStruct((batch, seq, LANES), BF16),
        jax.ShapeDtypeStruct((batch, LANES, seq), F32),
        jax.ShapeDtypeStruct((m, CONV_WIDTH), BF16),
        jax.ShapeDtypeStruct((m, 2 * D_MODEL), BF16),
    )
    in_specs = [
        pl.BlockSpec((tm, D_MODEL), lambda i: (i, 0)),
        pl.BlockSpec((1, D_MODEL), const),
        pl.BlockSpec((PROJ_COLS, D_MODEL), const, pipeline_mode=pl.Buffered(1)),
        pl.BlockSpec((tm, LANES), lambda i: (i % nt, 0)),
        pl.BlockSpec((tm, LANES), lambda i: (i % nt, 0)),
        pl.BlockSpec((tm, LANES), lambda i: (i % nt, 0)),
        pl.BlockSpec((CONV_K, CONV_WIDTH), const),
    ]
    out_specs = (
        pl.BlockSpec((1, ATTN_WIDTH, tm), lambda i: (i // nt, 0, i % nt)),
        pl.BlockSpec((1, tm, ATTN_WIDTH), lambda i: (i // nt, i % nt, 0)),
        pl.BlockSpec((1, kt_per_step, ATTN_WIDTH, TK), lambda i: (i // nt, i % nt, 0, 0)),
        pl.BlockSpec((1, IQ_WIDTH, tm), lambda i: (i // nt, 0, i % nt)),
        pl.BlockSpec((1, tm, LANES), lambda i: (i // nt, i % nt, 0)),
        pl.BlockSpec((1, LANES, tm), lambda i: (i // nt, 0, i % nt)),
        pl.BlockSpec((tm, CONV_WIDTH), lambda i: (i, 0)),
        pl.BlockSpec((tm, 2 * D_MODEL), lambda i: (i, 0)),
    )
    return pl.pallas_call(
        functools.partial(_proj_kernel, tiles_per_seq=nt),
        grid=(m // tm,),
        in_specs=in_specs,
        out_specs=out_specs,
        out_shape=out_shape,
        scratch_shapes=[pltpu.VMEM((tm + SUBLANES, CONV_WIDTH), F32)],
        compiler_params=pltpu.CompilerParams(
            dimension_semantics=("arbitrary",), vmem_limit_bytes=VMEM_LIMIT),
        name="proj",
    )(x2d, gain, wt, cosv, sa, sb, cw)


def _attn_kernel(qT_ref, k_ref, vT_ref, iqT_ref, ikw_ref, ikwT_ref, o_ref,
                 s_ref, sb_ref, iq_ref, qz_ref, acc_ref, m_ref, a_ref, l_ref):
    j = pl.program_id(1)
    nkt = j + 1
    grp = TK // SUBLANES

    def tile_start(kt):
        return pl.multiple_of(kt * TK, TK)

    def vreduce(x, op):
        parts = [x[g * SUBLANES:(g + 1) * SUBLANES, :] for g in range(grp)]
        while len(parts) > 1:
            parts = [op(parts[i], parts[i + 1]) for i in range(0, len(parts), 2)]
        return parts[0]

    w_t = ikwT_ref[0, IW_LANE:IW_LANE + SUBLANES, :] * IDX_SCALE
    tpos = j * TQ + lax.broadcasted_iota(jnp.int32, (1, TQ), 1)
    limit = (tpos // CHUNK + 1) * CHUNK
    nadm = limit.astype(F32)
    kvec = jnp.minimum(limit, TOPK_MAX).astype(F32)

    zpad = jnp.zeros((LANES - IDX_DIM, TQ), BF16)
    iq_ref[...] = jnp.concatenate(
        [jnp.concatenate([iqT_ref[0, h * IDX_DIM:(h + 1) * IDX_DIM, :], zpad], axis=0)
         for h in range(IDX_HEADS)], axis=1)

    def idx_logits(kt):
        return jnp.dot(ikw_ref[0, pl.ds(tile_start(kt), TK), :], iq_ref[...],
                       preferred_element_type=F32)

    def idx_scores(kt, lg, stats):
        amax, cgt, cge, ssum, sabs = stats
        sc = jnp.maximum(lg[:, 0:TQ], 0.0) * w_t[0:1, :]
        for h in range(1, IDX_HEADS):
            sc = sc + jnp.maximum(lg[:, h * TQ:(h + 1) * TQ], 0.0) * w_t[h:h + 1, :]
        k0 = tile_start(kt)
        adm = (k0 + lax.broadcasted_iota(jnp.int32, (TK, 1), 0)) < limit
        sm = jnp.where(adm, sc, -jnp.inf)
        s_ref[pl.ds(k0, TK), :] = sm
        sz = jnp.where(adm, sc, 0.0)
        az = jnp.abs(sz)
        return (jnp.maximum(amax, vreduce(az, jnp.maximum)),
                cgt + vreduce(jnp.where(sm > 0.0, 1.0, 0.0), jnp.add),
                cge + vreduce(jnp.where(sm >= 0.0, 1.0, 0.0), jnp.add),
                ssum + vreduce(sz, jnp.add),
                sabs + vreduce(az, jnp.add))

    def p1_pair(i, stats):
        lg0 = idx_logits(2 * i)
        lg1 = idx_logits(2 * i + 1)
        return idx_scores(2 * i + 1, lg1, idx_scores(2 * i, lg0, stats))

    z8 = jnp.zeros((SUBLANES, TQ), F32)
    stats = lax.fori_loop(0, nkt // 2, p1_pair, (z8, z8, z8, z8, z8))
    stats = lax.cond(nkt % 2 == 1,
                     lambda st: idx_scores(nkt - 1, idx_logits(nkt - 1), st),
                     lambda st: st, stats)
    maxabs = stats[0].max(axis=0, keepdims=True)
    cg0, ce0, ssum, sabs = [x.sum(axis=0, keepdims=True) for x in stats[1:]]

    def count_gt(bound):
        b8 = jnp.broadcast_to(bound, (SUBLANES, TQ))

        def body(kt, accs):
            blk = s_ref[pl.ds(tile_start(kt), TK), :]
            accs = list(accs)
            for g in range(grp):
                part = blk[g * SUBLANES:(g + 1) * SUBLANES, :]
                accs[g % 4] = jnp.where(part > b8, accs[g % 4] + 1.0, accs[g % 4])
            return tuple(accs)

        z8 = jnp.zeros((SUBLANES, TQ), F32)
        accs = lax.fori_loop(0, nkt, body, (z8, z8, z8, z8))
        a = (accs[0] + accs[1]) + (accs[2] + accs[3])
        return a.sum(axis=0, keepdims=True)

    one = jnp.ones((1, TQ), F32)
    zero = jnp.zeros((1, TQ), F32)
    tgt = kvec - 0.5
    allsel = nadm <= kvec
    tie0 = jnp.logical_and(cg0 < kvec, ce0 >= kvec)
    pos = cg0 > kvec
    done0 = jnp.logical_or(allsel, jnp.logical_or(tie0, cg0 == kvec))
    theta0 = jnp.where(allsel, -jnp.inf, 0.0)
    tie_thr0 = jnp.where(jnp.logical_and(tie0, jnp.logical_not(allsel)), 0.0, jnp.inf)
    init = (
        jnp.int32(0), jnp.int32(1),
        jnp.where(pos, 0.0, -2.0 * maxabs), jnp.where(pos, maxabs, 0.0),
        jnp.where(pos, cg0, nadm) - tgt, jnp.where(pos, 0.0, ce0) - tgt,
        jnp.where(pos, 0.0, cg0), zero,
        jnp.where(done0, one, zero), theta0, tie_thr0, cg0,
    )

    def one_pass(st, hint=None):
        lo, hi, flo, fhi, chi, side, done, theta, tie_thr, tie_cgt = st
        interp = lo + (hi - lo) * (flo / (flo - fhi))
        bis = 0.5 * lo + 0.5 * hi
        mid = jnp.where(jnp.logical_and(interp > lo, interp < hi), interp, bis)
        if hint is not None:
            mid = jnp.where(jnp.logical_and(hint > lo, hint < hi), hint, mid)
        inside = jnp.logical_and(mid > lo, mid < hi)
        c = count_gt(mid)
        active = done < 0.5
        live = jnp.logical_and(active, inside)
        hit = jnp.logical_and(live, c == kvec)
        stuck = jnp.logical_and(active, jnp.logical_not(inside))
        up = jnp.logical_and(live, c > kvec)
        dn = jnp.logical_and(live, c < kvec)
        theta = jnp.where(hit, mid, jnp.where(stuck, hi, theta))
        tie_thr = jnp.where(stuck, hi, tie_thr)
        tie_cgt = jnp.where(stuck, chi, tie_cgt)
        done = jnp.where(jnp.logical_or(hit, stuck), 1.0, done)
        fhi_n = jnp.where(dn, c - tgt, jnp.where(jnp.logical_and(up, side > 0.5), fhi * 0.5, fhi))
        flo_n = jnp.where(up, c - tgt, jnp.where(jnp.logical_and(dn, side < -0.5), flo * 0.5, flo))
        return (jnp.where(up, mid, lo), jnp.where(dn, mid, hi), flo_n, fhi_n,
                jnp.where(dn, c, chi), jnp.where(up, 1.0, jnp.where(dn, -1.0, side)),
                done, theta, tie_thr, tie_cgt)

    npos = jnp.maximum(cg0, 1.0)
    nneg = jnp.maximum(nadm - ce0, 1.0)
    mean_pos = 0.5 * (sabs + ssum) / npos
    mean_neg = 0.5 * (sabs - ssum) / nneg
    frac_neg = jnp.minimum(jnp.maximum((kvec - ce0) / nneg, 1e-6), 1.0 - 1e-6)
    guess = jnp.where(pos, mean_pos * jnp.log(npos / kvec), mean_neg * jnp.log(1.0 - frac_neg))
    st = one_pass(init[2:], hint=guess)
    widen = jnp.where(pos, GUESS_WIDEN, 1.0 / GUESS_WIDEN)
    st = one_pass(st, hint=jnp.where(st[5] > 0.5, st[0] * widen, st[1] / widen))
    st = lax.fori_loop(0, PASSES_UNCHECKED - 2, lambda _, s: one_pass(s), st)

    def status_of(s):
        code = jnp.where(s[6] < 0.5, 1.0, 0.0) + jnp.where(s[8] < jnp.inf, STATUS_TIE, 0.0)
        return jnp.sum(code).astype(jnp.int32)

    def check_body(carry):
        s = carry[2:]
        for _ in range(PASSES_PER_CHECK):
            s = one_pass(s)
        return (carry[0] + 1, status_of(s)) + s

    def check_cond(carry):
        return jnp.logical_and(carry[1] % int(STATUS_TIE) > 0, carry[0] < MAX_CHECKS)

    final = lax.while_loop(check_cond, check_body, (jnp.int32(0), jnp.int32(1)) + st)
    theta, tie_thr, tie_cgt = final[9], final[10], final[11]
    tie_budget = kvec - tie_cgt
    any_tie = final[1] // int(STATUS_TIE)

    @pl.when(any_tie == 0)
    def _():
        def body(kt, c):
            k0 = tile_start(kt)
            s_ref[pl.ds(k0, TK), :] = jnp.where(s_ref[pl.ds(k0, TK), :] > theta, 0.0, NEG_BIAS)
            return c

        lax.fori_loop(0, nkt, body, 0)

    @pl.when(any_tie > 0)
    def _():
        ri = lax.broadcasted_iota(jnp.int32, (TK, TK), 0)
        ci = lax.broadcasted_iota(jnp.int32, (TK, TK), 1)
        lower = jnp.where(ci < ri, 1.0, 0.0).astype(BF16)

        def tie_rank(kt):
            is_tie = s_ref[pl.ds(tile_start(kt), TK), :] == tie_thr
            return jnp.dot(lower, jnp.where(is_tie, 1.0, 0.0).astype(BF16), preferred_element_type=F32)

        def write_bias(kt, rank, budget_left):
            k0 = tile_start(kt)
            blk = s_ref[pl.ds(k0, TK), :]
            is_tie = blk == tie_thr
            tie_bias = jnp.where(rank < budget_left, jnp.where(is_tie, 0.0, NEG_BIAS), NEG_BIAS)
            s_ref[pl.ds(k0, TK), :] = jnp.where(blk > theta, 0.0, tie_bias)
            return budget_left - (rank[TK - 1:TK, :] + jnp.where(is_tie[TK - 1:TK, :], 1.0, 0.0))

        def pair(i, budget_left):
            r0 = tie_rank(2 * i)
            r1 = tie_rank(2 * i + 1)
            return write_bias(2 * i + 1, r1, write_bias(2 * i, r0, budget_left))

        left = lax.fori_loop(0, nkt // 2, pair, tie_budget)

        @pl.when(nkt % 2 == 1)
        def _():
            write_bias(nkt - 1, tie_rank(nkt - 1), left)

    rows = lax.broadcasted_iota(jnp.int32, (LANES, TQ), 0)
    for h in range(ATTN_HEADS):
        pair = qT_ref[0, (h // 2) * LANES:(h // 2 + 1) * LANES, :]
        qz_ref[h] = jnp.where((rows // HEAD_DIM) == (h % 2), pair, jnp.zeros_like(pair))
    acc_ref[...] = jnp.zeros_like(acc_ref)
    l_ref[...] = jnp.zeros_like(l_ref)

    def scores(kt, slot, h, m_old):
        k0 = tile_start(kt)
        kp = k_ref[0, pl.ds(k0, TK), (h // 2) * LANES:(h // 2 + 1) * LANES]
        s = jnp.dot(kp, qz_ref[h], preferred_element_type=F32) + s_ref[pl.ds(k0, TK), :]
        sb_ref[slot, h] = s
        m_h = jnp.maximum(m_old, vreduce(s, jnp.maximum).max(axis=0, keepdims=True))
        m_ref[slot, h:h + 1, :] = m_h
        a_ref[slot, h:h + 1, :] = jnp.exp2(m_old - m_h)

    ones_rows = jnp.ones((BF16_ROWS, TK), BF16)

    def values(kt, slot, h):
        alpha = a_ref[slot, h:h + 1, :]
        p = jnp.exp2(sb_ref[slot, h] - m_ref[slot, h:h + 1, :]).astype(BF16)
        v_h = jnp.concatenate([vT_ref[0, kt, h * HEAD_DIM:(h + 1) * HEAD_DIM, :], ones_rows], axis=0)
        pv = jnp.dot(v_h, p, preferred_element_type=F32)
        l_ref[h:h + 1, :] = alpha * l_ref[h:h + 1, :] + pv[HEAD_DIM:HEAD_DIM + 1, :]
        acc_ref[h * HEAD_DIM:(h + 1) * HEAD_DIM, :] = (
            alpha * acc_ref[h * HEAD_DIM:(h + 1) * HEAD_DIM, :] + pv[0:HEAD_DIM, :])

    def step(kt, slot):
        for h in range(ATTN_HEADS):
            scores(kt + 1, 1 - slot, h, m_ref[slot, h:h + 1, :])
            values(kt, slot, h)

    for h in range(ATTN_HEADS):
        scores(0, 0, h, jnp.full((1, TQ), M_INIT, F32))

    def p3(i, c):
        step(2 * i, 0)
        step(2 * i + 1, 1)
        return c

    lax.fori_loop(0, (nkt - 1) // 2, p3, 0)

    @pl.when(nkt % 2 == 0)
    def _():
        step(nkt - 2, 0)
        for h in range(ATTN_HEADS):
            values(nkt - 1, 1, h)

    @pl.when(nkt % 2 == 1)
    def _():
        for h in range(ATTN_HEADS):
            values(nkt - 1, 0, h)

    l_all = l_ref[...]
    outs = []
    for h in range(ATTN_HEADS):
        outs.append(acc_ref[h * HEAD_DIM:(h + 1) * HEAD_DIM, :] / l_all[h:h + 1, :])
    o_ref[0] = jnp.concatenate(outs, axis=0).T.astype(BF16)


def _attn_call(qT, k, vT, iqT, ikw, ikwT, *, batch, seq):
    nq = seq // TQ
    nkt = seq // TK
    return pl.pallas_call(
        _attn_kernel,
        grid=(batch, nq),
        in_specs=[
            pl.BlockSpec((1, ATTN_WIDTH, TQ), lambda b, j: (b, 0, j)),
            pl.BlockSpec((1, seq, ATTN_WIDTH), lambda b, j: (b, 0, 0)),
            pl.BlockSpec((1, nkt, ATTN_WIDTH, TK), lambda b, j: (b, 0, 0, 0)),
            pl.BlockSpec((1, IQ_WIDTH, TQ), lambda b, j: (b, 0, j)),
            pl.BlockSpec((1, seq, LANES), lambda b, j: (b, 0, 0)),
            pl.BlockSpec((1, LANES, TQ), lambda b, j: (b, 0, j)),
        ],
        out_specs=pl.BlockSpec((1, TQ, ATTN_WIDTH), lambda b, j: (b, j, 0)),
        out_shape=jax.ShapeDtypeStruct((batch, seq, ATTN_WIDTH), BF16),
        scratch_shapes=[
            pltpu.VMEM((seq, TQ), F32),
            pltpu.VMEM((2, ATTN_HEADS, TK, TQ), F32),
            pltpu.VMEM((LANES, IDX_HEADS * TQ), BF16),
            pltpu.VMEM((ATTN_HEADS, LANES, TQ), BF16),
            pltpu.VMEM((ATTN_WIDTH, TQ), F32),
            pltpu.VMEM((2, ATTN_HEADS, TQ), F32),
            pltpu.VMEM((2, ATTN_HEADS, TQ), F32),
            pltpu.VMEM((ATTN_HEADS, TQ), F32),
        ],
        compiler_params=pltpu.CompilerParams(
            dimension_semantics=("arbitrary", "arbitrary"), vmem_limit_bytes=VMEM_LIMIT),
        name="dsa_attn",
    )(qT, k, vT, iqT, ikw, ikwT)


def _post_kernel(x_ref, at_ref, uc_ref, sg_ref, wao3_ref, wco3_ref, wmix3_ref, wup3_ref, wdn3_ref,
                 gm_ref, gf_ref, o_ref, *, final):
    wao_ref, wco_ref, wmix_ref, wup_ref, wdn_ref = [
        r.at[0] for r in (wao3_ref, wco3_ref, wmix3_ref, wup3_ref, wdn3_ref)]
    ya = jnp.dot(at_ref[...], wao_ref[...], preferred_element_type=F32)
    yc = jnp.dot(uc_ref[...], wco_ref[...], preferred_element_type=F32)
    merged = (sg_ref[:, 0:D_MODEL].astype(F32) * ya
              + sg_ref[:, D_MODEL:2 * D_MODEL].astype(F32) * yc)
    x1 = x_ref[...] + jnp.dot(merged.astype(BF16), wmix_ref[...], preferred_element_type=F32)
    u = _rmsnorm(x1, gm_ref[...]).astype(BF16)
    x2 = x1
    for c in range(MLP_HIDDEN // MLP_CHUNK):
        hid = jnp.dot(u, wup_ref[:, c * MLP_CHUNK:(c + 1) * MLP_CHUNK], preferred_element_type=F32)
        hid = jnp.square(jnp.maximum(hid, 0.0)).astype(BF16)
        x2 = x2 + jnp.dot(hid, wdn_ref[c * MLP_CHUNK:(c + 1) * MLP_CHUNK, :], preferred_element_type=F32)
    if final:
        x2 = _rmsnorm(x2, gf_ref[...])
    o_ref[...] = x2


def _post_call(x2d, attn, uc, sg, weights, gm, gf, *, layer, final):
    m = x2d.shape[0]
    tm = TM_POST
    const = lambda i: (0, 0)
    row = lambda i: (i, 0)
    resident = lambda shape: pl.BlockSpec((1,) + shape, lambda i: (layer, 0, 0), pipeline_mode=pl.Buffered(1))
    return pl.pallas_call(
        functools.partial(_post_kernel, final=final),
        grid=(m // tm,),
        in_specs=[
            pl.BlockSpec((tm, D_MODEL), row),
            pl.BlockSpec((tm, ATTN_WIDTH), row),
            pl.BlockSpec((tm, CONV_WIDTH), row),
            pl.BlockSpec((tm, 2 * D_MODEL), row),
            resident((ATTN_WIDTH, D_MODEL)),
            resident((CONV_WIDTH, D_MODEL)),
            resident((D_MODEL, D_MODEL)),
            resident((D_MODEL, MLP_HIDDEN)),
            resident((MLP_HIDDEN, D_MODEL)),
            pl.BlockSpec((1, D_MODEL), const),
            pl.BlockSpec((1, D_MODEL), const),
        ],
        out_specs=pl.BlockSpec((tm, D_MODEL), row),
        out_shape=jax.ShapeDtypeStruct((m, D_MODEL), F32),
        compiler_params=pltpu.CompilerParams(
            dimension_semantics=("arbitrary",), vmem_limit_bytes=VMEM_LIMIT),
        name="mixer_tail",
    )(x2d, attn, uc, sg, *weights, gm, gf)


def _rope_tables(seq):
    half = ROPE_DIM // 2
    inv = 1.0 / (ROPE_THETA ** (jnp.arange(0, ROPE_DIM, 2, dtype=F32) / ROPE_DIM))
    lane = np.arange(LANES)
    in_head = lane % HEAD_DIM
    ang = jnp.arange(seq, dtype=F32)[:, None] * inv[lane % half][None, :]
    cos, sin = jnp.cos(ang), jnp.sin(ang)
    first = jnp.asarray(in_head < half)[None, :]
    second = jnp.asarray((in_head >= half) & (in_head < ROPE_DIM))[None, :]
    return (jnp.where(first | second, cos, 1.0), jnp.where(first, -sin, 0.0), jnp.where(second, sin, 0.0))


def kernel(x, norm_mix, w_in, conv_w, w_attn_out, w_conv_out, w_mix_out, norm_mlp, w_mlp_up, w_mlp_down,
           norm_final):
    batch, seq, d = x.shape
    depth = w_in.shape[0]
    assert d == D_MODEL and seq % TM_PROJ == 0 and TM_PROJ % TK == 0 and TQ == TK
    assert seq // 4 >= TOPK_MAX
    cosv, sa, sb = _rope_tables(seq)
    h = x.reshape(batch * seq, d)
    wt16 = jnp.swapaxes(w_in, 1, 2).astype(BF16)
    tail_w = [w.astype(BF16) for w in (w_attn_out, w_conv_out, w_mix_out, w_mlp_up, w_mlp_down)]
    for l in range(depth):
        qT, k, vT, iqT, ikw, ikwT, uc, sg = _proj_call(
            h, norm_mix[l][None, :], _wprep_call(wt16, l), cosv, sa, sb, conv_w[l],
            batch=batch, seq=seq)
        attn = _attn_call(qT, k, vT, iqT, ikw, ikwT, batch=batch, seq=seq)
        h = _post_call(
            h, attn.reshape(batch * seq, ATTN_WIDTH), uc, sg, tail_w,
            norm_mlp[l][None, :], norm_final[None, :], layer=l, final=(l == depth - 1))
    return h.reshape(batch, seq, d)
```

```python
import functools

import jax
import jax.numpy as jnp
import numpy as np
from jax import lax
from jax.experimental import pallas as pl
from jax.experimental.pallas import tpu as pltpu

F32 = jnp.float32
BF16 = jnp.bfloat16

D_MODEL = 1024
CHUNK = 64
EPS = 1e-6
HEAD_DIM = 64
ATTN_WIDTH = 512
ATTN_HEADS = 8
ROPE_DIM = 16
ROPE_THETA = 500000.0
IDX_HEADS = 4
IDX_DIM = 64
IDX_SCALE = (IDX_DIM ** -0.5) * (IDX_HEADS ** -0.5)
TOPK_MAX = 256
CONV_WIDTH = 512
CONV_K = 3
MLP_HIDDEN = 4 * D_MODEL

LANES = 128
SUBLANES = 8
BF16_ROWS = 16
QK_SCALE = (HEAD_DIM ** -0.5) * float(np.log2(np.e))

WBLK = 512
OFF_Q = 0
OFF_K = OFF_Q + ATTN_WIDTH
OFF_V = OFF_K + ATTN_WIDTH
IQ_WIDTH = IDX_HEADS * IDX_DIM
OFF_IQ = OFF_V + ATTN_WIDTH
OFF_IKW = OFF_IQ + IQ_WIDTH
IW_LANE = 96
OFF_CB = OFF_IQ + WBLK
OFF_CC = OFF_CB + CONV_WIDTH
OFF_CH = OFF_CC + CONV_WIDTH
OFF_G = OFF_CH + CONV_WIDTH
PROJ_COLS = OFF_G + 2 * D_MODEL
SRC_IQ = 3 * ATTN_WIDTH
SRC_IK = SRC_IQ + IDX_HEADS * IDX_DIM
SRC_IW = SRC_IK + IDX_DIM
SRC_CB = SRC_IW + IDX_HEADS

TM_PROJ = 1024
TQ = 256
TK = 256
TM_POST = 512
MLP_CHUNK = 1024
PASSES_UNCHECKED = 8
PASSES_PER_CHECK = 3
MAX_CHECKS = 192
GUESS_WIDEN = 1.3
NEG_BIAS = -2e30
M_INIT = -1e30
VMEM_LIMIT = 48 * 1024 * 1024


def _rmsnorm(x, g):
    ms = jnp.mean(x * x, axis=-1, keepdims=True)
    return x * lax.rsqrt(ms + EPS) * g


def _wprep_kernel(a3_ref, b3_ref, o_ref):
    a_ref = a3_ref.at[0]
    b_ref = b3_ref.at[0]
    s = pl.program_id(0)
    n_plain = OFF_IQ // WBLK

    def zeros(n):
        return jnp.zeros((n, D_MODEL), BF16)

    @pl.when(s < n_plain)
    def _():
        o_ref[...] = a_ref[...]

    @pl.when(s == n_plain)
    def _():
        iq0 = SRC_IQ % WBLK
        ik0 = SRC_IK % WBLK
        iw0 = SRC_IW % WBLK
        rows = lax.broadcasted_iota(jnp.int32, (BF16_ROWS, D_MODEL), 0)
        iw = jnp.where(rows < IDX_HEADS, a_ref[iw0:iw0 + BF16_ROWS, :].astype(F32), 0.0).astype(BF16)
        o_ref[...] = jnp.concatenate(
            [a_ref[iq0:iq0 + IQ_WIDTH, :], a_ref[ik0:ik0 + IDX_DIM, :], zeros(IW_LANE - IDX_DIM), iw,
             zeros(WBLK - IQ_WIDTH - IW_LANE - BF16_ROWS)], axis=0)

    @pl.when(s > n_plain)
    def _():
        shift = SRC_CB % WBLK
        for c in range(D_MODEL // (2 * LANES)):
            cols = slice(c * 2 * LANES, (c + 1) * 2 * LANES)
            both = jnp.concatenate([a_ref[:, cols].astype(F32), b_ref[:, cols].astype(F32)], axis=0)
            o_ref[:, cols] = pltpu.roll(both, 2 * WBLK - shift, 0)[0:WBLK, :].astype(BF16)


def _wprep_call(wt16, layer):
    n_plain = OFF_IQ // WBLK
    blk0 = SRC_CB // WBLK
    assert SRC_IQ // WBLK == SRC_IK // WBLK == SRC_IW // WBLK == blk0 == n_plain
    assert SRC_IW % WBLK % BF16_ROWS == 0 and (SRC_IW % WBLK) + BF16_ROWS <= WBLK

    def blk_a(s):
        return jnp.where(s < n_plain, s, jnp.where(s == n_plain, blk0, s - (n_plain + 1) + blk0))

    def blk_b(s):
        return jnp.where(s <= n_plain, 0, s - (n_plain + 1) + blk0 + 1)

    return pl.pallas_call(
        _wprep_kernel,
        grid=(PROJ_COLS // WBLK,),
        in_specs=[pl.BlockSpec((1, WBLK, D_MODEL), lambda s: (layer, blk_a(s), 0)),
                  pl.BlockSpec((1, WBLK, D_MODEL), lambda s: (layer, blk_b(s), 0))],
        out_specs=pl.BlockSpec((WBLK, D_MODEL), lambda s: (s, 0)),
        out_shape=jax.ShapeDtypeStruct((PROJ_COLS, D_MODEL), BF16),
        compiler_params=pltpu.CompilerParams(
            dimension_semantics=("arbitrary",), vmem_limit_bytes=VMEM_LIMIT),
        name="w_arrange",
    )(wt16, wt16)


def _proj_kernel(x_ref, g_ref, wt_ref, cos_ref, sa_ref, sb_ref, cw_ref,
                 qT_ref, k_ref, vT_ref, iqT_ref, ikw_ref, ikwT_ref, uc_ref, sg_ref,
                 zbuf, *, tiles_per_seq):
    i = pl.program_id(0)
    tm = x_ref.shape[0]
    u = _rmsnorm(x_ref[...], g_ref[...]).astype(BF16)
    cosv = cos_ref[...]
    sa = sa_ref[...]
    sb = sb_ref[...]

    def proj(c0, n):
        return lax.dot_general(u, wt_ref[c0:c0 + n, :], (((1,), (1,)), ((), ())),
                               preferred_element_type=F32)

    def rope(a):
        outs = []
        for gidx in range(a.shape[1] // LANES):
            ag = a[:, gidx * LANES:(gidx + 1) * LANES]
            outs.append(ag * cosv
                        + pltpu.roll(ag, LANES - ROPE_DIM // 2, 1) * sa
                        + pltpu.roll(ag, ROPE_DIM // 2, 1) * sb)
        return outs[0] if len(outs) == 1 else jnp.concatenate(outs, axis=1)

    q = rope(proj(OFF_Q, ATTN_WIDTH)) * QK_SCALE
    qT_ref[0] = q.T.astype(BF16)
    k_ref[0] = rope(proj(OFF_K, ATTN_WIDTH)).astype(BF16)
    v_t = proj(OFF_V, ATTN_WIDTH).T.astype(BF16)
    for t in range(tm // TK):
        vT_ref[0, t] = v_t[:, t * TK:(t + 1) * TK]
    iqT_ref[0] = rope(proj(OFF_IQ, IQ_WIDTH)).T.astype(BF16)
    ikw = rope(proj(OFF_IKW, LANES))
    ikw_ref[0] = ikw.astype(BF16)
    ikwT_ref[0] = ikw.T

    z = proj(OFF_CC, CONV_WIDTH) * proj(OFF_CH, CONV_WIDTH)

    @pl.when(i % tiles_per_seq == 0)
    def _():
        zbuf[0:SUBLANES, :] = jnp.zeros((SUBLANES, CONV_WIDTH), F32)

    @pl.when(i % tiles_per_seq != 0)
    def _():
        zbuf[0:SUBLANES, :] = zbuf[tm:tm + SUBLANES, :]

    zbuf[SUBLANES:SUBLANES + tm, :] = z
    z1 = zbuf[SUBLANES - 1:SUBLANES - 1 + tm, :]
    z2 = zbuf[SUBLANES - 2:SUBLANES - 2 + tm, :]
    conv = z2 * cw_ref[0:1, :] + z1 * cw_ref[1:2, :] + z * cw_ref[2:3, :]
    uc_ref[...] = (proj(OFF_CB, CONV_WIDTH) * conv).astype(BF16)

    for c in range(4):
        gc = proj(OFF_G + c * 512, 512)
        sg_ref[:, c * 512:(c + 1) * 512] = jax.nn.sigmoid(gc).astype(BF16)


def _proj_call(x2d, gain, wt, cosv, sa, sb, cw, *, batch, seq):
    m = x2d.shape[0]
    tm = TM_PROJ
    nt = seq // tm
    kt_per_step = tm // TK
    const = lambda i: (0, 0)
    out_shape = (
        jax.ShapeDtypeStruct((batch, ATTN_WIDTH, seq), BF16),
        jax.ShapeDtypeStruct((batch, seq, ATTN_WIDTH), BF16),
        jax.ShapeDtypeStruct((batch, seq // TK, ATTN_WIDTH, TK), BF16),
        jax.ShapeDtypeStruct((batch, IQ_WIDTH, seq), BF16),
        jax.ShapeDtypeStruct((batch, seq, LANES), BF16),
        jax.ShapeDtypeStruct((batch, LANES, seq), F32),
        jax.ShapeDtypeStruct((m, CONV_WIDTH), BF16),
        jax.ShapeDtypeStruct((m, 2 * D_MODEL), BF16),
    )
    in_specs = [
        pl.BlockSpec((tm, D_MODEL), lambda i: (i, 0)),
        pl.BlockSpec((1, D_MODEL), const),
        pl.BlockSpec((PROJ_COLS, D_MODEL), const, pipeline_mode=pl.Buffered(1)),
        pl.BlockSpec((tm, LANES), lambda i: (i % nt, 0)),
        pl.BlockSpec((tm, LANES), lambda i: (i % nt, 0)),
        pl.BlockSpec((tm, LANES), lambda i: (i % nt, 0)),
        pl.BlockSpec((CONV_K, CONV_WIDTH), const),
    ]
    out_specs = (
        pl.BlockSpec((1, ATTN_WIDTH, tm), lambda i: (i // nt, 0, i % nt)),
        pl.BlockSpec((1, tm, ATTN_WIDTH), lambda i: (i // nt, i % nt, 0)),
        pl.BlockSpec((1, kt_per_step, ATTN_WIDTH, TK), lambda i: (i // nt, i % nt, 0, 0)),
        pl.BlockSpec((1, IQ_WIDTH, tm), lambda i: (i // nt, 0, i % nt)),
        pl.BlockSpec((1, tm, LANES), lambda i: (i // nt, i % nt, 0)),
        pl.BlockSpec((1, LANES, tm), lambda i: (i // nt, 0, i % nt)),
        pl.BlockSpec((tm, CONV_WIDTH), lambda i: (i, 0)),
        pl.BlockSpec((tm, 2 * D_MODEL), lambda i: (i, 0)),
    )
    return pl.pallas_call(
        functools.partial(_proj_kernel, tiles_per_seq=nt),
        grid=(m // tm,),
        in_specs=in_specs,
        out_specs=out_specs,
        out_shape=out_shape,
        scratch_shapes=[pltpu.VMEM((tm + SUBLANES, CONV_WIDTH), F32)],
        compiler_params=pltpu.CompilerParams(
            dimension_semantics=("arbitrary",), vmem_limit_bytes=VMEM_LIMIT),
        name="proj",
    )(x2d, gain, wt, cosv, sa, sb, cw)


def _attn_kernel(qT_ref, k_ref, vT_ref, iqT_ref, ikw_ref, ikwT_ref, o_ref,
                 s_ref, bias_ref, sb_ref, iq_ref, qz_ref, acc_ref, m_ref, a_ref, l_ref):
    j = pl.program_id(1)
    nkt = j + 1
    grp = TK // SUBLANES

    def tile_start(kt):
        return pl.multiple_of(kt * TK, TK)

    def vreduce(x, op):
        parts = [x[g * SUBLANES:(g + 1) * SUBLANES, :] for g in range(grp)]
        while len(parts) > 1:
            parts = [op(parts[i], parts[i + 1]) for i in range(0, len(parts), 2)]
        return parts[0]

    w_t = ikwT_ref[0, IW_LANE:IW_LANE + SUBLANES, :] * IDX_SCALE
    tpos = j * TQ + lax.broadcasted_iota(jnp.int32, (1, TQ), 1)
    limit = (tpos // CHUNK + 1) * CHUNK
    nadm = limit.astype(F32)
    kvec = jnp.minimum(limit, TOPK_MAX).astype(F32)

    zpad = jnp.zeros((LANES - IDX_DIM, TQ), BF16)
    iq_ref[...] = jnp.concatenate(
        [jnp.concatenate([iqT_ref[0, h * IDX_DIM:(h + 1) * IDX_DIM, :], zpad], axis=0)
         for h in range(IDX_HEADS)], axis=1)

    def idx_logits(kt):
        return jnp.dot(ikw_ref[0, pl.ds(tile_start(kt), TK), :], iq_ref[...],
                       preferred_element_type=F32)

    def idx_scores(kt, lg, stats):
        amax, cgt, cge, ssum, sabs = stats
        sc = jnp.maximum(lg[:, 0:TQ], 0.0) * w_t[0:1, :]
        for h in range(1, IDX_HEADS):
            sc = sc + jnp.maximum(lg[:, h * TQ:(h + 1) * TQ], 0.0) * w_t[h:h + 1, :]
        k0 = tile_start(kt)
        adm = (k0 + lax.broadcasted_iota(jnp.int32, (TK, 1), 0)) < limit
        sm = jnp.where(adm, sc, -jnp.inf)
        s_ref[pl.ds(k0, TK), :] = sm
        sz = jnp.where(adm, sc, 0.0)
        az = jnp.abs(sz)
        return (jnp.maximum(amax, vreduce(az, jnp.maximum)),
                cgt + vreduce(jnp.where(sm > 0.0, 1.0, 0.0), jnp.add),
                cge + vreduce(jnp.where(sm >= 0.0, 1.0, 0.0), jnp.add),
                ssum + vreduce(sz, jnp.add),
                sabs + vreduce(az, jnp.add))

    def p1_pair(i, stats):
        lg0 = idx_logits(2 * i)
        lg1 = idx_logits(2 * i + 1)
        return idx_scores(2 * i + 1, lg1, idx_scores(2 * i, lg0, stats))

    z8 = jnp.zeros((SUBLANES, TQ), F32)
    stats = lax.fori_loop(0, nkt // 2, p1_pair, (z8, z8, z8, z8, z8))
    stats = lax.cond(nkt % 2 == 1,
                     lambda st: idx_scores(nkt - 1, idx_logits(nkt - 1), st),
                     lambda st: st, stats)
    maxabs = stats[0].max(axis=0, keepdims=True)
    cg0, ce0, ssum, sabs = [x.sum(axis=0, keepdims=True) for x in stats[1:]]

    def count_gt(bound):
        b8 = jnp.broadcast_to(bound, (SUBLANES, TQ))

        def body(kt, accs):
            blk = s_ref[pl.ds(tile_start(kt), TK), :]
            accs = list(accs)
            for g in range(grp):
                part = blk[g * SUBLANES:(g + 1) * SUBLANES, :]
                accs[g % 4] = jnp.where(part > b8, accs[g % 4] + 1.0, accs[g % 4])
            return tuple(accs)

        z8 = jnp.zeros((SUBLANES, TQ), F32)
        accs = lax.fori_loop(0, nkt, body, (z8, z8, z8, z8))
        a = (accs[0] + accs[1]) + (accs[2] + accs[3])
        return a.sum(axis=0, keepdims=True)

    one = jnp.ones((1, TQ), F32)
    zero = jnp.zeros((1, TQ), F32)
    tgt = kvec - 0.5
    allsel = nadm <= kvec
    tie0 = jnp.logical_and(cg0 < kvec, ce0 >= kvec)
    pos = cg0 > kvec
    done0 = jnp.logical_or(allsel, jnp.logical_or(tie0, cg0 == kvec))
    theta0 = jnp.where(allsel, -jnp.inf, 0.0)
    tie_thr0 = jnp.where(jnp.logical_and(tie0, jnp.logical_not(allsel)), 0.0, jnp.inf)
    init = (
        jnp.int32(0), jnp.int32(1),
        jnp.where(pos, 0.0, -2.0 * maxabs), jnp.where(pos, maxabs, 0.0),
        jnp.where(pos, cg0, nadm) - tgt, jnp.where(pos, 0.0, ce0) - tgt,
        jnp.where(pos, 0.0, cg0), zero,
        jnp.where(done0, one, zero), theta0, tie_thr0, cg0,
    )

    def one_pass(st, hint=None):
        lo, hi, flo, fhi, chi, side, done, theta, tie_thr, tie_cgt = st
        interp = lo + (hi - lo) * (flo / (flo - fhi))
        bis = 0.5 * lo + 0.5 * hi
        mid = jnp.where(jnp.logical_and(interp > lo, interp < hi), interp, bis)
        if hint is not None:
            mid = jnp.where(jnp.logical_and(hint > lo, hint < hi), hint, mid)
        inside = jnp.logical_and(mid > lo, mid < hi)
        c = count_gt(mid)
        active = done < 0.5
        live = jnp.logical_and(active, inside)
        hit = jnp.logical_and(live, c == kvec)
        stuck = jnp.logical_and(active, jnp.logical_not(inside))
        up = jnp.logical_and(live, c > kvec)
        dn = jnp.logical_and(live, c < kvec)
        theta = jnp.where(hit, mid, jnp.where(stuck, hi, theta))
        tie_thr = jnp.where(stuck, hi, tie_thr)
        tie_cgt = jnp.where(stuck, chi, tie_cgt)
        done = jnp.where(jnp.logical_or(hit, stuck), 1.0, done)
        fhi_n = jnp.where(dn, c - tgt, jnp.where(jnp.logical_and(up, side > 0.5), fhi * 0.5, fhi))
        flo_n = jnp.where(up, c - tgt, jnp.where(jnp.logical_and(dn, side < -0.5), flo * 0.5, flo))
        return (jnp.where(up, mid, lo), jnp.where(dn, mid, hi), flo_n, fhi_n,
                jnp.where(dn, c, chi), jnp.where(up, 1.0, jnp.where(dn, -1.0, side)),
                done, theta, tie_thr, tie_cgt)

    npos = jnp.maximum(cg0, 1.0)
    nneg = jnp.maximum(nadm - ce0, 1.0)
    mean_pos = 0.5 * (sabs + ssum) / npos
    mean_neg = 0.5 * (sabs - ssum) / nneg
    frac_neg = jnp.minimum(jnp.maximum((kvec - ce0) / nneg, 1e-6), 1.0 - 1e-6)
    guess = jnp.where(pos, mean_pos * jnp.log(npos / kvec), mean_neg * jnp.log(1.0 - frac_neg))
    st = one_pass(init[2:], hint=guess)
    widen = jnp.where(pos, GUESS_WIDEN, 1.0 / GUESS_WIDEN)
    st = one_pass(st, hint=jnp.where(st[5] > 0.5, st[0] * widen, st[1] / widen))
    st = lax.fori_loop(0, PASSES_UNCHECKED - 2, lambda _, s: one_pass(s), st)

    def pending_of(s):
        return jnp.sum(jnp.where(s[6] < 0.5, 1.0, 0.0)).astype(jnp.int32)

    def check_body(carry):
        s = carry[2:]
        for _ in range(PASSES_PER_CHECK):
            s = one_pass(s)
        return (carry[0] + 1, pending_of(s)) + s

    def check_cond(carry):
        return jnp.logical_and(carry[1] > 0, carry[0] < MAX_CHECKS)

    final = lax.while_loop(check_cond, check_body, (jnp.int32(0), jnp.int32(1)) + st)
    theta, tie_thr, tie_cgt = final[9], final[10], final[11]
    tie_budget = kvec - tie_cgt

    ri = lax.broadcasted_iota(jnp.int32, (TK, TK), 0)
    ci = lax.broadcasted_iota(jnp.int32, (TK, TK), 1)
    lower = jnp.where(ci < ri, 1.0, 0.0).astype(BF16)

    def mask_tile(kt, slot, budget_left):
        blk = s_ref[pl.ds(tile_start(kt), TK), :]
        is_tie = blk == tie_thr
        rank = jnp.dot(lower, jnp.where(is_tie, 1.0, 0.0).astype(BF16), preferred_element_type=F32)
        tie_bias = jnp.where(rank < budget_left, jnp.where(is_tie, 0.0, NEG_BIAS), NEG_BIAS)
        bias_ref[slot] = jnp.where(blk > theta, 0.0, tie_bias)
        return budget_left - (rank[TK - 1:TK, :] + jnp.where(is_tie[TK - 1:TK, :], 1.0, 0.0))

    rows = lax.broadcasted_iota(jnp.int32, (LANES, TQ), 0)
    for h in range(ATTN_HEADS):
        pair = qT_ref[0, (h // 2) * LANES:(h // 2 + 1) * LANES, :]
        qz_ref[h] = jnp.where((rows // HEAD_DIM) == (h % 2), pair, jnp.zeros_like(pair))
    acc_ref[...] = jnp.zeros_like(acc_ref)
    l_ref[...] = jnp.zeros_like(l_ref)

    def scores(kt, slot, h, m_old):
        k0 = tile_start(kt)
        kp = k_ref[0, pl.ds(k0, TK), (h // 2) * LANES:(h // 2 + 1) * LANES]
        s = jnp.dot(kp, qz_ref[h], preferred_element_type=F32) + bias_ref[slot]
        sb_ref[slot, h] = s
        m_h = jnp.maximum(m_old, vreduce(s, jnp.maximum).max(axis=0, keepdims=True))
        m_ref[slot, h:h + 1, :] = m_h
        a_ref[slot, h:h + 1, :] = jnp.exp2(m_old - m_h)

    ones_rows = jnp.ones((BF16_ROWS, TK), BF16)

    def values(kt, slot, h):
        alpha = a_ref[slot, h:h + 1, :]
        p = jnp.exp2(sb_ref[slot, h] - m_ref[slot, h:h + 1, :]).astype(BF16)
        v_h = jnp.concatenate([vT_ref[0, kt, h * HEAD_DIM:(h + 1) * HEAD_DIM, :], ones_rows], axis=0)
        pv = jnp.dot(v_h, p, preferred_element_type=F32)
        l_ref[h:h + 1, :] = alpha * l_ref[h:h + 1, :] + pv[HEAD_DIM:HEAD_DIM + 1, :]
        acc_ref[h * HEAD_DIM:(h + 1) * HEAD_DIM, :] = (
            alpha * acc_ref[h * HEAD_DIM:(h + 1) * HEAD_DIM, :] + pv[0:HEAD_DIM, :])

    def step(kt, slot, budget_left):
        budget_left = mask_tile(kt + 1, 1 - slot, budget_left)
        for h in range(ATTN_HEADS):
            scores(kt + 1, 1 - slot, h, m_ref[slot, h:h + 1, :])
            values(kt, slot, h)
        return budget_left

    left = mask_tile(0, 0, tie_budget)
    for h in range(ATTN_HEADS):
        scores(0, 0, h, jnp.full((1, TQ), M_INIT, F32))

    def p3(i, budget_left):
        return step(2 * i + 1, 1, step(2 * i, 0, budget_left))

    left = lax.fori_loop(0, (nkt - 1) // 2, p3, left)

    @pl.when(nkt % 2 == 0)
    def _():
        step(nkt - 2, 0, left)
        for h in range(ATTN_HEADS):
            values(nkt - 1, 1, h)

    @pl.when(nkt % 2 == 1)
    def _():
        for h in range(ATTN_HEADS):
            values(nkt - 1, 0, h)

    l_all = l_ref[...]
    outs = []
    for h in range(ATTN_HEADS):
        outs.append(acc_ref[h * HEAD_DIM:(h + 1) * HEAD_DIM, :] / l_all[h:h + 1, :])
    o_ref[0] = jnp.concatenate(outs, axis=0).T.astype(BF16)


def _attn_call(qT, k, vT, iqT, ikw, ikwT, *, batch, seq):
    nq = seq // TQ
    nkt = seq // TK
    return pl.pallas_call(
        _attn_kernel,
        grid=(batch, nq),
        in_specs=[
            pl.BlockSpec((1, ATTN_WIDTH, TQ), lambda b, j: (b, 0, j)),
            pl.BlockSpec((1, seq, ATTN_WIDTH), lambda b, j: (b, 0, 0)),
            pl.BlockSpec((1, nkt, ATTN_WIDTH, TK), lambda b, j: (b, 0, 0, 0)),
            pl.BlockSpec((1, IQ_WIDTH, TQ), lambda b, j: (b, 0, j)),
            pl.BlockSpec((1, seq, LANES), lambda b, j: (b, 0, 0)),
            pl.BlockSpec((1, LANES, TQ), lambda b, j: (b, 0, j)),
        ],
        out_specs=pl.BlockSpec((1, TQ, ATTN_WIDTH), lambda b, j: (b, j, 0)),
        out_shape=jax.ShapeDtypeStruct((batch, seq, ATTN_WIDTH), BF16),
        scratch_shapes=[
            pltpu.VMEM((seq, TQ), F32),
            pltpu.VMEM((2, TK, TQ), F32),
            pltpu.VMEM((2, ATTN_HEADS, TK, TQ), F32),
            pltpu.VMEM((LANES, IDX_HEADS * TQ), BF16),
            pltpu.VMEM((ATTN_HEADS, LANES, TQ), BF16),
            pltpu.VMEM((ATTN_WIDTH, TQ), F32),
            pltpu.VMEM((2, ATTN_HEADS, TQ), F32),
            pltpu.VMEM((2, ATTN_HEADS, TQ), F32),
            pltpu.VMEM((ATTN_HEADS, TQ), F32),
        ],
        compiler_params=pltpu.CompilerParams(
            dimension_semantics=("arbitrary", "arbitrary"), vmem_limit_bytes=VMEM_LIMIT),
        name="dsa_attn",
    )(qT, k, vT, iqT, ikw, ikwT)


def _post_kernel(x_ref, at_ref, uc_ref, sg_ref, wao3_ref, wco3_ref, wmix3_ref, wup3_ref, wdn3_ref,
                 gm_ref, gf_ref, o_ref, *, final):
    wao_ref, wco_ref, wmix_ref, wup_ref, wdn_ref = [
        r.at[0] for r in (wao3_ref, wco3_ref, wmix3_ref, wup3_ref, wdn3_ref)]
    ya = jnp.dot(at_ref[...], wao_ref[...], preferred_element_type=F32)
    yc = jnp.dot(uc_ref[...], wco_ref[...], preferred_element_type=F32)
    merged = (sg_ref[:, 0:D_MODEL].astype(F32) * ya
              + sg_ref[:, D_MODEL:2 * D_MODEL].astype(F32) * yc)
    x1 = x_ref[...] + jnp.dot(merged.astype(BF16), wmix_ref[...], preferred_element_type=F32)
    u = _rmsnorm(x1, gm_ref[...]).astype(BF16)
    x2 = x1
    for c in range(MLP_HIDDEN // MLP_CHUNK):
        hid = jnp.dot(u, wup_ref[:, c * MLP_CHUNK:(c + 1) * MLP_CHUNK], preferred_element_type=F32)
        hid = jnp.square(jnp.maximum(hid, 0.0)).astype(BF16)
        x2 = x2 + jnp.dot(hid, wdn_ref[c * MLP_CHUNK:(c + 1) * MLP_CHUNK, :], preferred_element_type=F32)
    if final:
        x2 = _rmsnorm(x2, gf_ref[...])
    o_ref[...] = x2


def _post_call(x2d, attn, uc, sg, weights, gm, gf, *, layer, final):
    m = x2d.shape[0]
    tm = TM_POST
    const = lambda i: (0, 0)
    row = lambda i: (i, 0)
    resident = lambda shape: pl.BlockSpec((1,) + shape, lambda i: (layer, 0, 0), pipeline_mode=pl.Buffered(1))
    return pl.pallas_call(
        functools.partial(_post_kernel, final=final),
        grid=(m // tm,),
        in_specs=[
            pl.BlockSpec((tm, D_MODEL), row),
            pl.BlockSpec((tm, ATTN_WIDTH), row),
            pl.BlockSpec((tm, CONV_WIDTH), row),
            pl.BlockSpec((tm, 2 * D_MODEL), row),
            resident((ATTN_WIDTH, D_MODEL)),
            resident((CONV_WIDTH, D_MODEL)),
            resident((D_MODEL, D_MODEL)),
            resident((D_MODEL, MLP_HIDDEN)),
            resident((MLP_HIDDEN, D_MODEL)),
            pl.BlockSpec((1, D_MODEL), const),
            pl.BlockSpec((1, D_MODEL), const),
        ],
        out_specs=pl.BlockSpec((tm, D_MODEL), row),
        out_shape=jax.ShapeDtypeStruct((m, D_MODEL), F32),
        compiler_params=pltpu.CompilerParams(
            dimension_semantics=("arbitrary",), vmem_limit_bytes=VMEM_LIMIT),
        name="mixer_tail",
    )(x2d, attn, uc, sg, *weights, gm, gf)


def _rope_tables(seq):
    half = ROPE_DIM // 2
    inv = 1.0 / (ROPE_THETA ** (jnp.arange(0, ROPE_DIM, 2, dtype=F32) / ROPE_DIM))
    lane = np.arange(LANES)
    in_head = lane % HEAD_DIM
    ang = jnp.arange(seq, dtype=F32)[:, None] * inv[lane % half][None, :]
    cos, sin = jnp.cos(ang), jnp.sin(ang)
    first = jnp.asarray(in_head < half)[None, :]
    second = jnp.asarray((in_head >= half) & (in_head < ROPE_DIM))[None, :]
    return (jnp.where(first | second, cos, 1.0), jnp.where(first, -sin, 0.0), jnp.where(second, sin, 0.0))


def kernel(x, norm_mix, w_in, conv_w, w_attn_out, w_conv_out, w_mix_out, norm_mlp, w_mlp_up, w_mlp_down,
           norm_final):
    batch, seq, d = x.shape
    depth = w_in.shape[0]
    assert d == D_MODEL and seq % TM_PROJ == 0 and TM_PROJ % TK == 0 and TQ == TK
    assert seq // 4 >= TOPK_MAX
    cosv, sa, sb = _rope_tables(seq)
    h = x.reshape(batch * seq, d)
    wt16 = jnp.swapaxes(w_in, 1, 2).astype(BF16)
    tail_w = [w.astype(BF16) for w in (w_attn_out, w_conv_out, w_mix_out, w_mlp_up, w_mlp_down)]
    for l in range(depth):
        qT, k, vT, iqT, ikw, ikwT, uc, sg = _proj_call(
            h, norm_mix[l][None, :], _wprep_call(wt16, l), cosv, sa, sb, conv_w[l],
            batch=batch, seq=seq)
        attn = _attn_call(qT, k, vT, iqT, ikw, ikwT, batch=batch, seq=seq)
        h = _post_call(
            h, attn.reshape(batch * seq, ATTN_WIDTH), uc, sg, tail_w,
            norm_mlp[l][None, :], norm_final[None, :], layer=l, final=(l == depth - 1))
    return h.reshape(batch, seq, d)
```

```python
import functools

import jax
import jax.numpy as jnp
import numpy as np
from jax import lax
from jax.experimental import pallas as pl
from jax.experimental.pallas import tpu as pltpu

F32 = jnp.float32
BF16 = jnp.bfloat16

D_MODEL = 1024
CHUNK = 64
EPS = 1e-6
HEAD_DIM = 64
ATTN_WIDTH = 512
ATTN_HEADS = 8
ROPE_DIM = 16
ROPE_THETA = 500000.0
IDX_HEADS = 4
IDX_DIM = 64
IDX_SCALE = (IDX_DIM ** -0.5) * (IDX_HEADS ** -0.5)
TOPK_MAX = 256
CONV_WIDTH = 512
CONV_K = 3
MLP_HIDDEN = 4 * D_MODEL

LANES = 128
SUBLANES = 8
BF16_ROWS = 16
QK_SCALE = (HEAD_DIM ** -0.5) * float(np.log2(np.e))

WBLK = 512
OFF_Q = 0
OFF_K = OFF_Q + ATTN_WIDTH
OFF_V = OFF_K + ATTN_WIDTH
IQ_WIDTH = IDX_HEADS * IDX_DIM
OFF_IQ = OFF_V + ATTN_WIDTH
OFF_IKW = OFF_IQ + IQ_WIDTH
IW_LANE = 96
OFF_CB = OFF_IQ + WBLK
OFF_CC = OFF_CB + CONV_WIDTH
OFF_CH = OFF_CC + CONV_WIDTH
OFF_G = OFF_CH + CONV_WIDTH
PROJ_COLS = OFF_G + 2 * D_MODEL
SRC_IQ = 3 * ATTN_WIDTH
SRC_IK = SRC_IQ + IDX_HEADS * IDX_DIM
SRC_IW = SRC_IK + IDX_DIM
SRC_CB = SRC_IW + IDX_HEADS

TM_PROJ = 1024
TQ = 256
TK = 256
TM_POST = 512
MLP_CHUNK = 1024
PASSES_UNCHECKED = 8
PASSES_PER_CHECK = 3
MAX_CHECKS = 192
GUESS_WIDEN = 1.3
NEG_BIAS = -2e30
M_INIT = -1e30
VMEM_LIMIT = 48 * 1024 * 1024


def _rmsnorm(x, g):
    ms = jnp.mean(x * x, axis=-1, keepdims=True)
    return x * lax.rsqrt(ms + EPS) * g


def _wprep_kernel(a3_ref, b3_ref, o_ref):
    a_ref = a3_ref.at[0]
    b_ref = b3_ref.at[0]
    s = pl.program_id(0)
    n_plain = OFF_IQ // WBLK

    def zeros(n):
        return jnp.zeros((n, D_MODEL), BF16)

    @pl.when(s < n_plain)
    def _():
        o_ref[...] = a_ref[...]

    @pl.when(s == n_plain)
    def _():
        iq0 = SRC_IQ % WBLK
        ik0 = SRC_IK % WBLK
        iw0 = SRC_IW % WBLK
        rows = lax.broadcasted_iota(jnp.int32, (BF16_ROWS, D_MODEL), 0)
        iw = jnp.where(rows < IDX_HEADS, a_ref[iw0:iw0 + BF16_ROWS, :].astype(F32), 0.0).astype(BF16)
        o_ref[...] = jnp.concatenate(
            [a_ref[iq0:iq0 + IQ_WIDTH, :], a_ref[ik0:ik0 + IDX_DIM, :], zeros(IW_LANE - IDX_DIM), iw,
             zeros(WBLK - IQ_WIDTH - IW_LANE - BF16_ROWS)], axis=0)

    @pl.when(s > n_plain)
    def _():
        shift = SRC_CB % WBLK
        for c in range(D_MODEL // (2 * LANES)):
            cols = slice(c * 2 * LANES, (c + 1) * 2 * LANES)
            both = jnp.concatenate([a_ref[:, cols].astype(F32), b_ref[:, cols].astype(F32)], axis=0)
            o_ref[:, cols] = pltpu.roll(both, 2 * WBLK - shift, 0)[0:WBLK, :].astype(BF16)


def _wprep_call(wt16, layer):
    n_plain = OFF_IQ // WBLK
    blk0 = SRC_CB // WBLK
    assert SRC_IQ // WBLK == SRC_IK // WBLK == SRC_IW // WBLK == blk0 == n_plain
    assert SRC_IW % WBLK % BF16_ROWS == 0 and (SRC_IW % WBLK) + BF16_ROWS <= WBLK

    def blk_a(s):
        return jnp.where(s < n_plain, s, jnp.where(s == n_plain, blk0, s - (n_plain + 1) + blk0))

    def blk_b(s):
        return jnp.where(s <= n_plain, 0, s - (n_plain + 1) + blk0 + 1)

    return pl.pallas_call(
        _wprep_kernel,
        grid=(PROJ_COLS // WBLK,),
        in_specs=[pl.BlockSpec((1, WBLK, D_MODEL), lambda s: (layer, blk_a(s), 0)),
                  pl.BlockSpec((1, WBLK, D_MODEL), lambda s: (layer, blk_b(s), 0))],
        out_specs=pl.BlockSpec((WBLK, D_MODEL), lambda s: (s, 0)),
        out_shape=jax.ShapeDtypeStruct((PROJ_COLS, D_MODEL), BF16),
        compiler_params=pltpu.CompilerParams(
            dimension_semantics=("arbitrary",), vmem_limit_bytes=VMEM_LIMIT),
        name="w_arrange",
    )(wt16, wt16)


def _proj_kernel(x_ref, g_ref, wt_ref, cos_ref, sa_ref, sb_ref, cw_ref,
                 qT_ref, k_ref, vT_ref, iqT_ref, ikw_ref, ikwT_ref, uc_ref, sg_ref,
                 zbuf, *, tiles_per_seq):
    i = pl.program_id(0)
    tm = x_ref.shape[0]
    u = _rmsnorm(x_ref[...], g_ref[...]).astype(BF16)
    cosv = cos_ref[...]
    sa = sa_ref[...]
    sb = sb_ref[...]

    def proj(c0, n):
        return lax.dot_general(u, wt_ref[c0:c0 + n, :], (((1,), (1,)), ((), ())),
                               preferred_element_type=F32)

    def rope(a):
        outs = []
        for gidx in range(a.shape[1] // LANES):
            ag = a[:, gidx * LANES:(gidx + 1) * LANES]
            outs.append(ag * cosv
                        + pltpu.roll(ag, LANES - ROPE_DIM // 2, 1) * sa
                        + pltpu.roll(ag, ROPE_DIM // 2, 1) * sb)
        return outs[0] if len(outs) == 1 else jnp.concatenate(outs, axis=1)

    q = rope(proj(OFF_Q, ATTN_WIDTH)) * QK_SCALE
    qT_ref[0] = q.T.astype(BF16)
    k_ref[0] = rope(proj(OFF_K, ATTN_WIDTH)).astype(BF16)
    v_t = proj(OFF_V, ATTN_WIDTH).T.astype(BF16)
    for t in range(tm // TK):
        vT_ref[0, t] = v_t[:, t * TK:(t + 1) * TK]
    iqT_ref[0] = rope(proj(OFF_IQ, IQ_WIDTH)).T.astype(BF16)
    ikw = rope(proj(OFF_IKW, LANES))
    ikw_ref[0] = ikw.astype(BF16)
    ikwT_ref[0] = ikw.T

    z = proj(OFF_CC, CONV_WIDTH) * proj(OFF_CH, CONV_WIDTH)

    @pl.when(i % tiles_per_seq == 0)
    def _():
        zbuf[0:SUBLANES, :] = jnp.zeros((SUBLANES, CONV_WIDTH), F32)

    @pl.when(i % tiles_per_seq != 0)
    def _():
        zbuf[0:SUBLANES, :] = zbuf[tm:tm + SUBLANES, :]

    zbuf[SUBLANES:SUBLANES + tm, :] = z
    z1 = zbuf[SUBLANES - 1:SUBLANES - 1 + tm, :]
    z2 = zbuf[SUBLANES - 2:SUBLANES - 2 + tm, :]
    conv = z2 * cw_ref[0:1, :] + z1 * cw_ref[1:2, :] + z * cw_ref[2:3, :]
    uc_ref[...] = (proj(OFF_CB, CONV_WIDTH) * conv).astype(BF16)

    for c in range(4):
        gc = proj(OFF_G + c * 512, 512)
        sg_ref[:, c * 512:(c + 1) * 512] = jax.nn.sigmoid(gc).astype(BF16)


def _proj_call(x2d, gain, wt, cosv, sa, sb, cw, *, batch, seq):
    m = x2d.shape[0]
    tm = TM_PROJ
    nt = seq // tm
    kt_per_step = tm // TK
    const = lambda i: (0, 0)
    out_shape = (
        jax.ShapeDtypeStruct((batch, ATTN_WIDTH, seq), BF16),
        jax.ShapeDtypeStruct((batch, seq, ATTN_WIDTH), BF16),
        jax.ShapeDtypeStruct((batch, seq // TK, ATTN_WIDTH, TK), BF16),
        jax.ShapeDtypeStruct((batch, IQ_WIDTH, seq), BF16),
        jax.ShapeDtypeStruct((batch, seq, LANES), BF16),
        jax.ShapeDtypeStruct((batch, LANES, seq), F32),
        jax.ShapeDtypeStruct((m, CONV_WIDTH), BF16),
        jax.ShapeDtypeStruct((m, 2 * D_MODEL), BF16),
    )
    in_specs = [
        pl.BlockSpec((tm, D_MODEL), lambda i: (i, 0)),
        pl.BlockSpec((1, D_MODEL), const),
        pl.BlockSpec((PROJ_COLS, D_MODEL), const, pipeline_mode=pl.Buffered(1)),
        pl.BlockSpec((tm, LANES), lambda i: (i % nt, 0)),
        pl.BlockSpec((tm, LANES), lambda i: (i % nt, 0)),
        pl.BlockSpec((tm, LANES), lambda i: (i % nt, 0)),
        pl.BlockSpec((CONV_K, CONV_WIDTH), const),
    ]
    out_specs = (
        pl.BlockSpec((1, ATTN_WIDTH, tm), lambda i: (i // nt, 0, i % nt)),
        pl.BlockSpec((1, tm, ATTN_WIDTH), lambda i: (i // nt, i % nt, 0)),
        pl.BlockSpec((1, kt_per_step, ATTN_WIDTH, TK), lambda i: (i // nt, i % nt, 0, 0)),
        pl.BlockSpec((1, IQ_WIDTH, tm), lambda i: (i // nt, 0, i % nt)),
        pl.BlockSpec((1, tm, LANES), lambda i: (i // nt, i % nt, 0)),
        pl.BlockSpec((1, LANES, tm), lambda i: (i // nt, 0, i % nt)),
        pl.BlockSpec((tm, CONV_WIDTH), lambda i: (i, 0)),
        pl.BlockSpec((tm, 2 * D_MODEL), lambda i: (i, 0)),
    )
    return pl.pallas_call(
        functools.partial(_proj_kernel, tiles_per_seq=nt),
        grid=(m // tm,),
        in_specs=in_specs,
        out_specs=out_specs,
        out_shape=out_shape,
        scratch_shapes=[pltpu.VMEM((tm + SUBLANES, CONV_WIDTH), F32)],
        compiler_params=pltpu.CompilerParams(
            dimension_semantics=("arbitrary",), vmem_limit_bytes=VMEM_LIMIT),
        name="proj",
    )(x2d, gain, wt, cosv, sa, sb, cw)


def _attn_kernel(qT_ref, k_ref, vT_ref, iqT_ref, ikw_ref, ikwT_ref, o_ref,
                 s_ref, bias_ref, sb_ref, iq_ref, qz_ref, acc_ref, m_ref, a_ref, l_ref):
    j = pl.program_id(1)
    nkt = j + 1
    grp = TK // SUBLANES

    def tile_start(kt):
        return pl.multiple_of(kt * TK, TK)

    def vreduce(x, op):
        parts = [x[g * SUBLANES:(g + 1) * SUBLANES, :] for g in range(grp)]
        while len(parts) > 1:
            parts = [op(parts[i], parts[i + 1]) for i in range(0, len(parts), 2)]
        return parts[0]

    w_t = ikwT_ref[0, IW_LANE:IW_LANE + SUBLANES, :] * IDX_SCALE
    tpos = j * TQ + lax.broadcasted_iota(jnp.int32, (1, TQ), 1)
    limit = (tpos // CHUNK + 1) * CHUNK
    nadm = limit.astype(F32)
    kvec = jnp.minimum(limit, TOPK_MAX).astype(F32)

    zpad = jnp.zeros((LANES - IDX_DIM, TQ), BF16)
    iq_ref[...] = jnp.concatenate(
        [jnp.concatenate([iqT_ref[0, h * IDX_DIM:(h + 1) * IDX_DIM, :], zpad], axis=0)
         for h in range(IDX_HEADS)], axis=1)

    def idx_logits(kt):
        return jnp.dot(ikw_ref[0, pl.ds(tile_start(kt), TK), :], iq_ref[...],
                       preferred_element_type=F32)

    def idx_scores(kt, lg, stats):
        amax, cgt, cge, ssum, sabs = stats
        sc = jnp.maximum(lg[:, 0:TQ], 0.0) * w_t[0:1, :]
        for h in range(1, IDX_HEADS):
            sc = sc + jnp.maximum(lg[:, h * TQ:(h + 1) * TQ], 0.0) * w_t[h:h + 1, :]
        k0 = tile_start(kt)
        adm = (k0 + lax.broadcasted_iota(jnp.int32, (TK, 1), 0)) < limit
        sm = jnp.where(adm, sc, -jnp.inf)
        s_ref[pl.ds(k0, TK), :] = sm
        sz = jnp.where(adm, sc, 0.0)
        az = jnp.abs(sz)
        return (jnp.maximum(amax, vreduce(az, jnp.maximum)),
                cgt + vreduce(jnp.where(sm > 0.0, 1.0, 0.0), jnp.add),
                cge + vreduce(jnp.where(sm >= 0.0, 1.0, 0.0), jnp.add),
                ssum + vreduce(sz, jnp.add),
                sabs + vreduce(az, jnp.add))

    def p1_pair(i, stats):
        lg0 = idx_logits(2 * i)
        lg1 = idx_logits(2 * i + 1)
        return idx_scores(2 * i + 1, lg1, idx_scores(2 * i, lg0, stats))

    z8 = jnp.zeros((SUBLANES, TQ), F32)
    stats = lax.fori_loop(0, nkt // 2, p1_pair, (z8, z8, z8, z8, z8))
    stats = lax.cond(nkt % 2 == 1,
                     lambda st: idx_scores(nkt - 1, idx_logits(nkt - 1), st),
                     lambda st: st, stats)
    maxabs = stats[0].max(axis=0, keepdims=True)
    cg0, ce0, ssum, sabs = [x.sum(axis=0, keepdims=True) for x in stats[1:]]

    def count_gt(bound):
        b8 = jnp.broadcast_to(bound, (SUBLANES, TQ))

        def body(kt, accs):
            blk = s_ref[pl.ds(tile_start(kt), TK), :]
            accs = list(accs)
            for g in range(grp):
                part = blk[g * SUBLANES:(g + 1) * SUBLANES, :]
                accs[g % 4] = jnp.where(part > b8, accs[g % 4] + 1.0, accs[g % 4])
            return tuple(accs)

        z8 = jnp.zeros((SUBLANES, TQ), F32)
        accs = lax.fori_loop(0, nkt // 2, lambda i, a: body(2 * i + 1, body(2 * i, a)), (z8, z8, z8, z8))
        accs = lax.cond(nkt % 2 == 1, lambda a: body(nkt - 1, a), lambda a: a, accs)
        a = (accs[0] + accs[1]) + (accs[2] + accs[3])
        return a.sum(axis=0, keepdims=True)

    one = jnp.ones((1, TQ), F32)
    zero = jnp.zeros((1, TQ), F32)
    tgt = kvec - 0.5
    allsel = nadm <= kvec
    tie0 = jnp.logical_and(cg0 < kvec, ce0 >= kvec)
    pos = cg0 > kvec
    done0 = jnp.logical_or(allsel, jnp.logical_or(tie0, cg0 == kvec))
    theta0 = jnp.where(allsel, -jnp.inf, 0.0)
    tie_thr0 = jnp.where(jnp.logical_and(tie0, jnp.logical_not(allsel)), 0.0, jnp.inf)
    init = (
        jnp.int32(0), jnp.int32(1),
        jnp.where(pos, 0.0, -2.0 * maxabs), jnp.where(pos, maxabs, 0.0),
        jnp.where(pos, cg0, nadm) - tgt, jnp.where(pos, 0.0, ce0) - tgt,
        jnp.where(pos, 0.0, cg0), zero,
        jnp.where(done0, one, zero), theta0, tie_thr0, cg0,
    )

    def one_pass(st, hint=None):
        lo, hi, flo, fhi, chi, side, done, theta, tie_thr, tie_cgt = st
        interp = lo + (hi - lo) * (flo / (flo - fhi))
        bis = 0.5 * lo + 0.5 * hi
        mid = jnp.where(jnp.logical_and(interp > lo, interp < hi), interp, bis)
        if hint is not None:
            mid = jnp.where(jnp.logical_and(hint > lo, hint < hi), hint, mid)
        inside = jnp.logical_and(mid > lo, mid < hi)
        c = count_gt(mid)
        active = done < 0.5
        live = jnp.logical_and(active, inside)
        hit = jnp.logical_and(live, c == kvec)
        stuck = jnp.logical_and(active, jnp.logical_not(inside))
        up = jnp.logical_and(live, c > kvec)
        dn = jnp.logical_and(live, c < kvec)
        theta = jnp.where(hit, mid, jnp.where(stuck, hi, theta))
        tie_thr = jnp.where(stuck, hi, tie_thr)
        tie_cgt = jnp.where(stuck, chi, tie_cgt)
        done = jnp.where(jnp.logical_or(hit, stuck), 1.0, done)
        fhi_n = jnp.where(dn, c - tgt, jnp.where(jnp.logical_and(up, side > 0.5), fhi * 0.5, fhi))
        flo_n = jnp.where(up, c - tgt, jnp.where(jnp.logical_and(dn, side < -0.5), flo * 0.5, flo))
        return (jnp.where(up, mid, lo), jnp.where(dn, mid, hi), flo_n, fhi_n,
                jnp.where(dn, c, chi), jnp.where(up, 1.0, jnp.where(dn, -1.0, side)),
                done, theta, tie_thr, tie_cgt)

    npos = jnp.maximum(cg0, 1.0)
    nneg = jnp.maximum(nadm - ce0, 1.0)
    mean_pos = 0.5 * (sabs + ssum) / npos
    mean_neg = 0.5 * (sabs - ssum) / nneg
    frac_neg = jnp.minimum(jnp.maximum((kvec - ce0) / nneg, 1e-6), 1.0 - 1e-6)
    guess = jnp.where(pos, mean_pos * jnp.log(npos / kvec), mean_neg * jnp.log(1.0 - frac_neg))
    st = one_pass(init[2:], hint=guess)
    widen = jnp.where(pos, GUESS_WIDEN, 1.0 / GUESS_WIDEN)
    st = one_pass(st, hint=jnp.where(st[5] > 0.5, st[0] * widen, st[1] / widen))
    st = lax.fori_loop(0, PASSES_UNCHECKED - 2, lambda _, s: one_pass(s), st)

    def pending_of(s):
        return jnp.sum(jnp.where(s[6] < 0.5, 1.0, 0.0)).astype(jnp.int32)

    def check_body(carry):
        s = carry[2:]
        for _ in range(PASSES_PER_CHECK):
            s = one_pass(s)
        return (carry[0] + 1, pending_of(s)) + s

    def check_cond(carry):
        return jnp.logical_and(carry[1] > 0, carry[0] < MAX_CHECKS)

    final = lax.while_loop(check_cond, check_body, (jnp.int32(0), jnp.int32(1)) + st)
    theta, tie_thr, tie_cgt = final[9], final[10], final[11]
    tie_budget = kvec - tie_cgt

    ri = lax.broadcasted_iota(jnp.int32, (TK, TK), 0)
    ci = lax.broadcasted_iota(jnp.int32, (TK, TK), 1)
    lower = jnp.where(ci < ri, 1.0, 0.0).astype(BF16)

    def mask_tile(kt, slot, budget_left):
        blk = s_ref[pl.ds(tile_start(kt), TK), :]
        is_tie = blk == tie_thr
        rank = jnp.dot(lower, jnp.where(is_tie, 1.0, 0.0).astype(BF16), preferred_element_type=F32)
        tie_bias = jnp.where(rank < budget_left, jnp.where(is_tie, 0.0, NEG_BIAS), NEG_BIAS)
        bias_ref[slot] = jnp.where(blk > theta, 0.0, tie_bias)
        return budget_left - (rank[TK - 1:TK, :] + jnp.where(is_tie[TK - 1:TK, :], 1.0, 0.0))

    rows = lax.broadcasted_iota(jnp.int32, (LANES, TQ), 0)
    for h in range(ATTN_HEADS):
        pair = qT_ref[0, (h // 2) * LANES:(h // 2 + 1) * LANES, :]
        qz_ref[h] = jnp.where((rows // HEAD_DIM) == (h % 2), pair, jnp.zeros_like(pair))
    acc_ref[...] = jnp.zeros_like(acc_ref)
    l_ref[...] = jnp.zeros_like(l_ref)

    def scores(kt, slot, h, m_old):
        k0 = tile_start(kt)
        kp = k_ref[0, pl.ds(k0, TK), (h // 2) * LANES:(h // 2 + 1) * LANES]
        s = jnp.dot(kp, qz_ref[h], preferred_element_type=F32) + bias_ref[slot]
        sb_ref[slot, h] = s
        m_h = jnp.maximum(m_old, vreduce(s, jnp.maximum).max(axis=0, keepdims=True))
        m_ref[slot, h:h + 1, :] = m_h
        a_ref[slot, h:h + 1, :] = jnp.exp2(m_old - m_h)

    ones_rows = jnp.ones((BF16_ROWS, TK), BF16)

    def values(kt, slot, h):
        alpha = a_ref[slot, h:h + 1, :]
        p = jnp.exp2(sb_ref[slot, h] - m_ref[slot, h:h + 1, :]).astype(BF16)
        v_h = jnp.concatenate([vT_ref[0, kt, h * HEAD_DIM:(h + 1) * HEAD_DIM, :], ones_rows], axis=0)
        pv = jnp.dot(v_h, p, preferred_element_type=F32)
        l_ref[h:h + 1, :] = alpha * l_ref[h:h + 1, :] + pv[HEAD_DIM:HEAD_DIM + 1, :]
        acc_ref[h * HEAD_DIM:(h + 1) * HEAD_DIM, :] = (
            alpha * acc_ref[h * HEAD_DIM:(h + 1) * HEAD_DIM, :] + pv[0:HEAD_DIM, :])

    def step(kt, slot, budget_left):
        budget_left = mask_tile(kt + 1, 1 - slot, budget_left)
        for h in range(ATTN_HEADS):
            scores(kt + 1, 1 - slot, h, m_ref[slot, h:h + 1, :])
            values(kt, slot, h)
        return budget_left

    left = mask_tile(0, 0, tie_budget)
    for h in range(ATTN_HEADS):
        scores(0, 0, h, jnp.full((1, TQ), M_INIT, F32))

    def p3(i, budget_left):
        return step(2 * i + 1, 1, step(2 * i, 0, budget_left))

    left = lax.fori_loop(0, (nkt - 1) // 2, p3, left)

    @pl.when(nkt % 2 == 0)
    def _():
        step(nkt - 2, 0, left)
        for h in range(ATTN_HEADS):
            values(nkt - 1, 1, h)

    @pl.when(nkt % 2 == 1)
    def _():
        for h in range(ATTN_HEADS):
            values(nkt - 1, 0, h)

    l_all = l_ref[...]
    outs = []
    for h in range(ATTN_HEADS):
        outs.append(acc_ref[h * HEAD_DIM:(h + 1) * HEAD_DIM, :] / l_all[h:h + 1, :])
    o_ref[0] = jnp.concatenate(outs, axis=0).T.astype(BF16)


def _attn_call(qT, k, vT, iqT, ikw, ikwT, *, batch, seq):
    nq = seq // TQ
    nkt = seq // TK
    return pl.pallas_call(
        _attn_kernel,
        grid=(batch, nq),
        in_specs=[
            pl.BlockSpec((1, ATTN_WIDTH, TQ), lambda b, j: (b, 0, j)),
            pl.BlockSpec((1, seq, ATTN_WIDTH), lambda b, j: (b, 0, 0)),
            pl.BlockSpec((1, nkt, ATTN_WIDTH, TK), lambda b, j: (b, 0, 0, 0)),
            pl.BlockSpec((1, IQ_WIDTH, TQ), lambda b, j: (b, 0, j)),
            pl.BlockSpec((1, seq, LANES), lambda b, j: (b, 0, 0)),
            pl.BlockSpec((1, LANES, TQ), lambda b, j: (b, 0, j)),
        ],
        out_specs=pl.BlockSpec((1, TQ, ATTN_WIDTH), lambda b, j: (b, j, 0)),
        out_shape=jax.ShapeDtypeStruct((batch, seq, ATTN_WIDTH), BF16),
        scratch_shapes=[
            pltpu.VMEM((seq, TQ), F32),
            pltpu.VMEM((2, TK, TQ), F32),
            pltpu.VMEM((2, ATTN_HEADS, TK, TQ), F32),
            pltpu.VMEM((LANES, IDX_HEADS * TQ), BF16),
            pltpu.VMEM((ATTN_HEADS, LANES, TQ), BF16),
            pltpu.VMEM((ATTN_WIDTH, TQ), F32),
            pltpu.VMEM((2, ATTN_HEADS, TQ), F32),
            pltpu.VMEM((2, ATTN_HEADS, TQ), F32),
            pltpu.VMEM((ATTN_HEADS, TQ), F32),
        ],
        compiler_params=pltpu.CompilerParams(
            dimension_semantics=("arbitrary", "arbitrary"), vmem_limit_bytes=VMEM_LIMIT),
        name="dsa_attn",
    )(qT, k, vT, iqT, ikw, ikwT)


def _post_kernel(x_ref, at_ref, uc_ref, sg_ref, wao3_ref, wco3_ref, wmix3_ref, wup3_ref, wdn3_ref,
                 gm_ref, gf_ref, o_ref, *, final):
    wao_ref, wco_ref, wmix_ref, wup_ref, wdn_ref = [
        r.at[0] for r in (wao3_ref, wco3_ref, wmix3_ref, wup3_ref, wdn3_ref)]
    ya = jnp.dot(at_ref[...], wao_ref[...], preferred_element_type=F32)
    yc = jnp.dot(uc_ref[...], wco_ref[...], preferred_element_type=F32)
    merged = (sg_ref[:, 0:D_MODEL].astype(F32) * ya
              + sg_ref[:, D_MODEL:2 * D_MODEL].astype(F32) * yc)
    x1 = x_ref[...] + jnp.dot(merged.astype(BF16), wmix_ref[...], preferred_element_type=F32)
    u = _rmsnorm(x1, gm_ref[...]).astype(BF16)
    x2 = x1
    for c in range(MLP_HIDDEN // MLP_CHUNK):
        hid = jnp.dot(u, wup_ref[:, c * MLP_CHUNK:(c + 1) * MLP_CHUNK], preferred_element_type=F32)
        hid = jnp.square(jnp.maximum(hid, 0.0)).astype(BF16)
        x2 = x2 + jnp.dot(hid, wdn_ref[c * MLP_CHUNK:(c + 1) * MLP_CHUNK, :], preferred_element_type=F32)
    if final:
        x2 = _rmsnorm(x2, gf_ref[...])
    o_ref[...] = x2


def _post_call(x2d, attn, uc, sg, weights, gm, gf, *, layer, final):
    m = x2d.shape[0]
    tm = TM_POST
    const = lambda i: (0, 0)
    row = lambda i: (i, 0)
    resident = lambda shape: pl.BlockSpec((1,) + shape, lambda i: (layer, 0, 0), pipeline_mode=pl.Buffered(1))
    return pl.pallas_call(
        functools.partial(_post_kernel, final=final),
        grid=(m // tm,),
        in_specs=[
            pl.BlockSpec((tm, D_MODEL), row),
            pl.BlockSpec((tm, ATTN_WIDTH), row),
            pl.BlockSpec((tm, CONV_WIDTH), row),
            pl.BlockSpec((tm, 2 * D_MODEL), row),
            resident((ATTN_WIDTH, D_MODEL)),
            resident((CONV_WIDTH, D_MODEL)),
            resident((D_MODEL, D_MODEL)),
            resident((D_MODEL, MLP_HIDDEN)),
            resident((MLP_HIDDEN, D_MODEL)),
            pl.BlockSpec((1, D_MODEL), const),
            pl.BlockSpec((1, D_MODEL), const),
        ],
        out_specs=pl.BlockSpec((tm, D_MODEL), row),
        out_shape=jax.ShapeDtypeStruct((m, D_MODEL), F32),
        compiler_params=pltpu.CompilerParams(
            dimension_semantics=("arbitrary",), vmem_limit_bytes=VMEM_LIMIT),
        name="mixer_tail",
    )(x2d, attn, uc, sg, *weights, gm, gf)


def _rope_tables(seq):
    half = ROPE_DIM // 2
    inv = 1.0 / (ROPE_THETA ** (jnp.arange(0, ROPE_DIM, 2, dtype=F32) / ROPE_DIM))
    lane = np.arange(LANES)
    in_head = lane % HEAD_DIM
    ang = jnp.arange(seq, dtype=F32)[:, None] * inv[lane % half][None, :]
    cos, sin = jnp.cos(ang), jnp.sin(ang)
    first = jnp.asarray(in_head < half)[None, :]
    second = jnp.asarray((in_head >= half) & (in_head < ROPE_DIM))[None, :]
    return (jnp.where(first | second, cos, 1.0), jnp.where(first, -sin, 0.0), jnp.where(second, sin, 0.0))


def kernel(x, norm_mix, w_in, conv_w, w_attn_out, w_conv_out, w_mix_out, norm_mlp, w_mlp_up, w_mlp_down,
           norm_final):
    batch, seq, d = x.shape
    depth = w_in.shape[0]
    assert d == D_MODEL and seq % TM_PROJ == 0 and TM_PROJ % TK == 0 and TQ == TK
    assert seq // 4 >= TOPK_MAX
    cosv, sa, sb = _rope_tables(seq)
    h = x.reshape(batch * seq, d)
    wt16 = jnp.swapaxes(w_in, 1, 2).astype(BF16)
    tail_w = [w.astype(BF16) for w in (w_attn_out, w_conv_out, w_mix_out, w_mlp_up, w_mlp_down)]
    for l in range(depth):
        qT, k, vT, iqT, ikw, ikwT, uc, sg = _proj_call(
            h, norm_mix[l][None, :], _wprep_call(wt16, l), cosv, sa, sb, conv_w[l],
            batch=batch, seq=seq)
        attn = _attn_call(qT, k, vT, iqT, ikw, ikwT, batch=batch, seq=seq)
        h = _post_call(
            h, attn.reshape(batch * seq, ATTN_WIDTH), uc, sg, tail_w,
            norm_mlp[l][None, :], norm_final[None, :], layer=l, final=(l == depth - 1))
    return h.reshape(batch, seq, d)
```

```python
import functools

import jax
import jax.numpy as jnp
import numpy as np
from jax import lax
from jax.experimental import pallas as pl
from jax.experimental.pallas import tpu as pltpu

F32 = jnp.float32
BF16 = jnp.bfloat16

D_MODEL = 1024
CHUNK = 64
EPS = 1e-6
HEAD_DIM = 64
ATTN_WIDTH = 512
ATTN_HEADS = 8
ROPE_DIM = 16
ROPE_THETA = 500000.0
IDX_HEADS = 4
IDX_DIM = 64
IDX_SCALE = (IDX_DIM ** -0.5) * (IDX_HEADS ** -0.5)
TOPK_MAX = 256
CONV_WIDTH = 512
CONV_K = 3
MLP_HIDDEN = 4 * D_MODEL

LANES = 128
SUBLANES = 8
BF16_ROWS = 16
QK_SCALE = (HEAD_DIM ** -0.5) * float(np.log2(np.e))

WBLK = 512
OFF_Q = 0
OFF_K = OFF_Q + ATTN_WIDTH
OFF_V = OFF_K + ATTN_WIDTH
IQ_WIDTH = IDX_HEADS * IDX_DIM
OFF_IQ = OFF_V + ATTN_WIDTH
OFF_IKW = OFF_IQ + IQ_WIDTH
IW_LANE = 96
OFF_CB = OFF_IQ + WBLK
OFF_CC = OFF_CB + CONV_WIDTH
OFF_CH = OFF_CC + CONV_WIDTH
OFF_G = OFF_CH + CONV_WIDTH
PROJ_COLS = OFF_G + 2 * D_MODEL
SRC_IQ = 3 * ATTN_WIDTH
SRC_IK = SRC_IQ + IDX_HEADS * IDX_DIM
SRC_IW = SRC_IK + IDX_DIM
SRC_CB = SRC_IW + IDX_HEADS

TM_PROJ = 1024
TQ = 256
TK = 256
TM_POST = 512
MLP_CHUNK = 1024
PASSES_UNCHECKED = 8
PASSES_PER_CHECK = 3
P3_TILES_PER_TRIP = 4
MAX_CHECKS = 192
GUESS_WIDEN = 1.3
NEG_BIAS = -2e30
M_INIT = -1e30
VMEM_LIMIT = 48 * 1024 * 1024


def _rmsnorm(x, g):
    ms = jnp.mean(x * x, axis=-1, keepdims=True)
    return x * lax.rsqrt(ms + EPS) * g


def _wprep_kernel(a3_ref, b3_ref, o_ref):
    a_ref = a3_ref.at[0]
    b_ref = b3_ref.at[0]
    s = pl.program_id(0)
    n_plain = OFF_IQ // WBLK

    def zeros(n):
        return jnp.zeros((n, D_MODEL), BF16)

    @pl.when(s < n_plain)
    def _():
        o_ref[...] = a_ref[...]

    @pl.when(s == n_plain)
    def _():
        iq0 = SRC_IQ % WBLK
        ik0 = SRC_IK % WBLK
        iw0 = SRC_IW % WBLK
        rows = lax.broadcasted_iota(jnp.int32, (BF16_ROWS, D_MODEL), 0)
        iw = jnp.where(rows < IDX_HEADS, a_ref[iw0:iw0 + BF16_ROWS, :].astype(F32), 0.0).astype(BF16)
        o_ref[...] = jnp.concatenate(
            [a_ref[iq0:iq0 + IQ_WIDTH, :], a_ref[ik0:ik0 + IDX_DIM, :], zeros(IW_LANE - IDX_DIM), iw,
             zeros(WBLK - IQ_WIDTH - IW_LANE - BF16_ROWS)], axis=0)

    @pl.when(s > n_plain)
    def _():
        shift = SRC_CB % WBLK
        for c in range(D_MODEL // (2 * LANES)):
            cols = slice(c * 2 * LANES, (c + 1) * 2 * LANES)
            both = jnp.concatenate([a_ref[:, cols].astype(F32), b_ref[:, cols].astype(F32)], axis=0)
            o_ref[:, cols] = pltpu.roll(both, 2 * WBLK - shift, 0)[0:WBLK, :].astype(BF16)


def _wprep_call(wt16, layer):
    n_plain = OFF_IQ // WBLK
    blk0 = SRC_CB // WBLK
    assert SRC_IQ // WBLK == SRC_IK // WBLK == SRC_IW // WBLK == blk0 == n_plain
    assert SRC_IW % WBLK % BF16_ROWS == 0 and (SRC_IW % WBLK) + BF16_ROWS <= WBLK

    def blk_a(s):
        return jnp.where(s < n_plain, s, jnp.where(s == n_plain, blk0, s - (n_plain + 1) + blk0))

    def blk_b(s):
        return jnp.where(s <= n_plain, 0, s - (n_plain + 1) + blk0 + 1)

    return pl.pallas_call(
        _wprep_kernel,
        grid=(PROJ_COLS // WBLK,),
        in_specs=[pl.BlockSpec((1, WBLK, D_MODEL), lambda s: (layer, blk_a(s), 0)),
                  pl.BlockSpec((1, WBLK, D_MODEL), lambda s: (layer, blk_b(s), 0))],
        out_specs=pl.BlockSpec((WBLK, D_MODEL), lambda s: (s, 0)),
        out_shape=jax.ShapeDtypeStruct((PROJ_COLS, D_MODEL), BF16),
        compiler_params=pltpu.CompilerParams(
            dimension_semantics=("arbitrary",), vmem_limit_bytes=VMEM_LIMIT),
        name="w_arrange",
    )(wt16, wt16)


def _proj_kernel(x_ref, g_ref, wt_ref, cos_ref, sa_ref, sb_ref, cw_ref,
                 qT_ref, k_ref, vT_ref, iqT_ref, ikw_ref, ikwT_ref, uc_ref, sg_ref,
                 zbuf, *, tiles_per_seq):
    i = pl.program_id(0)
    tm = x_ref.shape[0]
    u = _rmsnorm(x_ref[...], g_ref[...]).astype(BF16)
    cosv = cos_ref[...]
    sa = sa_ref[...]
    sb = sb_ref[...]

    def proj(c0, n):
        return lax.dot_general(u, wt_ref[c0:c0 + n, :], (((1,), (1,)), ((), ())),
                               preferred_element_type=F32)

    def rope(a):
        outs = []
        for gidx in range(a.shape[1] // LANES):
            ag = a[:, gidx * LANES:(gidx + 1) * LANES]
            outs.append(ag * cosv
                        + pltpu.roll(ag, LANES - ROPE_DIM // 2, 1) * sa
                        + pltpu.roll(ag, ROPE_DIM // 2, 1) * sb)
        return outs[0] if len(outs) == 1 else jnp.concatenate(outs, axis=1)

    q = rope(proj(OFF_Q, ATTN_WIDTH)) * QK_SCALE
    qT_ref[0] = q.T.astype(BF16)
    k_ref[0] = rope(proj(OFF_K, ATTN_WIDTH)).astype(BF16)
    v_t = proj(OFF_V, ATTN_WIDTH).T.astype(BF16)
    for t in range(tm // TK):
        vT_ref[0, t] = v_t[:, t * TK:(t + 1) * TK]
    iqT_ref[0] = rope(proj(OFF_IQ, IQ_WIDTH)).T.astype(BF16)
    ikw = rope(proj(OFF_IKW, LANES))
    ikw_ref[0] = ikw.astype(BF16)
    ikwT_ref[0] = ikw.T

    z = proj(OFF_CC, CONV_WIDTH) * proj(OFF_CH, CONV_WIDTH)

    @pl.when(i % tiles_per_seq == 0)
    def _():
        zbuf[0:SUBLANES, :] = jnp.zeros((SUBLANES, CONV_WIDTH), F32)

    @pl.when(i % tiles_per_seq != 0)
    def _():
        zbuf[0:SUBLANES, :] = zbuf[tm:tm + SUBLANES, :]

    zbuf[SUBLANES:SUBLANES + tm, :] = z
    z1 = zbuf[SUBLANES - 1:SUBLANES - 1 + tm, :]
    z2 = zbuf[SUBLANES - 2:SUBLANES - 2 + tm, :]
    conv = z2 * cw_ref[0:1, :] + z1 * cw_ref[1:2, :] + z * cw_ref[2:3, :]
    uc_ref[...] = (proj(OFF_CB, CONV_WIDTH) * conv).astype(BF16)

    for c in range(4):
        gc = proj(OFF_G + c * 512, 512)
        sg_ref[:, c * 512:(c + 1) * 512] = jax.nn.sigmoid(gc).astype(BF16)


def _proj_call(x2d, gain, wt, cosv, sa, sb, cw, *, batch, seq):
    m = x2d.shape[0]
    tm = TM_PROJ
    nt = seq // tm
    kt_per_step = tm // TK
    const = lambda i: (0, 0)
    out_shape = (
        jax.ShapeDtypeStruct((batch, ATTN_WIDTH, seq), BF16),
        jax.ShapeDtypeStruct((batch, seq, ATTN_WIDTH), BF16),
        jax.ShapeDtypeStruct((batch, seq // TK, ATTN_WIDTH, TK), BF16),
        jax.ShapeDtypeStruct((batch, IQ_WIDTH, seq), BF16),
        jax.ShapeDtypeStruct((batch, seq, LANES), BF16),
        jax.ShapeDtypeStruct((batch, LANES, seq), F32),
        jax.ShapeDtypeStruct((m, CONV_WIDTH), BF16),
        jax.ShapeDtypeStruct((m, 2 * D_MODEL), BF16),
    )
    in_specs = [
        pl.BlockSpec((tm, D_MODEL), lambda i: (i, 0)),
        pl.BlockSpec((1, D_MODEL), const),
        pl.BlockSpec((PROJ_COLS, D_MODEL), const, pipeline_mode=pl.Buffered(1)),
        pl.BlockSpec((tm, LANES), lambda i: (i % nt, 0)),
        pl.BlockSpec((tm, LANES), lambda i: (i % nt, 0)),
        pl.BlockSpec((tm, LANES), lambda i: (i % nt, 0)),
        pl.BlockSpec((CONV_K, CONV_WIDTH), const),
    ]
    out_specs = (
        pl.BlockSpec((1, ATTN_WIDTH, tm), lambda i: (i // nt, 0, i % nt)),
        pl.BlockSpec((1, tm, ATTN_WIDTH), lambda i: (i // nt, i % nt, 0)),
        pl.BlockSpec((1, kt_per_step, ATTN_WIDTH, TK), lambda i: (i // nt, i % nt, 0, 0)),
        pl.BlockSpec((1, IQ_WIDTH, tm), lambda i: (i // nt, 0, i % nt)),
        pl.BlockSpec((1, tm, LANES), lambda i: (i // nt, i % nt, 0)),
        pl.BlockSpec((1, LANES, tm), lambda i: (i // nt, 0, i % nt)),
        pl.BlockSpec((tm, CONV_WIDTH), lambda i: (i, 0)),
        pl.BlockSpec((tm, 2 * D_MODEL), lambda i: (i, 0)),
    )
    return pl.pallas_call(
        functools.partial(_proj_kernel, tiles_per_seq=nt),
        grid=(m // tm,),
        in_specs=in_specs,
        out_specs=out_specs,
        out_shape=out_shape,
        scratch_shapes=[pltpu.VMEM((tm + SUBLANES, CONV_WIDTH), F32)],
        compiler_params=pltpu.CompilerParams(
            dimension_semantics=("arbitrary",), vmem_limit_bytes=VMEM_LIMIT),
        name="proj",
    )(x2d, gain, wt, cosv, sa, sb, cw)


def _attn_kernel(qT_ref, k_ref, vT_ref, iqT_ref, ikw_ref, ikwT_ref, o_ref,
                 s_ref, bias_ref, sb_ref, iq_ref, qz_ref, acc_ref, m_ref, a_ref, l_ref):
    j = pl.program_id(1)
    nkt = j + 1
    grp = TK // SUBLANES

    def tile_start(kt):
        return pl.multiple_of(kt * TK, TK)

    def vreduce(x, op):
        parts = [x[g * SUBLANES:(g + 1) * SUBLANES, :] for g in range(grp)]
        while len(parts) > 1:
            parts = [op(parts[i], parts[i + 1]) for i in range(0, len(parts), 2)]
        return parts[0]

    w_t = ikwT_ref[0, IW_LANE:IW_LANE + SUBLANES, :] * IDX_SCALE
    tpos = j * TQ + lax.broadcasted_iota(jnp.int32, (1, TQ), 1)
    limit = (tpos // CHUNK + 1) * CHUNK
    nadm = limit.astype(F32)
    kvec = jnp.minimum(limit, TOPK_MAX).astype(F32)

    zpad = jnp.zeros((LANES - IDX_DIM, TQ), BF16)
    iq_ref[...] = jnp.concatenate(
        [jnp.concatenate([iqT_ref[0, h * IDX_DIM:(h + 1) * IDX_DIM, :], zpad], axis=0)
         for h in range(IDX_HEADS)], axis=1)

    def idx_logits(kt):
        return jnp.dot(ikw_ref[0, pl.ds(tile_start(kt), TK), :], iq_ref[...],
                       preferred_element_type=F32)

    def idx_scores(kt, lg, stats):
        amax, cgt, cge, ssum, sabs = stats
        sc = jnp.maximum(lg[:, 0:TQ], 0.0) * w_t[0:1, :]
        for h in range(1, IDX_HEADS):
            sc = sc + jnp.maximum(lg[:, h * TQ:(h + 1) * TQ], 0.0) * w_t[h:h + 1, :]
        k0 = tile_start(kt)
        adm = (k0 + lax.broadcasted_iota(jnp.int32, (TK, 1), 0)) < limit
        sm = jnp.where(adm, sc, -jnp.inf)
        s_ref[pl.ds(k0, TK), :] = sm
        sz = jnp.where(adm, sc, 0.0)
        az = jnp.abs(sz)
        return (jnp.maximum(amax, vreduce(az, jnp.maximum)),
                cgt + vreduce(jnp.where(sm > 0.0, 1.0, 0.0), jnp.add),
                cge + vreduce(jnp.where(sm >= 0.0, 1.0, 0.0), jnp.add),
                ssum + vreduce(sz, jnp.add),
                sabs + vreduce(az, jnp.add))

    def p1_pair(i, stats):
        lg0 = idx_logits(2 * i)
        lg1 = idx_logits(2 * i + 1)
        return idx_scores(2 * i + 1, lg1, idx_scores(2 * i, lg0, stats))

    z8 = jnp.zeros((SUBLANES, TQ), F32)
    stats = lax.fori_loop(0, nkt // 2, p1_pair, (z8, z8, z8, z8, z8))
    stats = lax.cond(nkt % 2 == 1,
                     lambda st: idx_scores(nkt - 1, idx_logits(nkt - 1), st),
                     lambda st: st, stats)
    maxabs = stats[0].max(axis=0, keepdims=True)
    cg0, ce0, ssum, sabs = [x.sum(axis=0, keepdims=True) for x in stats[1:]]

    def count_gt(bound):
        b8 = jnp.broadcast_to(bound, (SUBLANES, TQ))

        def body(kt, accs):
            blk = s_ref[pl.ds(tile_start(kt), TK), :]
            accs = list(accs)
            for g in range(grp):
                part = blk[g * SUBLANES:(g + 1) * SUBLANES, :]
                accs[g % 4] = jnp.where(part > b8, accs[g % 4] + 1.0, accs[g % 4])
            return tuple(accs)

        z8 = jnp.zeros((SUBLANES, TQ), F32)
        accs = lax.fori_loop(0, nkt // 2, lambda i, a: body(2 * i + 1, body(2 * i, a)), (z8, z8, z8, z8))
        accs = lax.cond(nkt % 2 == 1, lambda a: body(nkt - 1, a), lambda a: a, accs)
        a = (accs[0] + accs[1]) + (accs[2] + accs[3])
        return a.sum(axis=0, keepdims=True)

    one = jnp.ones((1, TQ), F32)
    zero = jnp.zeros((1, TQ), F32)
    tgt = kvec - 0.5
    allsel = nadm <= kvec
    tie0 = jnp.logical_and(cg0 < kvec, ce0 >= kvec)
    pos = cg0 > kvec
    done0 = jnp.logical_or(allsel, jnp.logical_or(tie0, cg0 == kvec))
    theta0 = jnp.where(allsel, -jnp.inf, 0.0)
    tie_thr0 = jnp.where(jnp.logical_and(tie0, jnp.logical_not(allsel)), 0.0, jnp.inf)
    init = (
        jnp.int32(0), jnp.int32(1),
        jnp.where(pos, 0.0, -2.0 * maxabs), jnp.where(pos, maxabs, 0.0),
        jnp.where(pos, cg0, nadm) - tgt, jnp.where(pos, 0.0, ce0) - tgt,
        jnp.where(pos, 0.0, cg0), zero,
        jnp.where(done0, one, zero), theta0, tie_thr0, cg0,
    )

    def one_pass(st, hint=None):
        lo, hi, flo, fhi, chi, side, done, theta, tie_thr, tie_cgt = st
        interp = lo + (hi - lo) * (flo / (flo - fhi))
        bis = 0.5 * lo + 0.5 * hi
        mid = jnp.where(jnp.logical_and(interp > lo, interp < hi), interp, bis)
        if hint is not None:
            mid = jnp.where(jnp.logical_and(hint > lo, hint < hi), hint, mid)
        inside = jnp.logical_and(mid > lo, mid < hi)
        c = count_gt(mid)
        active = done < 0.5
        live = jnp.logical_and(active, inside)
        hit = jnp.logical_and(live, c == kvec)
        stuck = jnp.logical_and(active, jnp.logical_not(inside))
        up = jnp.logical_and(live, c > kvec)
        dn = jnp.logical_and(live, c < kvec)
        theta = jnp.where(hit, mid, jnp.where(stuck, hi, theta))
        tie_thr = jnp.where(stuck, hi, tie_thr)
        tie_cgt = jnp.where(stuck, chi, tie_cgt)
        done = jnp.where(jnp.logical_or(hit, stuck), 1.0, done)
        fhi_n = jnp.where(dn, c - tgt, jnp.where(jnp.logical_and(up, side > 0.5), fhi * 0.5, fhi))
        flo_n = jnp.where(up, c - tgt, jnp.where(jnp.logical_and(dn, side < -0.5), flo * 0.5, flo))
        return (jnp.where(up, mid, lo), jnp.where(dn, mid, hi), flo_n, fhi_n,
                jnp.where(dn, c, chi), jnp.where(up, 1.0, jnp.where(dn, -1.0, side)),
                done, theta, tie_thr, tie_cgt)

    npos = jnp.maximum(cg0, 1.0)
    nneg = jnp.maximum(nadm - ce0, 1.0)
    mean_pos = 0.5 * (sabs + ssum) / npos
    mean_neg = 0.5 * (sabs - ssum) / nneg
    frac_neg = jnp.minimum(jnp.maximum((kvec - ce0) / nneg, 1e-6), 1.0 - 1e-6)
    guess = jnp.where(pos, mean_pos * jnp.log(npos / kvec), mean_neg * jnp.log(1.0 - frac_neg))
    st = one_pass(init[2:], hint=guess)
    widen = jnp.where(pos, GUESS_WIDEN, 1.0 / GUESS_WIDEN)
    st = one_pass(st, hint=jnp.where(st[5] > 0.5, st[0] * widen, st[1] / widen))
    st = lax.fori_loop(0, PASSES_UNCHECKED - 2, lambda _, s: one_pass(s), st)

    def pending_of(s):
        return jnp.sum(jnp.where(s[6] < 0.5, 1.0, 0.0)).astype(jnp.int32)

    def check_body(carry):
        s = carry[2:]
        for _ in range(PASSES_PER_CHECK):
            s = one_pass(s)
        return (carry[0] + 1, pending_of(s)) + s

    def check_cond(carry):
        return jnp.logical_and(carry[1] > 0, carry[0] < MAX_CHECKS)

    final = lax.while_loop(check_cond, check_body, (jnp.int32(0), jnp.int32(1)) + st)
    theta, tie_thr, tie_cgt = final[9], final[10], final[11]
    tie_budget = kvec - tie_cgt

    ri = lax.broadcasted_iota(jnp.int32, (TK, TK), 0)
    ci = lax.broadcasted_iota(jnp.int32, (TK, TK), 1)
    lower = jnp.where(ci < ri, 1.0, 0.0).astype(BF16)

    def mask_tile(kt, slot, budget_left):
        blk = s_ref[pl.ds(tile_start(kt), TK), :]
        is_tie = blk == tie_thr
        rank = jnp.dot(lower, jnp.where(is_tie, 1.0, 0.0).astype(BF16), preferred_element_type=F32)
        tie_bias = jnp.where(rank < budget_left, jnp.where(is_tie, 0.0, NEG_BIAS), NEG_BIAS)
        bias_ref[slot] = jnp.where(blk > theta, 0.0, tie_bias)
        return budget_left - (rank[TK - 1:TK, :] + jnp.where(is_tie[TK - 1:TK, :], 1.0, 0.0))

    rows = lax.broadcasted_iota(jnp.int32, (LANES, TQ), 0)
    for h in range(ATTN_HEADS):
        pair = qT_ref[0, (h // 2) * LANES:(h // 2 + 1) * LANES, :]
        qz_ref[h] = jnp.where((rows // HEAD_DIM) == (h % 2), pair, jnp.zeros_like(pair))
    acc_ref[...] = jnp.zeros_like(acc_ref)
    l_ref[...] = jnp.zeros_like(l_ref)

    def scores(kt, slot, h, m_old):
        k0 = tile_start(kt)
        kp = k_ref[0, pl.ds(k0, TK), (h // 2) * LANES:(h // 2 + 1) * LANES]
        s = jnp.dot(kp, qz_ref[h], preferred_element_type=F32) + bias_ref[slot]
        sb_ref[slot, h] = s
        m_h = jnp.maximum(m_old, vreduce(s, jnp.maximum).max(axis=0, keepdims=True))
        m_ref[slot, h:h + 1, :] = m_h
        a_ref[slot, h:h + 1, :] = jnp.exp2(m_old - m_h)

    ones_rows = jnp.ones((BF16_ROWS, TK), BF16)

    def values(kt, slot, h):
        alpha = a_ref[slot, h:h + 1, :]
        p = jnp.exp2(sb_ref[slot, h] - m_ref[slot, h:h + 1, :]).astype(BF16)
        v_h = jnp.concatenate([vT_ref[0, kt, h * HEAD_DIM:(h + 1) * HEAD_DIM, :], ones_rows], axis=0)
        pv = jnp.dot(v_h, p, preferred_element_type=F32)
        l_ref[h:h + 1, :] = alpha * l_ref[h:h + 1, :] + pv[HEAD_DIM:HEAD_DIM + 1, :]
        acc_ref[h * HEAD_DIM:(h + 1) * HEAD_DIM, :] = (
            alpha * acc_ref[h * HEAD_DIM:(h + 1) * HEAD_DIM, :] + pv[0:HEAD_DIM, :])

    def step(kt, slot, budget_left):
        budget_left = mask_tile(kt + 1, 1 - slot, budget_left)
        for h in range(ATTN_HEADS):
            scores(kt + 1, 1 - slot, h, m_ref[slot, h:h + 1, :])
            values(kt, slot, h)
        return budget_left

    left = mask_tile(0, 0, tie_budget)
    for h in range(ATTN_HEADS):
        scores(0, 0, h, jnp.full((1, TQ), M_INIT, F32))

    def p3(i, budget_left):
        for t in range(P3_TILES_PER_TRIP):
            budget_left = step(P3_TILES_PER_TRIP * i + t, t % 2, budget_left)
        return budget_left

    trips = (nkt - 1) // P3_TILES_PER_TRIP
    left = lax.fori_loop(0, trips, p3, left)
    tail_start = nkt - 2 + nkt % 2
    for t in range(P3_TILES_PER_TRIP - 2):
        kt = P3_TILES_PER_TRIP * trips + t
        left = lax.cond(kt < tail_start, functools.partial(step, kt, t % 2), lambda b: b, left)

    @pl.when(nkt % 2 == 0)
    def _():
        step(nkt - 2, 0, left)
        for h in range(ATTN_HEADS):
            values(nkt - 1, 1, h)

    @pl.when(nkt % 2 == 1)
    def _():
        for h in range(ATTN_HEADS):
            values(nkt - 1, 0, h)

    l_all = l_ref[...]
    outs = []
    for h in range(ATTN_HEADS):
        outs.append(acc_ref[h * HEAD_DIM:(h + 1) * HEAD_DIM, :] / l_all[h:h + 1, :])
    o_ref[0] = jnp.concatenate(outs, axis=0).T.astype(BF16)


def _attn_call(qT, k, vT, iqT, ikw, ikwT, *, batch, seq):
    nq = seq // TQ
    nkt = seq // TK
    return pl.pallas_call(
        _attn_kernel,
        grid=(batch, nq),
        in_specs=[
            pl.BlockSpec((1, ATTN_WIDTH, TQ), lambda b, j: (b, 0, j)),
            pl.BlockSpec((1, seq, ATTN_WIDTH), lambda b, j: (b, 0, 0)),
            pl.BlockSpec((1, nkt, ATTN_WIDTH, TK), lambda b, j: (b, 0, 0, 0)),
            pl.BlockSpec((1, IQ_WIDTH, TQ), lambda b, j: (b, 0, j)),
            pl.BlockSpec((1, seq, LANES), lambda b, j: (b, 0, 0)),
            pl.BlockSpec((1, LANES, TQ), lambda b, j: (b, 0, j)),
        ],
        out_specs=pl.BlockSpec((1, TQ, ATTN_WIDTH), lambda b, j: (b, j, 0)),
        out_shape=jax.ShapeDtypeStruct((batch, seq, ATTN_WIDTH), BF16),
        scratch_shapes=[
            pltpu.VMEM((seq, TQ), F32),
            pltpu.VMEM((2, TK, TQ), F32),
            pltpu.VMEM((2, ATTN_HEADS, TK, TQ), F32),
            pltpu.VMEM((LANES, IDX_HEADS * TQ), BF16),
            pltpu.VMEM((ATTN_HEADS, LANES, TQ), BF16),
            pltpu.VMEM((ATTN_WIDTH, TQ), F32),
            pltpu.VMEM((2, ATTN_HEADS, TQ), F32),
            pltpu.VMEM((2, ATTN_HEADS, TQ), F32),
            pltpu.VMEM((ATTN_HEADS, TQ), F32),
        ],
        compiler_params=pltpu.CompilerParams(
            dimension_semantics=("arbitrary", "arbitrary"), vmem_limit_bytes=VMEM_LIMIT),
        name="dsa_attn",
    )(qT, k, vT, iqT, ikw, ikwT)


def _post_kernel(x_ref, at_ref, uc_ref, sg_ref, wao3_ref, wco3_ref, wmix3_ref, wup3_ref, wdn3_ref,
                 gm_ref, gf_ref, o_ref, *, final):
    wao_ref, wco_ref, wmix_ref, wup_ref, wdn_ref = [
        r.at[0] for r in (wao3_ref, wco3_ref, wmix3_ref, wup3_ref, wdn3_ref)]
    ya = jnp.dot(at_ref[...], wao_ref[...], preferred_element_type=F32)
    yc = jnp.dot(uc_ref[...], wco_ref[...], preferred_element_type=F32)
    merged = (sg_ref[:, 0:D_MODEL].astype(F32) * ya
              + sg_ref[:, D_MODEL:2 * D_MODEL].astype(F32) * yc)
    x1 = x_ref[...] + jnp.dot(merged.astype(BF16), wmix_ref[...], preferred_element_type=F32)
    u = _rmsnorm(x1, gm_ref[...]).astype(BF16)
    x2 = x1
    for c in range(MLP_HIDDEN // MLP_CHUNK):
        hid = jnp.dot(u, wup_ref[:, c * MLP_CHUNK:(c + 1) * MLP_CHUNK], preferred_element_type=F32)
        hid = jnp.square(jnp.maximum(hid, 0.0)).astype(BF16)
        x2 = x2 + jnp.dot(hid, wdn_ref[c * MLP_CHUNK:(c + 1) * MLP_CHUNK, :], preferred_element_type=F32)
    if final:
        x2 = _rmsnorm(x2, gf_ref[...])
    o_ref[...] = x2


def _post_call(x2d, attn, uc, sg, weights, gm, gf, *, layer, final):
    m = x2d.shape[0]
    tm = TM_POST
    const = lambda i: (0, 0)
    row = lambda i: (i, 0)
    resident = lambda shape: pl.BlockSpec((1,) + shape, lambda i: (layer, 0, 0), pipeline_mode=pl.Buffered(1))
    return pl.pallas_call(
        functools.partial(_post_kernel, final=final),
        grid=(m // tm,),
        in_specs=[
            pl.BlockSpec((tm, D_MODEL), row),
            pl.BlockSpec((tm, ATTN_WIDTH), row),
            pl.BlockSpec((tm, CONV_WIDTH), row),
            pl.BlockSpec((tm, 2 * D_MODEL), row),
            resident((ATTN_WIDTH, D_MODEL)),
            resident((CONV_WIDTH, D_MODEL)),
            resident((D_MODEL, D_MODEL)),
            resident((D_MODEL, MLP_HIDDEN)),
            resident((MLP_HIDDEN, D_MODEL)),
            pl.BlockSpec((1, D_MODEL), const),
            pl.BlockSpec((1, D_MODEL), const),
        ],
        out_specs=pl.BlockSpec((tm, D_MODEL), row),
        out_shape=jax.ShapeDtypeStruct((m, D_MODEL), F32),
        compiler_params=pltpu.CompilerParams(
            dimension_semantics=("arbitrary",), vmem_limit_bytes=VMEM_LIMIT),
        name="mixer_tail",
    )(x2d, attn, uc, sg, *weights, gm, gf)


def _rope_tables(seq):
    half = ROPE_DIM // 2
    inv = 1.0 / (ROPE_THETA ** (jnp.arange(0, ROPE_DIM, 2, dtype=F32) / ROPE_DIM))
    lane = np.arange(LANES)
    in_head = lane % HEAD_DIM
    ang = jnp.arange(seq, dtype=F32)[:, None] * inv[lane % half][None, :]
    cos, sin = jnp.cos(ang), jnp.sin(ang)
    first = jnp.asarray(in_head < half)[None, :]
    second = jnp.asarray((in_head >= half) & (in_head < ROPE_DIM))[None, :]
    return (jnp.where(first | second, cos, 1.0), jnp.where(first, -sin, 0.0), jnp.where(second, sin, 0.0))


def kernel(x, norm_mix, w_in, conv_w, w_attn_out, w_conv_out, w_mix_out, norm_mlp, w_mlp_up, w_mlp_down,
           norm_final):
    batch, seq, d = x.shape
    depth = w_in.shape[0]
    assert d == D_MODEL and seq % TM_PROJ == 0 and TM_PROJ % TK == 0 and TQ == TK
    assert seq // 4 >= TOPK_MAX
    cosv, sa, sb = _rope_tables(seq)
    h = x.reshape(batch * seq, d)
    wt16 = jnp.swapaxes(w_in, 1, 2).astype(BF16)
    tail_w = [w.astype(BF16) for w in (w_attn_out, w_conv_out, w_mix_out, w_mlp_up, w_mlp_down)]
    for l in range(depth):
        qT, k, vT, iqT, ikw, ikwT, uc, sg = _proj_call(
            h, norm_mix[l][None, :], _wprep_call(wt16, l), cosv, sa, sb, conv_w[l],
            batch=batch, seq=seq)
        attn = _attn_call(qT, k, vT, iqT, ikw, ikwT, batch=batch, seq=seq)
        h = _post_call(
            h, attn.reshape(batch * seq, ATTN_WIDTH), uc, sg, tail_w,
            norm_mlp[l][None, :], norm_final[None, :], layer=l, final=(l == depth - 1))
    return h.reshape(batch, seq, d)
```

```python
import functools

import jax
import jax.numpy as jnp
import numpy as np
from jax import lax
from jax.experimental import pallas as pl
from jax.experimental.pallas import tpu as pltpu

F32 = jnp.float32
BF16 = jnp.bfloat16

D_MODEL = 1024
CHUNK = 64
EPS = 1e-6
HEAD_DIM = 64
ATTN_WIDTH = 512
ATTN_HEADS = 8
ROPE_DIM = 16
ROPE_THETA = 500000.0
IDX_HEADS = 4
IDX_DIM = 64
IDX_SCALE = (IDX_DIM ** -0.5) * (IDX_HEADS ** -0.5)
TOPK_MAX = 256
CONV_WIDTH = 512
CONV_K = 3
MLP_HIDDEN = 4 * D_MODEL

LANES = 128
SUBLANES = 8
BF16_ROWS = 16
QK_SCALE = (HEAD_DIM ** -0.5) * float(np.log2(np.e))

WBLK = 512
OFF_Q = 0
OFF_K = OFF_Q + ATTN_WIDTH
OFF_V = OFF_K + ATTN_WIDTH
IQ_WIDTH = IDX_HEADS * IDX_DIM
OFF_IQ = OFF_V + ATTN_WIDTH
OFF_IKW = OFF_IQ + IQ_WIDTH
IW_LANE = 96
OFF_CB = OFF_IQ + WBLK
OFF_CC = OFF_CB + CONV_WIDTH
OFF_CH = OFF_CC + CONV_WIDTH
OFF_G = OFF_CH + CONV_WIDTH
PROJ_COLS = OFF_G + 2 * D_MODEL
SRC_IQ = 3 * ATTN_WIDTH
SRC_IK = SRC_IQ + IDX_HEADS * IDX_DIM
SRC_IW = SRC_IK + IDX_DIM
SRC_CB = SRC_IW + IDX_HEADS

TM_PROJ = 1024
TQ = 256
TK = 256
TM_POST = 512
MLP_CHUNK = 1024
PASSES_UNCHECKED = 8
PASSES_PER_CHECK = 3
P3_TILES_PER_TRIP = 4
MAX_CHECKS = 192
GUESS_WIDEN = 1.3
NEG_BIAS = -2e30
M_INIT = -1e30
VMEM_LIMIT = 48 * 1024 * 1024


def _rmsnorm(x, g):
    ms = jnp.mean(x * x, axis=-1, keepdims=True)
    return x * lax.rsqrt(ms + EPS) * g


def _wprep_kernel(a3_ref, b3_ref, o_ref):
    a_ref = a3_ref.at[0]
    b_ref = b3_ref.at[0]
    s = pl.program_id(0)
    n_plain = OFF_IQ // WBLK

    def zeros(n):
        return jnp.zeros((n, D_MODEL), BF16)

    @pl.when(s < n_plain)
    def _():
        o_ref[...] = a_ref[...]

    @pl.when(s == n_plain)
    def _():
        iq0 = SRC_IQ % WBLK
        ik0 = SRC_IK % WBLK
        iw0 = SRC_IW % WBLK
        rows = lax.broadcasted_iota(jnp.int32, (BF16_ROWS, D_MODEL), 0)
        iw = jnp.where(rows < IDX_HEADS, a_ref[iw0:iw0 + BF16_ROWS, :].astype(F32), 0.0).astype(BF16)
        o_ref[...] = jnp.concatenate(
            [a_ref[iq0:iq0 + IQ_WIDTH, :], a_ref[ik0:ik0 + IDX_DIM, :], zeros(IW_LANE - IDX_DIM), iw,
             zeros(WBLK - IQ_WIDTH - IW_LANE - BF16_ROWS)], axis=0)

    @pl.when(s > n_plain)
    def _():
        shift = SRC_CB % WBLK
        for c in range(D_MODEL // (2 * LANES)):
            cols = slice(c * 2 * LANES, (c + 1) * 2 * LANES)
            both = jnp.concatenate([a_ref[:, cols].astype(F32), b_ref[:, cols].astype(F32)], axis=0)
            o_ref[:, cols] = pltpu.roll(both, 2 * WBLK - shift, 0)[0:WBLK, :].astype(BF16)


def _wprep_call(wt16, layer):
    n_plain = OFF_IQ // WBLK
    blk0 = SRC_CB // WBLK
    assert SRC_IQ // WBLK == SRC_IK // WBLK == SRC_IW // WBLK == blk0 == n_plain
    assert SRC_IW % WBLK % BF16_ROWS == 0 and (SRC_IW % WBLK) + BF16_ROWS <= WBLK

    def blk_a(s):
        return jnp.where(s < n_plain, s, jnp.where(s == n_plain, blk0, s - (n_plain + 1) + blk0))

    def blk_b(s):
        return jnp.where(s <= n_plain, 0, s - (n_plain + 1) + blk0 + 1)

    return pl.pallas_call(
        _wprep_kernel,
        grid=(PROJ_COLS // WBLK,),
        in_specs=[pl.BlockSpec((1, WBLK, D_MODEL), lambda s: (layer, blk_a(s), 0)),
                  pl.BlockSpec((1, WBLK, D_MODEL), lambda s: (layer, blk_b(s), 0))],
        out_specs=pl.BlockSpec((WBLK, D_MODEL), lambda s: (s, 0)),
        out_shape=jax.ShapeDtypeStruct((PROJ_COLS, D_MODEL), BF16),
        compiler_params=pltpu.CompilerParams(
            dimension_semantics=("arbitrary",), vmem_limit_bytes=VMEM_LIMIT),
        name="w_arrange",
    )(wt16, wt16)


def _proj_kernel(x_ref, g_ref, wt_ref, cos_ref, sa_ref, sb_ref, cw_ref,
                 qT_ref, k_ref, vT_ref, iqT_ref, ikw_ref, ikwT_ref, uc_ref, sg_ref,
                 zbuf, *, tiles_per_seq):
    i = pl.program_id(0)
    tm = x_ref.shape[0]
    u = _rmsnorm(x_ref[...], g_ref[...]).astype(BF16)
    cosv = cos_ref[...]
    sa = sa_ref[...]
    sb = sb_ref[...]

    def proj(c0, n):
        return lax.dot_general(u, wt_ref[c0:c0 + n, :], (((1,), (1,)), ((), ())),
                               preferred_element_type=F32)

    def rope(a):
        outs = []
        for gidx in range(a.shape[1] // LANES):
            ag = a[:, gidx * LANES:(gidx + 1) * LANES]
            outs.append(ag * cosv
                        + pltpu.roll(ag, LANES - ROPE_DIM // 2, 1) * sa
                        + pltpu.roll(ag, ROPE_DIM // 2, 1) * sb)
        return outs[0] if len(outs) == 1 else jnp.concatenate(outs, axis=1)

    q = rope(proj(OFF_Q, ATTN_WIDTH)) * QK_SCALE
    qT_ref[0] = q.T.astype(BF16)
    k_ref[0] = rope(proj(OFF_K, ATTN_WIDTH)).astype(BF16)
    v_t = proj(OFF_V, ATTN_WIDTH).T.astype(BF16)
    for t in range(tm // TK):
        vT_ref[0, t] = v_t[:, t * TK:(t + 1) * TK]
    iqT_ref[0] = rope(proj(OFF_IQ, IQ_WIDTH)).T.astype(BF16)
    ikw = rope(proj(OFF_IKW, LANES))
    ikw_ref[0] = ikw.astype(BF16)
    ikwT_ref[0] = ikw.T

    z = proj(OFF_CC, CONV_WIDTH) * proj(OFF_CH, CONV_WIDTH)

    @pl.when(i % tiles_per_seq == 0)
    def _():
        zbuf[0:SUBLANES, :] = jnp.zeros((SUBLANES, CONV_WIDTH), F32)

    @pl.when(i % tiles_per_seq != 0)
    def _():
        zbuf[0:SUBLANES, :] = zbuf[tm:tm + SUBLANES, :]

    zbuf[SUBLANES:SUBLANES + tm, :] = z
    z1 = zbuf[SUBLANES - 1:SUBLANES - 1 + tm, :]
    z2 = zbuf[SUBLANES - 2:SUBLANES - 2 + tm, :]
    conv = z2 * cw_ref[0:1, :] + z1 * cw_ref[1:2, :] + z * cw_ref[2:3, :]
    uc_ref[...] = (proj(OFF_CB, CONV_WIDTH) * conv).astype(BF16)

    for c in range(4):
        gc = proj(OFF_G + c * 512, 512)
        sg_ref[:, c * 512:(c + 1) * 512] = jax.nn.sigmoid(gc).astype(BF16)


def _proj_call(x2d, gain, wt, cosv, sa, sb, cw, *, batch, seq):
    m = x2d.shape[0]
    tm = TM_PROJ
    nt = seq // tm
    kt_per_step = tm // TK
    const = lambda i: (0, 0)
    out_shape = (
        jax.ShapeDtypeStruct((batch, ATTN_WIDTH, seq), BF16),
        jax.ShapeDtypeStruct((batch, seq, ATTN_WIDTH), BF16),
        jax.ShapeDtypeStruct((batch, seq // TK, ATTN_WIDTH, TK), BF16),
        jax.ShapeDtypeStruct((batch, IQ_WIDTH, seq), BF16),
        jax.ShapeDtypeStruct((batch, seq, LANES), BF16),
        jax.ShapeDtypeStruct((batch, LANES, seq), F32),
        jax.ShapeDtypeStruct((m, CONV_WIDTH), BF16),
        jax.ShapeDtypeStruct((m, 2 * D_MODEL), BF16),
    )
    in_specs = [
        pl.BlockSpec((tm, D_MODEL), lambda i: (i, 0)),
        pl.BlockSpec((1, D_MODEL), const),
        pl.BlockSpec((PROJ_COLS, D_MODEL), const, pipeline_mode=pl.Buffered(1)),
        pl.BlockSpec((tm, LANES), lambda i: (i % nt, 0)),
        pl.BlockSpec((tm, LANES), lambda i: (i % nt, 0)),
        pl.BlockSpec((tm, LANES), lambda i: (i % nt, 0)),
        pl.BlockSpec((CONV_K, CONV_WIDTH), const),
    ]
    out_specs = (
        pl.BlockSpec((1, ATTN_WIDTH, tm), lambda i: (i // nt, 0, i % nt)),
        pl.BlockSpec((1, tm, ATTN_WIDTH), lambda i: (i // nt, i % nt, 0)),
        pl.BlockSpec((1, kt_per_step, ATTN_WIDTH, TK), lambda i: (i // nt, i % nt, 0, 0)),
        pl.BlockSpec((1, IQ_WIDTH, tm), lambda i: (i // nt, 0, i % nt)),
        pl.BlockSpec((1, tm, LANES), lambda i: (i // nt, i % nt, 0)),
        pl.BlockSpec((1, LANES, tm), lambda i: (i // nt, 0, i % nt)),
        pl.BlockSpec((tm, CONV_WIDTH), lambda i: (i, 0)),
        pl.BlockSpec((tm, 2 * D_MODEL), lambda i: (i, 0)),
    )
    return pl.pallas_call(
        functools.partial(_proj_kernel, tiles_per_seq=nt),
        grid=(m // tm,),
        in_specs=in_specs,
        out_specs=out_specs,
        out_shape=out_shape,
        scratch_shapes=[pltpu.VMEM((tm + SUBLANES, CONV_WIDTH), F32)],
        compiler_params=pltpu.CompilerParams(
            dimension_semantics=("arbitrary",), vmem_limit_bytes=VMEM_LIMIT),
        name="proj",
    )(x2d, gain, wt, cosv, sa, sb, cw)


def _attn_kernel(qT_ref, k_ref, vT_ref, iqT_ref, ikw_ref, ikwT_ref, o_ref,
                 s_ref, bias_ref, sb_ref, iq_ref, qz_ref, acc_ref, m_ref, a_ref, l_ref):
    j = pl.program_id(1)
    nkt = j + 1
    grp = TK // SUBLANES

    def tile_start(kt):
        return pl.multiple_of(kt * TK, TK)

    def vreduce(x, op):
        parts = [x[g * SUBLANES:(g + 1) * SUBLANES, :] for g in range(grp)]
        while len(parts) > 1:
            parts = [op(parts[i], parts[i + 1]) for i in range(0, len(parts), 2)]
        return parts[0]

    w_t = ikwT_ref[0, IW_LANE:IW_LANE + SUBLANES, :] * IDX_SCALE
    tpos = j * TQ + lax.broadcasted_iota(jnp.int32, (1, TQ), 1)
    limit = (tpos // CHUNK + 1) * CHUNK
    nadm = limit.astype(F32)
    kvec = jnp.minimum(limit, TOPK_MAX).astype(F32)

    zpad = jnp.zeros((LANES - IDX_DIM, TQ), BF16)
    iq_ref[...] = jnp.concatenate(
        [jnp.concatenate([iqT_ref[0, h * IDX_DIM:(h + 1) * IDX_DIM, :], zpad], axis=0)
         for h in range(IDX_HEADS)], axis=1)

    def idx_logits(kt):
        return jnp.dot(ikw_ref[0, pl.ds(tile_start(kt), TK), :], iq_ref[...],
                       preferred_element_type=F32)

    def idx_scores(kt, lg, stats):
        amax, cgt, cge, ssum, sabs = stats
        sc = jnp.maximum(lg[:, 0:TQ], 0.0) * w_t[0:1, :]
        for h in range(1, IDX_HEADS):
            sc = sc + jnp.maximum(lg[:, h * TQ:(h + 1) * TQ], 0.0) * w_t[h:h + 1, :]
        k0 = tile_start(kt)
        adm = (k0 + lax.broadcasted_iota(jnp.int32, (TK, 1), 0)) < limit
        sm = jnp.where(adm, sc, -jnp.inf)
        s_ref[pl.ds(k0, TK), :] = sm
        sz = jnp.where(adm, sc, 0.0)
        az = jnp.abs(sz)
        return (jnp.maximum(amax, vreduce(az, jnp.maximum)),
                cgt + vreduce(jnp.where(sm > 0.0, 1.0, 0.0), jnp.add),
                cge + vreduce(jnp.where(sm >= 0.0, 1.0, 0.0), jnp.add),
                ssum + vreduce(sz, jnp.add),
                sabs + vreduce(az, jnp.add))

    def p1_tiles(first, count, stats):
        for t in range(count):
            stats = idx_scores(first + t, idx_logits(first + t), stats)
        return stats

    z8 = jnp.zeros((SUBLANES, TQ), F32)
    quads = nkt // 4
    stats = lax.fori_loop(0, quads, lambda i, st: p1_tiles(4 * i, 4, st), (z8, z8, z8, z8, z8))
    rest = nkt - 4 * quads
    stats = lax.cond(rest >= 2, lambda st: p1_tiles(4 * quads, 2, st), lambda st: st, stats)
    stats = lax.cond(rest % 2 == 1, lambda st: p1_tiles(nkt - 1, 1, st), lambda st: st, stats)
    maxabs = stats[0].max(axis=0, keepdims=True)
    cg0, ce0, ssum, sabs = [x.sum(axis=0, keepdims=True) for x in stats[1:]]

    def count_gt(bound):
        b8 = jnp.broadcast_to(bound, (SUBLANES, TQ))

        def body(kt, accs):
            blk = s_ref[pl.ds(tile_start(kt), TK), :]
            accs = list(accs)
            for g in range(grp):
                part = blk[g * SUBLANES:(g + 1) * SUBLANES, :]
                accs[g % 4] = jnp.where(part > b8, accs[g % 4] + 1.0, accs[g % 4])
            return tuple(accs)

        z8 = jnp.zeros((SUBLANES, TQ), F32)
        def tiles(first, count, a):
            for t in range(count):
                a = body(first + t, a)
            return a

        accs = lax.fori_loop(0, quads, lambda i, a: tiles(4 * i, 4, a), (z8, z8, z8, z8))
        accs = lax.cond(rest >= 2, lambda a: tiles(4 * quads, 2, a), lambda a: a, accs)
        accs = lax.cond(rest % 2 == 1, lambda a: body(nkt - 1, a), lambda a: a, accs)
        a = (accs[0] + accs[1]) + (accs[2] + accs[3])
        return a.sum(axis=0, keepdims=True)

    one = jnp.ones((1, TQ), F32)
    zero = jnp.zeros((1, TQ), F32)
    tgt = kvec - 0.5
    allsel = nadm <= kvec
    tie0 = jnp.logical_and(cg0 < kvec, ce0 >= kvec)
    pos = cg0 > kvec
    done0 = jnp.logical_or(allsel, jnp.logical_or(tie0, cg0 == kvec))
    theta0 = jnp.where(allsel, -jnp.inf, 0.0)
    tie_thr0 = jnp.where(jnp.logical_and(tie0, jnp.logical_not(allsel)), 0.0, jnp.inf)
    init = (
        jnp.int32(0), jnp.int32(1),
        jnp.where(pos, 0.0, -2.0 * maxabs), jnp.where(pos, maxabs, 0.0),
        jnp.where(pos, cg0, nadm) - tgt, jnp.where(pos, 0.0, ce0) - tgt,
        jnp.where(pos, 0.0, cg0), zero,
        jnp.where(done0, one, zero), theta0, tie_thr0, cg0,
    )

    def one_pass(st, hint=None):
        lo, hi, flo, fhi, chi, side, done, theta, tie_thr, tie_cgt = st
        interp = lo + (hi - lo) * (flo / (flo - fhi))
        bis = 0.5 * lo + 0.5 * hi
        mid = jnp.where(jnp.logical_and(interp > lo, interp < hi), interp, bis)
        if hint is not None:
            mid = jnp.where(jnp.logical_and(hint > lo, hint < hi), hint, mid)
        inside = jnp.logical_and(mid > lo, mid < hi)
        c = count_gt(mid)
        active = done < 0.5
        live = jnp.logical_and(active, inside)
        hit = jnp.logical_and(live, c == kvec)
        stuck = jnp.logical_and(active, jnp.logical_not(inside))
        up = jnp.logical_and(live, c > kvec)
        dn = jnp.logical_and(live, c < kvec)
        theta = jnp.where(hit, mid, jnp.where(stuck, hi, theta))
        tie_thr = jnp.where(stuck, hi, tie_thr)
        tie_cgt = jnp.where(stuck, chi, tie_cgt)
        done = jnp.where(jnp.logical_or(hit, stuck), 1.0, done)
        fhi_n = jnp.where(dn, c - tgt, jnp.where(jnp.logical_and(up, side > 0.5), fhi * 0.5, fhi))
        flo_n = jnp.where(up, c - tgt, jnp.where(jnp.logical_and(dn, side < -0.5), flo * 0.5, flo))
        return (jnp.where(up, mid, lo), jnp.where(dn, mid, hi), flo_n, fhi_n,
                jnp.where(dn, c, chi), jnp.where(up, 1.0, jnp.where(dn, -1.0, side)),
                done, theta, tie_thr, tie_cgt)

    npos = jnp.maximum(cg0, 1.0)
    nneg = jnp.maximum(nadm - ce0, 1.0)
    mean_pos = 0.5 * (sabs + ssum) / npos
    mean_neg = 0.5 * (sabs - ssum) / nneg
    frac_neg = jnp.minimum(jnp.maximum((kvec - ce0) / nneg, 1e-6), 1.0 - 1e-6)
    guess = jnp.where(pos, mean_pos * jnp.log(npos / kvec), mean_neg * jnp.log(1.0 - frac_neg))
    st = one_pass(init[2:], hint=guess)
    widen = jnp.where(pos, GUESS_WIDEN, 1.0 / GUESS_WIDEN)
    st = one_pass(st, hint=jnp.where(st[5] > 0.5, st[0] * widen, st[1] / widen))
    st = lax.fori_loop(0, PASSES_UNCHECKED - 2, lambda _, s: one_pass(s), st)

    def pending_of(s):
        return jnp.sum(jnp.where(s[6] < 0.5, 1.0, 0.0)).astype(jnp.int32)

    def check_body(carry):
        s = carry[2:]
        for _ in range(PASSES_PER_CHECK):
            s = one_pass(s)
        return (carry[0] + 1, pending_of(s)) + s

    def check_cond(carry):
        return jnp.logical_and(carry[1] > 0, carry[0] < MAX_CHECKS)

    final = lax.while_loop(check_cond, check_body, (jnp.int32(0), jnp.int32(1)) + st)
    theta, tie_thr, tie_cgt = final[9], final[10], final[11]
    tie_budget = kvec - tie_cgt

    ri = lax.broadcasted_iota(jnp.int32, (TK, TK), 0)
    ci = lax.broadcasted_iota(jnp.int32, (TK, TK), 1)
    lower = jnp.where(ci < ri, 1.0, 0.0).astype(BF16)

    def mask_tile(kt, slot, budget_left):
        blk = s_ref[pl.ds(tile_start(kt), TK), :]
        is_tie = blk == tie_thr
        rank = jnp.dot(lower, jnp.where(is_tie, 1.0, 0.0).astype(BF16), preferred_element_type=F32)
        tie_bias = jnp.where(rank < budget_left, jnp.where(is_tie, 0.0, NEG_BIAS), NEG_BIAS)
        bias_ref[slot] = jnp.where(blk > theta, 0.0, tie_bias)
        return budget_left - (rank[TK - 1:TK, :] + jnp.where(is_tie[TK - 1:TK, :], 1.0, 0.0))

    rows = lax.broadcasted_iota(jnp.int32, (LANES, TQ), 0)
    for h in range(ATTN_HEADS):
        pair = qT_ref[0, (h // 2) * LANES:(h // 2 + 1) * LANES, :]
        qz_ref[h] = jnp.where((rows // HEAD_DIM) == (h % 2), pair, jnp.zeros_like(pair))
    acc_ref[...] = jnp.zeros_like(acc_ref)
    l_ref[...] = jnp.zeros_like(l_ref)

    def scores(kt, slot, h, m_old):
        k0 = tile_start(kt)
        kp = k_ref[0, pl.ds(k0, TK), (h // 2) * LANES:(h // 2 + 1) * LANES]
        s = jnp.dot(kp, qz_ref[h], preferred_element_type=F32) + bias_ref[slot]
        sb_ref[slot, h] = s
        m_h = jnp.maximum(m_old, vreduce(s, jnp.maximum).max(axis=0, keepdims=True))
        m_ref[slot, h:h + 1, :] = m_h
        a_ref[slot, h:h + 1, :] = jnp.exp2(m_old - m_h)

    ones_rows = jnp.ones((BF16_ROWS, TK), BF16)

    def values(kt, slot, h):
        alpha = a_ref[slot, h:h + 1, :]
        p = jnp.exp2(sb_ref[slot, h] - m_ref[slot, h:h + 1, :]).astype(BF16)
        v_h = jnp.concatenate([vT_ref[0, kt, h * HEAD_DIM:(h + 1) * HEAD_DIM, :], ones_rows], axis=0)
        pv = jnp.dot(v_h, p, preferred_element_type=F32)
        l_ref[h:h + 1, :] = alpha * l_ref[h:h + 1, :] + pv[HEAD_DIM:HEAD_DIM + 1, :]
        acc_ref[h * HEAD_DIM:(h + 1) * HEAD_DIM, :] = (
            alpha * acc_ref[h * HEAD_DIM:(h + 1) * HEAD_DIM, :] + pv[0:HEAD_DIM, :])

    def step(kt, slot, budget_left):
        budget_left = mask_tile(kt + 1, 1 - slot, budget_left)
        for h in range(ATTN_HEADS):
            scores(kt + 1, 1 - slot, h, m_ref[slot, h:h + 1, :])
            values(kt, slot, h)
        return budget_left

    left = mask_tile(0, 0, tie_budget)
    for h in range(ATTN_HEADS):
        scores(0, 0, h, jnp.full((1, TQ), M_INIT, F32))

    def p3(i, budget_left):
        for t in range(P3_TILES_PER_TRIP):
            budget_left = step(P3_TILES_PER_TRIP * i + t, t % 2, budget_left)
        return budget_left

    trips = (nkt - 1) // P3_TILES_PER_TRIP
    left = lax.fori_loop(0, trips, p3, left)
    tail_start = nkt - 2 + nkt % 2
    for t in range(P3_TILES_PER_TRIP - 2):
        kt = P3_TILES_PER_TRIP * trips + t
        left = lax.cond(kt < tail_start, functools.partial(step, kt, t % 2), lambda b: b, left)

    @pl.when(nkt % 2 == 0)
    def _():
        step(nkt - 2, 0, left)
        for h in range(ATTN_HEADS):
            values(nkt - 1, 1, h)

    @pl.when(nkt % 2 == 1)
    def _():
        for h in range(ATTN_HEADS):
            values(nkt - 1, 0, h)

    l_all = l_ref[...]
    outs = []
    for h in range(ATTN_HEADS):
        outs.append(acc_ref[h * HEAD_DIM:(h + 1) * HEAD_DIM, :] / l_all[h:h + 1, :])
    o_ref[0] = jnp.concatenate(outs, axis=0).T.astype(BF16)


def _attn_call(qT, k, vT, iqT, ikw, ikwT, *, batch, seq):
    nq = seq // TQ
    nkt = seq // TK
    return pl.pallas_call(
        _attn_kernel,
        grid=(batch, nq),
        in_specs=[
            pl.BlockSpec((1, ATTN_WIDTH, TQ), lambda b, j: (b, 0, j)),
            pl.BlockSpec((1, seq, ATTN_WIDTH), lambda b, j: (b, 0, 0)),
            pl.BlockSpec((1, nkt, ATTN_WIDTH, TK), lambda b, j: (b, 0, 0, 0)),
            pl.BlockSpec((1, IQ_WIDTH, TQ), lambda b, j: (b, 0, j)),
            pl.BlockSpec((1, seq, LANES), lambda b, j: (b, 0, 0)),
            pl.BlockSpec((1, LANES, TQ), lambda b, j: (b, 0, j)),
        ],
        out_specs=pl.BlockSpec((1, TQ, ATTN_WIDTH), lambda b, j: (b, j, 0)),
        out_shape=jax.ShapeDtypeStruct((batch, seq, ATTN_WIDTH), BF16),
        scratch_shapes=[
            pltpu.VMEM((seq, TQ), F32),
            pltpu.VMEM((2, TK, TQ), F32),
            pltpu.VMEM((2, ATTN_HEADS, TK, TQ), F32),
            pltpu.VMEM((LANES, IDX_HEADS * TQ), BF16),
            pltpu.VMEM((ATTN_HEADS, LANES, TQ), BF16),
            pltpu.VMEM((ATTN_WIDTH, TQ), F32),
            pltpu.VMEM((2, ATTN_HEADS, TQ), F32),
            pltpu.VMEM((2, ATTN_HEADS, TQ), F32),
            pltpu.VMEM((ATTN_HEADS, TQ), F32),
        ],
        compiler_params=pltpu.CompilerParams(
            dimension_semantics=("arbitrary", "arbitrary"), vmem_limit_bytes=VMEM_LIMIT),
        name="dsa_attn",
    )(qT, k, vT, iqT, ikw, ikwT)


def _post_kernel(x_ref, at_ref, uc_ref, sg_ref, wao3_ref, wco3_ref, wmix3_ref, wup3_ref, wdn3_ref,
                 gm_ref, gf_ref, o_ref, *, final):
    wao_ref, wco_ref, wmix_ref, wup_ref, wdn_ref = [
        r.at[0] for r in (wao3_ref, wco3_ref, wmix3_ref, wup3_ref, wdn3_ref)]
    ya = jnp.dot(at_ref[...], wao_ref[...], preferred_element_type=F32)
    yc = jnp.dot(uc_ref[...], wco_ref[...], preferred_element_type=F32)
    merged = (sg_ref[:, 0:D_MODEL].astype(F32) * ya
              + sg_ref[:, D_MODEL:2 * D_MODEL].astype(F32) * yc)
    x1 = x_ref[...] + jnp.dot(merged.astype(BF16), wmix_ref[...], preferred_element_type=F32)
    u = _rmsnorm(x1, gm_ref[...]).astype(BF16)
    x2 = x1
    for c in range(MLP_HIDDEN // MLP_CHUNK):
        hid = jnp.dot(u, wup_ref[:, c * MLP_CHUNK:(c + 1) * MLP_CHUNK], preferred_element_type=F32)
        hid = jnp.square(jnp.maximum(hid, 0.0)).astype(BF16)
        x2 = x2 + jnp.dot(hid, wdn_ref[c * MLP_CHUNK:(c + 1) * MLP_CHUNK, :], preferred_element_type=F32)
    if final:
        x2 = _rmsnorm(x2, gf_ref[...])
    o_ref[...] = x2


def _post_call(x2d, attn, uc, sg, weights, gm, gf, *, layer, final):
    m = x2d.shape[0]
    tm = TM_POST
    const = lambda i: (0, 0)
    row = lambda i: (i, 0)
    resident = lambda shape: pl.BlockSpec((1,) + shape, lambda i: (layer, 0, 0), pipeline_mode=pl.Buffered(1))
    return pl.pallas_call(
        functools.partial(_post_kernel, final=final),
        grid=(m // tm,),
        in_specs=[
            pl.BlockSpec((tm, D_MODEL), row),
            pl.BlockSpec((tm, ATTN_WIDTH), row),
            pl.BlockSpec((tm, CONV_WIDTH), row),
            pl.BlockSpec((tm, 2 * D_MODEL), row),
            resident((ATTN_WIDTH, D_MODEL)),
            resident((CONV_WIDTH, D_MODEL)),
            resident((D_MODEL, D_MODEL)),
            resident((D_MODEL, MLP_HIDDEN)),
            resident((MLP_HIDDEN, D_MODEL)),
            pl.BlockSpec((1, D_MODEL), const),
            pl.BlockSpec((1, D_MODEL), const),
        ],
        out_specs=pl.BlockSpec((tm, D_MODEL), row),
        out_shape=jax.ShapeDtypeStruct((m, D_MODEL), F32),
        compiler_params=pltpu.CompilerParams(
            dimension_semantics=("arbitrary",), vmem_limit_bytes=VMEM_LIMIT),
        name="mixer_tail",
    )(x2d, attn, uc, sg, *weights, gm, gf)


def _rope_tables(seq):
    half = ROPE_DIM // 2
    inv = 1.0 / (ROPE_THETA ** (jnp.arange(0, ROPE_DIM, 2, dtype=F32) / ROPE_DIM))
    lane = np.arange(LANES)
    in_head = lane % HEAD_DIM
    ang = jnp.arange(seq, dtype=F32)[:, None] * inv[lane % half][None, :]
    cos, sin = jnp.cos(ang), jnp.sin(ang)
    first = jnp.asarray(in_head < half)[None, :]
    second = jnp.asarray((in_head >= half) & (in_head < ROPE_DIM))[None, :]
    return (jnp.where(first | second, cos, 1.0), jnp.where(first, -sin, 0.0), jnp.where(second, sin, 0.0))


def kernel(x, norm_mix, w_in, conv_w, w_attn_out, w_conv_out, w_mix_out, norm_mlp, w_mlp_up, w_mlp_down,
           norm_final):
    batch, seq, d = x.shape
    depth = w_in.shape[0]
    assert d == D_MODEL and seq % TM_PROJ == 0 and TM_PROJ % TK == 0 and TQ == TK
    assert seq // 4 >= TOPK_MAX
    cosv, sa, sb = _rope_tables(seq)
    h = x.reshape(batch * seq, d)
    wt16 = jnp.swapaxes(w_in, 1, 2).astype(BF16)
    tail_w = [w.astype(BF16) for w in (w_attn_out, w_conv_out, w_mix_out, w_mlp_up, w_mlp_down)]
    for l in range(depth):
        qT, k, vT, iqT, ikw, ikwT, uc, sg = _proj_call(
            h, norm_mix[l][None, :], _wprep_call(wt16, l), cosv, sa, sb, conv_w[l],
            batch=batch, seq=seq)
        attn = _attn_call(qT, k, vT, iqT, ikw, ikwT, batch=batch, seq=seq)
        h = _post_call(
            h, attn.reshape(batch * seq, ATTN_WIDTH), uc, sg, tail_w,
            norm_mlp[l][None, :], norm_final[None, :], layer=l, final=(l == depth - 1))
    return h.reshape(batch, seq, d)
```

```python
import functools

import jax
import jax.numpy as jnp
import numpy as np
from jax import lax
from jax.experimental import pallas as pl
from jax.experimental.pallas import tpu as pltpu

F32 = jnp.float32
BF16 = jnp.bfloat16

D_MODEL = 1024
CHUNK = 64
EPS = 1e-6
HEAD_DIM = 64
ATTN_WIDTH = 512
ATTN_HEADS = 8
ROPE_DIM = 16
ROPE_THETA = 500000.0
IDX_HEADS = 4
IDX_DIM = 64
IDX_SCALE = (IDX_DIM ** -0.5) * (IDX_HEADS ** -0.5)
TOPK_MAX = 256
CONV_WIDTH = 512
CONV_K = 3
MLP_HIDDEN = 4 * D_MODEL

LANES = 128
SUBLANES = 8
BF16_ROWS = 16
QK_SCALE = (HEAD_DIM ** -0.5) * float(np.log2(np.e))

WBLK = 512
OFF_Q = 0
OFF_K = OFF_Q + ATTN_WIDTH
OFF_V = OFF_K + ATTN_WIDTH
IQ_WIDTH = IDX_HEADS * IDX_DIM
OFF_IQ = OFF_V + ATTN_WIDTH
OFF_IKW = OFF_IQ + IQ_WIDTH
IW_LANE = 96
OFF_CB = OFF_IQ + WBLK
OFF_CC = OFF_CB + CONV_WIDTH
OFF_CH = OFF_CC + CONV_WIDTH
OFF_G = OFF_CH + CONV_WIDTH
PROJ_COLS = OFF_G + 2 * D_MODEL
SRC_IQ = 3 * ATTN_WIDTH
SRC_IK = SRC_IQ + IDX_HEADS * IDX_DIM
SRC_IW = SRC_IK + IDX_DIM
SRC_CB = SRC_IW + IDX_HEADS

TM_PROJ = 1024
TQ = 256
TK = 256
TM_POST = 512
MLP_CHUNK = 1024
PASSES_UNCHECKED = 8
PASSES_PER_CHECK = 3
P3_TILES_PER_TRIP = 4
MAX_CHECKS = 192
GUESS_WIDEN = 1.3
NEG_BIAS = -2e30
M_INIT = -1e30
VMEM_LIMIT = 48 * 1024 * 1024


def _rmsnorm(x, g):
    ms = jnp.mean(x * x, axis=-1, keepdims=True)
    return x * lax.rsqrt(ms + EPS) * g


def _wprep_kernel(a3_ref, b3_ref, o_ref):
    a_ref = a3_ref.at[0]
    b_ref = b3_ref.at[0]
    s = pl.program_id(0)
    n_plain = OFF_IQ // WBLK

    def zeros(n):
        return jnp.zeros((n, D_MODEL), BF16)

    @pl.when(s < n_plain)
    def _():
        o_ref[...] = a_ref[...]

    @pl.when(s == n_plain)
    def _():
        iq0 = SRC_IQ % WBLK
        ik0 = SRC_IK % WBLK
        iw0 = SRC_IW % WBLK
        rows = lax.broadcasted_iota(jnp.int32, (BF16_ROWS, D_MODEL), 0)
        iw = jnp.where(rows < IDX_HEADS, a_ref[iw0:iw0 + BF16_ROWS, :].astype(F32), 0.0).astype(BF16)
        o_ref[...] = jnp.concatenate(
            [a_ref[iq0:iq0 + IQ_WIDTH, :], a_ref[ik0:ik0 + IDX_DIM, :], zeros(IW_LANE - IDX_DIM), iw,
             zeros(WBLK - IQ_WIDTH - IW_LANE - BF16_ROWS)], axis=0)

    @pl.when(s > n_plain)
    def _():
        shift = SRC_CB % WBLK
        for c in range(D_MODEL // (2 * LANES)):
            cols = slice(c * 2 * LANES, (c + 1) * 2 * LANES)
            both = jnp.concatenate([a_ref[:, cols].astype(F32), b_ref[:, cols].astype(F32)], axis=0)
            o_ref[:, cols] = pltpu.roll(both, 2 * WBLK - shift, 0)[0:WBLK, :].astype(BF16)


def _wprep_call(wt16, layer):
    n_plain = OFF_IQ // WBLK
    blk0 = SRC_CB // WBLK
    assert SRC_IQ // WBLK == SRC_IK // WBLK == SRC_IW // WBLK == blk0 == n_plain
    assert SRC_IW % WBLK % BF16_ROWS == 0 and (SRC_IW % WBLK) + BF16_ROWS <= WBLK

    def blk_a(s):
        return jnp.where(s < n_plain, s, jnp.where(s == n_plain, blk0, s - (n_plain + 1) + blk0))

    def blk_b(s):
        return jnp.where(s <= n_plain, 0, s - (n_plain + 1) + blk0 + 1)

    return pl.pallas_call(
        _wprep_kernel,
        grid=(PROJ_COLS // WBLK,),
        in_specs=[pl.BlockSpec((1, WBLK, D_MODEL), lambda s: (layer, blk_a(s), 0)),
                  pl.BlockSpec((1, WBLK, D_MODEL), lambda s: (layer, blk_b(s), 0))],
        out_specs=pl.BlockSpec((WBLK, D_MODEL), lambda s: (s, 0)),
        out_shape=jax.ShapeDtypeStruct((PROJ_COLS, D_MODEL), BF16),
        compiler_params=pltpu.CompilerParams(
            dimension_semantics=("arbitrary",), vmem_limit_bytes=VMEM_LIMIT),
        name="w_arrange",
    )(wt16, wt16)


def _proj_kernel(x_ref, g_ref, wt_ref, cos_ref, sa_ref, sb_ref, cw_ref,
                 qT_ref, k_ref, vT_ref, iqT_ref, ikw_ref, ikwT_ref, uc_ref, sg_ref,
                 zbuf, *, tiles_per_seq):
    i = pl.program_id(0)
    tm = x_ref.shape[0]
    u = _rmsnorm(x_ref[...], g_ref[...]).astype(BF16)
    cosv = cos_ref[...]
    sa = sa_ref[...]
    sb = sb_ref[...]

    def proj(c0, n):
        return lax.dot_general(u, wt_ref[c0:c0 + n, :], (((1,), (1,)), ((), ())),
                               preferred_element_type=F32)

    def rope(a):
        outs = []
        for gidx in range(a.shape[1] // LANES):
            ag = a[:, gidx * LANES:(gidx + 1) * LANES]
            outs.append(ag * cosv
                        + pltpu.roll(ag, LANES - ROPE_DIM // 2, 1) * sa
                        + pltpu.roll(ag, ROPE_DIM // 2, 1) * sb)
        return outs[0] if len(outs) == 1 else jnp.concatenate(outs, axis=1)

    q = rope(proj(OFF_Q, ATTN_WIDTH)) * QK_SCALE
    qT_ref[0] = q.T.astype(BF16)
    k_ref[0] = rope(proj(OFF_K, ATTN_WIDTH)).astype(BF16)
    v_t = proj(OFF_V, ATTN_WIDTH).T.astype(BF16)
    for t in range(tm // TK):
        vT_ref[0, t] = v_t[:, t * TK:(t + 1) * TK]
    iqT_ref[0] = rope(proj(OFF_IQ, IQ_WIDTH)).T.astype(BF16)
    ikw = rope(proj(OFF_IKW, LANES))
    ikw_ref[0] = ikw.astype(BF16)
    ikwT_ref[0] = ikw.T

    z = proj(OFF_CC, CONV_WIDTH) * proj(OFF_CH, CONV_WIDTH)

    @pl.when(i % tiles_per_seq == 0)
    def _():
        zbuf[0:SUBLANES, :] = jnp.zeros((SUBLANES, CONV_WIDTH), F32)

    @pl.when(i % tiles_per_seq != 0)
    def _():
        zbuf[0:SUBLANES, :] = zbuf[tm:tm + SUBLANES, :]

    zbuf[SUBLANES:SUBLANES + tm, :] = z
    z1 = zbuf[SUBLANES - 1:SUBLANES - 1 + tm, :]
    z2 = zbuf[SUBLANES - 2:SUBLANES - 2 + tm, :]
    conv = z2 * cw_ref[0:1, :] + z1 * cw_ref[1:2, :] + z * cw_ref[2:3, :]
    uc_ref[...] = (proj(OFF_CB, CONV_WIDTH) * conv).astype(BF16)

    for c in range(4):
        gc = proj(OFF_G + c * 512, 512)
        sg_ref[:, c * 512:(c + 1) * 512] = jax.nn.sigmoid(gc).astype(BF16)


def _proj_call(x2d, gain, wt, cosv, sa, sb, cw, *, batch, seq):
    m = x2d.shape[0]
    tm = TM_PROJ
    nt = seq // tm
    kt_per_step = tm // TK
    const = lambda i: (0, 0)
    out_shape = (
        jax.ShapeDtypeStruct((batch, ATTN_WIDTH, seq), BF16),
        jax.ShapeDtypeStruct((batch, seq, ATTN_WIDTH), BF16),
        jax.ShapeDtypeStruct((batch, seq // TK, ATTN_WIDTH, TK), BF16),
        jax.ShapeDtypeStruct((batch, IQ_WIDTH, seq), BF16),
        jax.ShapeDtypeStruct((batch, seq, LANES), BF16),
        jax.ShapeDtypeStruct((batch, LANES, seq), F32),
        jax.ShapeDtypeStruct((m, CONV_WIDTH), BF16),
        jax.ShapeDtypeStruct((m, 2 * D_MODEL), BF16),
    )
    in_specs = [
        pl.BlockSpec((tm, D_MODEL), lambda i: (i, 0)),
        pl.BlockSpec((1, D_MODEL), const),
        pl.BlockSpec((PROJ_COLS, D_MODEL), const, pipeline_mode=pl.Buffered(1)),
        pl.BlockSpec((tm, LANES), lambda i: (i % nt, 0)),
        pl.BlockSpec((tm, LANES), lambda i: (i % nt, 0)),
        pl.BlockSpec((tm, LANES), lambda i: (i % nt, 0)),
        pl.BlockSpec((CONV_K, CONV_WIDTH), const),
    ]
    out_specs = (
        pl.BlockSpec((1, ATTN_WIDTH, tm), lambda i: (i // nt, 0, i % nt)),
        pl.BlockSpec((1, tm, ATTN_WIDTH), lambda i: (i // nt, i % nt, 0)),
        pl.BlockSpec((1, kt_per_step, ATTN_WIDTH, TK), lambda i: (i // nt, i % nt, 0, 0)),
        pl.BlockSpec((1, IQ_WIDTH, tm), lambda i: (i // nt, 0, i % nt)),
        pl.BlockSpec((1, tm, LANES), lambda i: (i // nt, i % nt, 0)),
        pl.BlockSpec((1, LANES, tm), lambda i: (i // nt, 0, i % nt)),
        pl.BlockSpec((tm, CONV_WIDTH), lambda i: (i, 0)),
        pl.BlockSpec((tm, 2 * D_MODEL), lambda i: (i, 0)),
    )
    return pl.pallas_call(
        functools.partial(_proj_kernel, tiles_per_seq=nt),
        grid=(m // tm,),
        in_specs=in_specs,
        out_specs=out_specs,
        out_shape=out_shape,
        scratch_shapes=[pltpu.VMEM((tm + SUBLANES, CONV_WIDTH), F32)],
        compiler_params=pltpu.CompilerParams(
            dimension_semantics=("arbitrary",), vmem_limit_bytes=VMEM_LIMIT),
        name="proj",
    )(x2d, gain, wt, cosv, sa, sb, cw)


def _attn_kernel(qT_ref, k_ref, vT_ref, iqT_ref, ikw_ref, ikwT_ref, o_ref,
                 s_ref, bias_ref, sb_ref, iq_ref, qz_ref, acc_ref, m_ref, a_ref, l_ref):
    j = pl.program_id(1)
    nkt = j + 1
    grp = TK // SUBLANES

    def tile_start(kt):
        return pl.multiple_of(kt * TK, TK)

    def vreduce(x, op):
        parts = [x[g * SUBLANES:(g + 1) * SUBLANES, :] for g in range(grp)]
        while len(parts) > 1:
            parts = [op(parts[i], parts[i + 1]) for i in range(0, len(parts), 2)]
        return parts[0]

    w_t = ikwT_ref[0, IW_LANE:IW_LANE + SUBLANES, :] * IDX_SCALE
    tpos = j * TQ + lax.broadcasted_iota(jnp.int32, (1, TQ), 1)
    limit = (tpos // CHUNK + 1) * CHUNK
    nadm = limit.astype(F32)
    kvec = jnp.minimum(limit, TOPK_MAX).astype(F32)

    zpad = jnp.zeros((LANES - IDX_DIM, TQ), BF16)
    iq_ref[...] = jnp.concatenate(
        [jnp.concatenate([iqT_ref[0, h * IDX_DIM:(h + 1) * IDX_DIM, :], zpad], axis=0)
         for h in range(IDX_HEADS)], axis=1)

    def idx_logits(kt):
        return jnp.dot(ikw_ref[0, pl.ds(tile_start(kt), TK), :], iq_ref[...],
                       preferred_element_type=F32)

    def idx_scores(kt, lg, stats):
        amax, cgt, cge, ssum, sabs = stats
        sc = jnp.maximum(lg[:, 0:TQ], 0.0) * w_t[0:1, :]
        for h in range(1, IDX_HEADS):
            sc = sc + jnp.maximum(lg[:, h * TQ:(h + 1) * TQ], 0.0) * w_t[h:h + 1, :]
        k0 = tile_start(kt)
        adm = (k0 + lax.broadcasted_iota(jnp.int32, (TK, 1), 0)) < limit
        sm = jnp.where(adm, sc, -jnp.inf)
        s_ref[pl.ds(k0, TK), :] = sm
        sz = jnp.where(adm, sc, 0.0)
        az = jnp.abs(sz)
        return (jnp.maximum(amax, vreduce(az, jnp.maximum)),
                cgt + vreduce(jnp.where(sm > 0.0, 1.0, 0.0), jnp.add),
                cge + vreduce(jnp.where(sm >= 0.0, 1.0, 0.0), jnp.add),
                ssum + vreduce(sz, jnp.add),
                sabs + vreduce(az, jnp.add))

    def p1_tiles(first, count, stats):
        for t in range(count):
            stats = idx_scores(first + t, idx_logits(first + t), stats)
        return stats

    z8 = jnp.zeros((SUBLANES, TQ), F32)
    quads = nkt // 4
    stats = lax.fori_loop(0, quads, lambda i, st: p1_tiles(4 * i, 4, st), (z8, z8, z8, z8, z8))
    rest = nkt - 4 * quads
    stats = lax.cond(rest >= 2, lambda st: p1_tiles(4 * quads, 2, st), lambda st: st, stats)
    stats = lax.cond(rest % 2 == 1, lambda st: p1_tiles(nkt - 1, 1, st), lambda st: st, stats)
    maxabs = stats[0].max(axis=0, keepdims=True)
    cg0, ce0, ssum, sabs = [x.sum(axis=0, keepdims=True) for x in stats[1:]]

    def count_gt(bound):
        b8 = jnp.broadcast_to(bound, (SUBLANES, TQ))

        def body(kt, accs):
            blk = s_ref[pl.ds(tile_start(kt), TK), :]
            accs = list(accs)
            for g in range(grp):
                part = blk[g * SUBLANES:(g + 1) * SUBLANES, :]
                accs[g % 4] = jnp.where(part > b8, accs[g % 4] + 1.0, accs[g % 4])
            return tuple(accs)

        z8 = jnp.zeros((SUBLANES, TQ), F32)
        def tiles(first, count, a):
            for t in range(count):
                a = body(first + t, a)
            return a

        accs = lax.fori_loop(0, quads, lambda i, a: tiles(4 * i, 4, a), (z8, z8, z8, z8))
        accs = lax.cond(rest >= 2, lambda a: tiles(4 * quads, 2, a), lambda a: a, accs)
        accs = lax.cond(rest % 2 == 1, lambda a: body(nkt - 1, a), lambda a: a, accs)
        a = (accs[0] + accs[1]) + (accs[2] + accs[3])
        return a.sum(axis=0, keepdims=True)

    one = jnp.ones((1, TQ), F32)
    zero = jnp.zeros((1, TQ), F32)
    tgt = kvec - 0.5
    allsel = nadm <= kvec
    tie0 = jnp.logical_and(cg0 < kvec, ce0 >= kvec)
    pos = cg0 > kvec
    done0 = jnp.logical_or(allsel, jnp.logical_or(tie0, cg0 == kvec))
    theta0 = jnp.where(allsel, -jnp.inf, 0.0)
    tie_thr0 = jnp.where(jnp.logical_and(tie0, jnp.logical_not(allsel)), 0.0, jnp.inf)
    init = (
        jnp.int32(0), jnp.int32(1),
        jnp.where(pos, 0.0, -2.0 * maxabs), jnp.where(pos, maxabs, 0.0),
        jnp.where(pos, cg0, nadm) - tgt, jnp.where(pos, 0.0, ce0) - tgt,
        jnp.where(pos, 0.0, cg0), zero,
        jnp.where(done0, one, zero), theta0, tie_thr0, cg0,
    )

    def one_pass(st, hint=None):
        lo, hi, flo, fhi, chi, side, done, theta, tie_thr, tie_cgt = st
        interp = lo + (hi - lo) * (flo / (flo - fhi))
        bis = 0.5 * lo + 0.5 * hi
        mid = jnp.where(jnp.logical_and(interp > lo, interp < hi), interp, bis)
        if hint is not None:
            mid = jnp.where(jnp.logical_and(hint > lo, hint < hi), hint, mid)
        inside = jnp.logical_and(mid > lo, mid < hi)
        c = count_gt(mid)
        active = done < 0.5
        live = jnp.logical_and(active, inside)
        hit = jnp.logical_and(live, c == kvec)
        stuck = jnp.logical_and(active, jnp.logical_not(inside))
        up = jnp.logical_and(live, c > kvec)
        dn = jnp.logical_and(live, c < kvec)
        theta = jnp.where(hit, mid, jnp.where(stuck, hi, theta))
        tie_thr = jnp.where(stuck, hi, tie_thr)
        tie_cgt = jnp.where(stuck, chi, tie_cgt)
        done = jnp.where(jnp.logical_or(hit, stuck), 1.0, done)
        fhi_n = jnp.where(dn, c - tgt, jnp.where(jnp.logical_and(up, side > 0.5), fhi * 0.5, fhi))
        flo_n = jnp.where(up, c - tgt, jnp.where(jnp.logical_and(dn, side < -0.5), flo * 0.5, flo))
        return (jnp.where(up, mid, lo), jnp.where(dn, mid, hi), flo_n, fhi_n,
                jnp.where(dn, c, chi), jnp.where(up, 1.0, jnp.where(dn, -1.0, side)),
                done, theta, tie_thr, tie_cgt)

    npos = jnp.maximum(cg0, 1.0)
    nneg = jnp.maximum(nadm - ce0, 1.0)
    mean_pos = 0.5 * (sabs + ssum) / npos
    mean_neg = 0.5 * (sabs - ssum) / nneg
    frac_neg = jnp.minimum(jnp.maximum((kvec - ce0) / nneg, 1e-6), 1.0 - 1e-6)
    guess = jnp.where(pos, mean_pos * jnp.log(npos / kvec), mean_neg * jnp.log(1.0 - frac_neg))
    st = one_pass(init[2:], hint=guess)
    widen = jnp.where(pos, GUESS_WIDEN, 1.0 / GUESS_WIDEN)
    st = one_pass(st, hint=jnp.where(st[5] > 0.5, st[0] * widen, st[1] / widen))
    st = lax.fori_loop(0, PASSES_UNCHECKED - 2, lambda _, s: one_pass(s), st)

    def pending_of(s):
        return jnp.sum(jnp.where(s[6] < 0.5, 1.0, 0.0)).astype(jnp.int32)

    def check_body(carry):
        s = carry[2:]
        for _ in range(PASSES_PER_CHECK):
            s = one_pass(s)
        return (carry[0] + 1, pending_of(s)) + s

    def check_cond(carry):
        return jnp.logical_and(carry[1] > 0, carry[0] < MAX_CHECKS)

    final = lax.while_loop(check_cond, check_body, (jnp.int32(0), jnp.int32(1)) + st)
    theta, tie_thr, tie_cgt = final[9], final[10], final[11]
    tie_budget = kvec - tie_cgt

    ri = lax.broadcasted_iota(jnp.int32, (TK, TK), 0)
    ci = lax.broadcasted_iota(jnp.int32, (TK, TK), 1)
    lower = jnp.where(ci < ri, 1.0, 0.0).astype(BF16)

    def mask_tile(kt, slot, budget_left):
        blk = s_ref[pl.ds(tile_start(kt), TK), :]
        is_tie = blk == tie_thr
        rank = jnp.dot(lower, jnp.where(is_tie, 1.0, 0.0).astype(BF16), preferred_element_type=F32)
        tie_bias = jnp.where(rank < budget_left, jnp.where(is_tie, 0.0, NEG_BIAS), NEG_BIAS)
        bias_ref[slot] = jnp.where(blk > theta, 0.0, tie_bias)
        return budget_left - (rank[TK - 1:TK, :] + jnp.where(is_tie[TK - 1:TK, :], 1.0, 0.0))

    rows = lax.broadcasted_iota(jnp.int32, (LANES, TQ), 0)
    for h in range(ATTN_HEADS):
        pair = qT_ref[0, (h // 2) * LANES:(h // 2 + 1) * LANES, :]
        qz_ref[h] = jnp.where((rows // HEAD_DIM) == (h % 2), pair, jnp.zeros_like(pair))
    acc_ref[...] = jnp.zeros_like(acc_ref)
    l_ref[...] = jnp.zeros_like(l_ref)

    def scores(kt, slot, h, m_old):
        k0 = tile_start(kt)
        kp = k_ref[0, pl.ds(k0, TK), (h // 2) * LANES:(h // 2 + 1) * LANES]
        s = jnp.dot(kp, qz_ref[h], preferred_element_type=F32) + bias_ref[slot]
        sb_ref[slot, h] = s
        m_h = jnp.maximum(m_old, vreduce(s, jnp.maximum).max(axis=0, keepdims=True))
        m_ref[slot, h:h + 1, :] = m_h
        a_ref[slot, h:h + 1, :] = jnp.exp2(m_old - m_h)

    ones_rows = jnp.ones((BF16_ROWS, TK), BF16)

    def values(kt, slot, h):
        alpha = a_ref[slot, h:h + 1, :]
        p = jnp.exp2(sb_ref[slot, h] - m_ref[slot, h:h + 1, :]).astype(BF16)
        v_h = jnp.concatenate([vT_ref[0, kt, h * HEAD_DIM:(h + 1) * HEAD_DIM, :], ones_rows], axis=0)
        pv = jnp.dot(v_h, p, preferred_element_type=F32)
        l_ref[h:h + 1, :] = alpha * l_ref[h:h + 1, :] + pv[HEAD_DIM:HEAD_DIM + 1, :]
        acc_ref[h * HEAD_DIM:(h + 1) * HEAD_DIM, :] = (
            alpha * acc_ref[h * HEAD_DIM:(h + 1) * HEAD_DIM, :] + pv[0:HEAD_DIM, :])

    def step(kt, slot, budget_left):
        budget_left = mask_tile(kt + 1, 1 - slot, budget_left)
        for h in range(ATTN_HEADS):
            scores(kt + 1, 1 - slot, h, m_ref[slot, h:h + 1, :])
            values(kt, slot, h)
        return budget_left

    left = mask_tile(0, 0, tie_budget)
    for h in range(ATTN_HEADS):
        scores(0, 0, h, jnp.full((1, TQ), M_INIT, F32))

    def p3(i, budget_left):
        for t in range(P3_TILES_PER_TRIP):
            budget_left = step(P3_TILES_PER_TRIP * i + t, t % 2, budget_left)
        return budget_left

    trips = (nkt - 1) // P3_TILES_PER_TRIP
    left = lax.fori_loop(0, trips, p3, left)
    tail_start = nkt - 2 + nkt % 2
    for t in range(P3_TILES_PER_TRIP - 2):
        kt = P3_TILES_PER_TRIP * trips + t
        left = lax.cond(kt < tail_start, functools.partial(step, kt, t % 2), lambda b: b, left)

    @pl.when(nkt % 2 == 0)
    def _():
        step(nkt - 2, 0, left)
        for h in range(ATTN_HEADS):
            values(nkt - 1, 1, h)

    @pl.when(nkt % 2 == 1)
    def _():
        for h in range(ATTN_HEADS):
            values(nkt - 1, 0, h)

    l_all = l_ref[...]
    outs = []
    for h in range(ATTN_HEADS):
        outs.append(acc_ref[h * HEAD_DIM:(h + 1) * HEAD_DIM, :] / l_all[h:h + 1, :])
    o_ref[0] = jnp.concatenate(outs, axis=0).astype(BF16)


def _attn_call(qT, k, vT, iqT, ikw, ikwT, *, batch, seq):
    nq = seq // TQ
    nkt = seq // TK
    return pl.pallas_call(
        _attn_kernel,
        grid=(batch, nq),
        in_specs=[
            pl.BlockSpec((1, ATTN_WIDTH, TQ), lambda b, j: (b, 0, j)),
            pl.BlockSpec((1, seq, ATTN_WIDTH), lambda b, j: (b, 0, 0)),
            pl.BlockSpec((1, nkt, ATTN_WIDTH, TK), lambda b, j: (b, 0, 0, 0)),
            pl.BlockSpec((1, IQ_WIDTH, TQ), lambda b, j: (b, 0, j)),
            pl.BlockSpec((1, seq, LANES), lambda b, j: (b, 0, 0)),
            pl.BlockSpec((1, LANES, TQ), lambda b, j: (b, 0, j)),
        ],
        out_specs=pl.BlockSpec((1, ATTN_WIDTH, TQ), lambda b, j: (b, 0, j)),
        out_shape=jax.ShapeDtypeStruct((batch, ATTN_WIDTH, seq), BF16),
        scratch_shapes=[
            pltpu.VMEM((seq, TQ), F32),
            pltpu.VMEM((2, TK, TQ), F32),
            pltpu.VMEM((2, ATTN_HEADS, TK, TQ), F32),
            pltpu.VMEM((LANES, IDX_HEADS * TQ), BF16),
            pltpu.VMEM((ATTN_HEADS, LANES, TQ), BF16),
            pltpu.VMEM((ATTN_WIDTH, TQ), F32),
            pltpu.VMEM((2, ATTN_HEADS, TQ), F32),
            pltpu.VMEM((2, ATTN_HEADS, TQ), F32),
            pltpu.VMEM((ATTN_HEADS, TQ), F32),
        ],
        compiler_params=pltpu.CompilerParams(
            dimension_semantics=("arbitrary", "arbitrary"), vmem_limit_bytes=VMEM_LIMIT),
        name="dsa_attn",
    )(qT, k, vT, iqT, ikw, ikwT)


def _post_kernel(x_ref, at_ref, uc_ref, sg_ref, wao3_ref, wco3_ref, wmix3_ref, wup3_ref, wdn3_ref,
                 gm_ref, gf_ref, o_ref, *, final):
    wao_ref, wco_ref, wmix_ref, wup_ref, wdn_ref = [
        r.at[0] for r in (wao3_ref, wco3_ref, wmix3_ref, wup3_ref, wdn3_ref)]
    ya = lax.dot_general(at_ref[0], wao_ref[...], (((0,), (0,)), ((), ())),
                         preferred_element_type=F32)
    yc = jnp.dot(uc_ref[...], wco_ref[...], preferred_element_type=F32)
    merged = (sg_ref[:, 0:D_MODEL].astype(F32) * ya
              + sg_ref[:, D_MODEL:2 * D_MODEL].astype(F32) * yc)
    x1 = x_ref[...] + jnp.dot(merged.astype(BF16), wmix_ref[...], preferred_element_type=F32)
    u = _rmsnorm(x1, gm_ref[...]).astype(BF16)
    x2 = x1
    for c in range(MLP_HIDDEN // MLP_CHUNK):
        hid = jnp.dot(u, wup_ref[:, c * MLP_CHUNK:(c + 1) * MLP_CHUNK], preferred_element_type=F32)
        hid = jnp.square(jnp.maximum(hid, 0.0)).astype(BF16)
        x2 = x2 + jnp.dot(hid, wdn_ref[c * MLP_CHUNK:(c + 1) * MLP_CHUNK, :], preferred_element_type=F32)
    if final:
        x2 = _rmsnorm(x2, gf_ref[...])
    o_ref[...] = x2


def _post_call(x2d, attn_t, uc, sg, weights, gm, gf, *, layer, final):
    m = x2d.shape[0]
    tm = TM_POST
    nt = attn_t.shape[2] // tm
    const = lambda i: (0, 0)
    row = lambda i: (i, 0)
    resident = lambda shape: pl.BlockSpec((1,) + shape, lambda i: (layer, 0, 0), pipeline_mode=pl.Buffered(1))
    return pl.pallas_call(
        functools.partial(_post_kernel, final=final),
        grid=(m // tm,),
        in_specs=[
            pl.BlockSpec((tm, D_MODEL), row),
            pl.BlockSpec((1, ATTN_WIDTH, tm), lambda i: (i // nt, 0, i % nt)),
            pl.BlockSpec((tm, CONV_WIDTH), row),
            pl.BlockSpec((tm, 2 * D_MODEL), row),
            resident((ATTN_WIDTH, D_MODEL)),
            resident((CONV_WIDTH, D_MODEL)),
            resident((D_MODEL, D_MODEL)),
            resident((D_MODEL, MLP_HIDDEN)),
            resident((MLP_HIDDEN, D_MODEL)),
            pl.BlockSpec((1, D_MODEL), const),
            pl.BlockSpec((1, D_MODEL), const),
        ],
        out_specs=pl.BlockSpec((tm, D_MODEL), row),
        out_shape=jax.ShapeDtypeStruct((m, D_MODEL), F32),
        compiler_params=pltpu.CompilerParams(
            dimension_semantics=("arbitrary",), vmem_limit_bytes=VMEM_LIMIT),
        name="mixer_tail",
    )(x2d, attn_t, uc, sg, *weights, gm, gf)


def _rope_tables(seq):
    half = ROPE_DIM // 2
    inv = 1.0 / (ROPE_THETA ** (jnp.arange(0, ROPE_DIM, 2, dtype=F32) / ROPE_DIM))
    lane = np.arange(LANES)
    in_head = lane % HEAD_DIM
    ang = jnp.arange(seq, dtype=F32)[:, None] * inv[lane % half][None, :]
    cos, sin = jnp.cos(ang), jnp.sin(ang)
    first = jnp.asarray(in_head < half)[None, :]
    second = jnp.asarray((in_head >= half) & (in_head < ROPE_DIM))[None, :]
    return (jnp.where(first | second, cos, 1.0), jnp.where(first, -sin, 0.0), jnp.where(second, sin, 0.0))


def kernel(x, norm_mix, w_in, conv_w, w_attn_out, w_conv_out, w_mix_out, norm_mlp, w_mlp_up, w_mlp_down,
           norm_final):
    batch, seq, d = x.shape
    depth = w_in.shape[0]
    assert d == D_MODEL and seq % TM_PROJ == 0 and TM_PROJ % TK == 0 and TQ == TK
    assert seq // 4 >= TOPK_MAX
    cosv, sa, sb = _rope_tables(seq)
    h = x.reshape(batch * seq, d)
    wt16 = jnp.swapaxes(w_in, 1, 2).astype(BF16)
    tail_w = [w.astype(BF16) for w in (w_attn_out, w_conv_out, w_mix_out, w_mlp_up, w_mlp_down)]
    for l in range(depth):
        qT, k, vT, iqT, ikw, ikwT, uc, sg = _proj_call(
            h, norm_mix[l][None, :], _wprep_call(wt16, l), cosv, sa, sb, conv_w[l],
            batch=batch, seq=seq)
        attn = _attn_call(qT, k, vT, iqT, ikw, ikwT, batch=batch, seq=seq)
        h = _post_call(
            h, attn, uc, sg, tail_w,
            norm_mlp[l][None, :], norm_final[None, :], layer=l, final=(l == depth - 1))
    return h.reshape(batch, seq, d)
```
